```python
import jax, jax.numpy as jnp
from jax import lax
import numpy as np

D_MODEL = 1024
BATCH = 8
SEQ = 8192
DEPTH = 1
DEC_BATCH = 8
DEC_SEQ = 16
PAST_LEN = 4096

CHUNK = 64
Q_BLOCK = 128
EPS = 1e-6
D_FF = 2816
MLA_HEADS = 8
Q_LORA = 384
KV_LORA = 256
NOPE_DIM = 64
ROPE_DIM = 32
MLA_V_DIM = 64
ROPE_THETA = 10000.0
FOX_HEADS = 8
FOX_HEAD_DIM = 64
FOX_W = FOX_HEADS * FOX_HEAD_DIM
MIX_W = MLA_HEADS * MLA_V_DIM + FOX_W
IN_COLS = Q_LORA + KV_LORA + ROPE_DIM + 3 * FOX_W + FOX_HEADS

kernel_name = "mla_fox_macaron_streaming_step"


def _rmsnorm(x, g):
    x32 = x.astype(jnp.float32)
    y = x32 * lax.rsqrt(jnp.mean(x32 * x32, axis=-1, keepdims=True) + EPS)
    return (y * g.astype(jnp.float32)).astype(x.dtype)


def _swiglu(x, w_gu, w_down):
    gate, up = jnp.split(x @ w_gu, 2, axis=-1)
    return (jax.nn.silu(gate) * up) @ w_down


def _rope(x, pos):
    half = x.shape[-1] // 2
    freqs = ROPE_THETA ** (-jnp.arange(half, dtype=jnp.float32) / half)
    ang = pos.astype(jnp.float32)[:, None] * freqs[None, :]
    cos = jnp.cos(ang)[None, :, None, :]
    sin = jnp.sin(ang)[None, :, None, :]
    x1 = x[..., :half].astype(jnp.float32)
    x2 = x[..., half:].astype(jnp.float32)
    out = jnp.concatenate([x1 * cos - x2 * sin, x1 * sin + x2 * cos], axis=-1)
    return out.astype(x.dtype)


def _attend(q, k, v, q_pos, k_pos, per_frame, lq=None, lk=None):
    B, Tq, H, dk = q.shape
    qb = Q_BLOCK if Tq % Q_BLOCK == 0 else Tq
    nb = Tq // qb
    scale = dk ** -0.5
    k_chunk = k_pos // CHUNK
    decay = lq is not None
    lk_t = jnp.swapaxes(lk, 1, 2)[:, :, None, :] if decay else None

    def block(xs):
        qi, pi = xs[0], xs[1]
        s = jnp.einsum('bqhd,bkhd->bhqk', qi, k, preferred_element_type=jnp.float32) * scale
        if per_frame:
            mask = k_pos[None, :] <= pi[:, None]
        else:
            mask = k_chunk[None, :] <= (pi // CHUNK)[:, None]
        if decay:
            li = jnp.swapaxes(xs[2], 1, 2)[..., None]
            s = s + (li - lk_t)
        s = jnp.where(mask[None, None], s, -jnp.inf)
        p = jax.nn.softmax(s, axis=-1)
        return jnp.einsum('bhqk,bkhd->bqhd', p.astype(v.dtype), v)

    xs = (jnp.swapaxes(q.reshape(B, nb, qb, H, dk), 0, 1), q_pos.reshape(nb, qb))
    if decay:
        xs = xs + (jnp.swapaxes(lq.reshape(B, nb, qb, H), 0, 1),)
    o = lax.map(block, xs)
    return jnp.swapaxes(o, 0, 1).reshape(B, Tq, H, v.shape[-1])


def _token_mix(u, q_pos, past, w_in, b_forget, g_q_latent, w_q_up, g_kv_latent, w_kv_up, w_out):
    B, T, _ = u.shape
    proj = u @ w_in
    sizes = (Q_LORA, KV_LORA, ROPE_DIM, FOX_W, FOX_W, FOX_W, FOX_HEADS)
    idx = np.cumsum(sizes)[:-1].tolist()
    c_q, c_kv, k_rope, f_q, f_k, f_v, f_gate = jnp.split(proj, idx, axis=-1)

    q = (_rmsnorm(c_q, g_q_latent) @ w_q_up).reshape(B, T, MLA_HEADS, NOPE_DIM + ROPE_DIM)
    q = jnp.concatenate([q[..., :NOPE_DIM], _rope(q[..., NOPE_DIM:], q_pos)], axis=-1)
    c_kv = _rmsnorm(c_kv, g_kv_latent)
    k_rope = _rope(k_rope[:, :, None, :], q_pos)[:, :, 0, :]

    f_q = f_q.reshape(B, T, FOX_HEADS, FOX_HEAD_DIM)
    f_k = f_k.reshape(B, T, FOX_HEADS, FOX_HEAD_DIM)
    f_v = f_v.reshape(B, T, FOX_HEADS, FOX_HEAD_DIM)
    logf = jax.nn.log_sigmoid(f_gate.astype(jnp.float32) + b_forget.astype(jnp.float32))

    new_rows = (c_kv, k_rope, f_k, f_v, logf)
    if past is None:
        ckv_all, krope_all, fk_all, fv_all, logf_all = new_rows
        k_pos = q_pos
    else:
        p_ckv, p_krope, p_fk, p_fv, p_logf = past
        ckv_all = jnp.concatenate([p_ckv, c_kv], axis=1)
        krope_all = jnp.concatenate([p_krope, k_rope], axis=1)
        fk_all = jnp.concatenate([p_fk, f_k], axis=1)
        fv_all = jnp.concatenate([p_fv, f_v], axis=1)
        logf_all = jnp.concatenate([p_logf.astype(jnp.float32), logf], axis=1)
        k_pos = jnp.arange(ckv_all.shape[1])
    Tk = ckv_all.shape[1]

    kv = (ckv_all @ w_kv_up).reshape(B, Tk, MLA_HEADS, NOPE_DIM + MLA_V_DIM)
    k_mla = jnp.concatenate(
        [kv[..., :NOPE_DIM], jnp.broadcast_to(krope_all[:, :, None, :], (B, Tk, MLA_HEADS, ROPE_DIM))], axis=-1)
    o_mla = _attend(q, k_mla, kv[..., NOPE_DIM:], q_pos, k_pos, per_frame=False)

    lk = jnp.cumsum(logf_all, axis=1)
    lq = lk[:, Tk - T:]
    o_fox = _attend(f_q, fk_all, fv_all, q_pos, k_pos, True, lq, lk)

    o = jnp.concatenate([o_mla.reshape(B, T, -1), o_fox.reshape(B, T, -1)], axis=-1) @ w_out
    return o, new_rows


def setup_inputs(seed: int = 0) -> dict:
    key = jax.random.key(seed)
    ks = jax.random.split(key, 32)
    f32 = jnp.float32

    def nrm(k, shape, scale=1.0):
        return jax.random.normal(k, shape, f32) * scale

    def gain(k, n):
        return 1.0 + 0.02 * jax.random.normal(k, (DEPTH, n), f32)

    return {
        "x_prompt": nrm(ks[0], (BATCH, SEQ, D_MODEL)),
        "x_sample": nrm(ks[1], (DEC_BATCH, DEC_SEQ, D_MODEL)),
        "cache_mla_ckv": nrm(ks[2], (DEPTH, DEC_BATCH, PAST_LEN, KV_LORA)),
        "cache_mla_krope": nrm(ks[3], (DEPTH, DEC_BATCH, PAST_LEN, ROPE_DIM)),
        "cache_fox_k": nrm(ks[4], (DEPTH, DEC_BATCH, PAST_LEN, FOX_HEADS, FOX_HEAD_DIM)),
        "cache_fox_v": nrm(ks[5], (DEPTH, DEC_BATCH, PAST_LEN, FOX_HEADS, FOX_HEAD_DIM)),
        "cache_fox_logf": jax.nn.log_sigmoid(3.0 + nrm(ks[6], (DEPTH, DEC_BATCH, PAST_LEN, FOX_HEADS))),
        "g_ffn1_pre": gain(ks[7], D_MODEL),
        "g_ffn1_post": gain(ks[8], D_MODEL),
        "w_ffn1_gu": nrm(ks[9], (DEPTH, D_MODEL, 2 * D_FF), D_MODEL ** -0.5),
        "w_ffn1_down": nrm(ks[10], (DEPTH, D_FF, D_MODEL), D_FF ** -0.5),
        "g_mix_pre": gain(ks[11], D_MODEL),
        "g_mix_post": gain(ks[12], D_MODEL),
        "w_in": nrm(ks[13], (DEPTH, D_MODEL, IN_COLS), D_MODEL ** -0.5),
        "b_forget": 3.0 + 0.1 * nrm(ks[14], (DEPTH, FOX_HEADS)),
        "g_q_latent": gain(ks[15], Q_LORA),
        "w_q_up": nrm(ks[16], (DEPTH, Q_LORA, MLA_HEADS * (NOPE_DIM + ROPE_DIM)), Q_LORA ** -0.5),
        "g_kv_latent": gain(ks[17], KV_LORA),
        "w_kv_up": nrm(ks[18], (DEPTH, KV_LORA, MLA_HEADS * (NOPE_DIM + MLA_V_DIM)), KV_LORA ** -0.5),
        "w_out": nrm(ks[19], (DEPTH, MIX_W, D_MODEL), MIX_W ** -0.5),
        "g_ffn2_pre": gain(ks[20], D_MODEL),
        "g_ffn2_post": gain(ks[21], D_MODEL),
        "w_ffn2_gu": nrm(ks[22], (DEPTH, D_MODEL, 2 * D_FF), D_MODEL ** -0.5),
        "w_ffn2_down": nrm(ks[23], (DEPTH, D_FF, D_MODEL), D_FF ** -0.5),
    }


def reference(x_prompt, x_sample, cache_mla_ckv, cache_mla_krope, cache_fox_k, cache_fox_v, cache_fox_logf,
              g_ffn1_pre, g_ffn1_post, w_ffn1_gu, w_ffn1_down, g_mix_pre, g_mix_post, w_in, b_forget,
              g_q_latent, w_q_up, g_kv_latent, w_kv_up, w_out, g_ffn2_pre, g_ffn2_post, w_ffn2_gu, w_ffn2_down):
    past_len = cache_mla_ckv.shape[2]
    pos_p = jnp.arange(x_prompt.shape[1])
    pos_s = past_len + jnp.arange(x_sample.shape[1])

    xp, xs = x_prompt, x_sample
    rows_p, rows_s = [], []
    for l in range(DEPTH):
        def layer(x, pos, past):
            h = x + 0.5 * _rmsnorm(
                _swiglu(_rmsnorm(x, g_ffn1_pre[l]), w_ffn1_gu[l], w_ffn1_down[l]), g_ffn1_post[l])
            mix, rows = _token_mix(_rmsnorm(h, g_mix_pre[l]), pos, past, w_in[l], b_forget[l],
                                   g_q_latent[l], w_q_up[l], g_kv_latent[l], w_kv_up[l], w_out[l])
            h = h + _rmsnorm(mix, g_mix_post[l])
            h = h + 0.5 * _rmsnorm(
                _swiglu(_rmsnorm(h, g_ffn2_pre[l]), w_ffn2_gu[l], w_ffn2_down[l]), g_ffn2_post[l])
            return h, rows

        xp, rp = layer(xp, pos_p, None)
        xs, rs = layer(xs, pos_s, (cache_mla_ckv[l], cache_mla_krope[l], cache_fox_k[l],
                                   cache_fox_v[l], cache_fox_logf[l]))
        rows_p.append(rp)
        rows_s.append(rs)

    p_ckv, p_krope, p_fk, p_fv, p_logf = [jnp.stack([r[i] for r in rows_p]) for i in range(5)]
    s_ckv, s_krope, s_fk, s_fv, s_logf = [jnp.stack([r[i] for r in rows_s]) for i in range(5)]
    return (xp, xs, p_ckv, p_krope, p_fk, p_fv, p_logf, s_ckv, s_krope, s_fk, s_fv, s_logf)
```

```python
import functools

import jax
import jax.numpy as jnp
from jax import lax
from jax.experimental import pallas as pl
from jax.experimental.pallas import tpu as pltpu

EPS = 1e-6
CHUNK = 64
ROPE_THETA = 10000.0
HEADS = 8
NOPE_DIM = 64
ROPE_DIM = 32
HEAD_DIM = 64
Q_LORA = 384
KV_LORA = 256
LANES = 128
LOG2E = 1.4426950408889634
NEG = -1e30
LOGF_LANE0 = 8
VMEM_LIMIT = 56 * 1024 * 1024

BF16 = jnp.bfloat16
F32 = jnp.float32


def _rms(x, g):
    ms = jnp.mean(x * x, axis=-1, keepdims=True)
    return x * lax.rsqrt(ms + EPS) * g


def _const_spec(shape):
    return pl.BlockSpec(shape, lambda *_: (0,) * len(shape), pipeline_mode=pl.Buffered(1))


def _ffn_kernel(*refs, ff_chunk, with_mix):
    if with_mix:
        (h_ref, o1_ref, o2_ref, wo1_ref, wo2_ref, gmix_ref,
         gpre_ref, gpost_ref, wg_ref, wu_ref, wd_ref, out_ref) = refs
        mix = jnp.dot(o1_ref[...], wo1_ref[...], preferred_element_type=F32)
        mix = mix + jnp.dot(o2_ref[...], wo2_ref[...], preferred_element_type=F32)
        x = h_ref[...] + _rms(mix, gmix_ref[...])
    else:
        x_ref, gpre_ref, gpost_ref, wg_ref, wu_ref, wd_ref, out_ref = refs
        x = x_ref[...]
    n = _rms(x, gpre_ref[...]).astype(BF16)
    d_ff = wg_ref.shape[1]
    acc = jnp.zeros(x.shape, F32)
    for c in range(d_ff // ff_chunk):
        cols = slice(c * ff_chunk, (c + 1) * ff_chunk)
        gate = jnp.dot(n, wg_ref[:, cols], preferred_element_type=F32)
        up = jnp.dot(n, wu_ref[:, cols], preferred_element_type=F32)
        act = (gate * jax.nn.sigmoid(gate) * up).astype(BF16)
        acc = acc + jnp.dot(act, wd_ref[cols, :], preferred_element_type=F32)
    out_ref[...] = x + 0.5 * _rms(acc, gpost_ref[...])


def _ffn(x, g_pre, g_post, w_g, w_u, w_d, mix=None, *, tm):
    n, d = x.shape
    d_ff = w_g.shape[1]
    row = lambda w: pl.BlockSpec((tm, w), lambda i: (i, 0))
    in_specs = [row(d)]
    args = [x]
    if mix is not None:
        o1, o2, wo1, wo2, g_mix = mix
        in_specs += [row(o1.shape[1]), row(o2.shape[1]), _const_spec(wo1.shape), _const_spec(wo2.shape),
                     _const_spec((1, d))]
        args += [o1, o2, wo1, wo2, g_mix]
    in_specs += [_const_spec((1, d)), _const_spec((1, d)), _const_spec(w_g.shape), _const_spec(w_u.shape),
                 _const_spec(w_d.shape)]
    args += [g_pre, g_post, w_g, w_u, w_d]
    return pl.pallas_call(
        functools.partial(_ffn_kernel, ff_chunk=256, with_mix=mix is not None),
        out_shape=jax.ShapeDtypeStruct((n, d), F32),
        grid=(n // tm,),
        in_specs=in_specs,
        out_specs=row(d),
        compiler_params=pltpu.CompilerParams(dimension_semantics=("arbitrary",), vmem_limit_bytes=VMEM_LIMIT),
        name="ffn_mix" if mix is not None else "ffn",
    )(*args)


_C_CQ = 0
_C_CKV = _C_CQ + Q_LORA
_C_FQ = _C_CKV + KV_LORA
_C_FK = _C_FQ + HEADS * HEAD_DIM
_C_FV = _C_FK + HEADS * HEAD_DIM
_C_KR = _C_FV + HEADS * HEAD_DIM
_C_KRS = _C_KR + LANES
_C_GATE = _C_KRS + LANES
_C_END = _C_GATE + LANES


def _log_sigmoid(x):
    return jnp.minimum(x, 0.0) - jnp.log(1.0 + jnp.exp(-jnp.abs(x)))


def _proj_kernel(h_ref, cos_ref, sin_ref, gpre_ref, win_ref, bias_ref, gq_ref, wq_ref, gkv_ref, wkv_ref,
                 ckv_ref, krope_ref, fk_ref, fv_ref, logf_ref,
                 qn_ref, qr_ref, kn_ref, vm_ref, kr4_ref, fqb_ref, fkb_ref, fvb_ref, lf128_ref):
    hw = HEADS * HEAD_DIM
    u = _rms(h_ref[...], gpre_ref[...]).astype(BF16)
    proj = jnp.dot(u, win_ref[...], preferred_element_type=F32)
    cos = cos_ref[...]
    sin = sin_ref[...]

    cq = _rms(proj[:, _C_CQ:_C_CKV], gq_ref[...]).astype(BF16)
    q = jnp.dot(cq, wq_ref[...], preferred_element_type=F32)
    q_scale = (NOPE_DIM + ROPE_DIM) ** -0.5 * LOG2E
    rw = HEADS * ROPE_DIM
    qn_ref[...] = (q[:, :hw] * q_scale).astype(BF16)
    qr_ref[...] = ((q[:, hw:hw + rw] * cos + q[:, hw + rw:] * sin) * q_scale).astype(BF16)

    ckv = _rms(proj[:, _C_CKV:_C_FQ], gkv_ref[...])
    ckv_ref[...] = ckv
    kv = jnp.dot(ckv.astype(BF16), wkv_ref[...], preferred_element_type=F32)
    kn_ref[...] = kv[:, :hw].astype(BF16)
    vm_ref[...] = kv[:, hw:].astype(BF16)
    kr4 = proj[:, _C_KR:_C_KRS] * cos[:, :LANES] + proj[:, _C_KRS:_C_GATE] * sin[:, :LANES]
    krope_ref[...] = kr4[:, :ROPE_DIM]
    kr4_ref[...] = kr4.astype(BF16)

    fq = proj[:, _C_FQ:_C_FK]
    fk = proj[:, _C_FK:_C_FV]
    fv = proj[:, _C_FV:_C_KR]
    fqb_ref[...] = (fq * (HEAD_DIM ** -0.5 * LOG2E)).astype(BF16)
    fk_ref[...] = fk
    fv_ref[...] = fv
    fkb_ref[...] = fk.astype(BF16)
    fvb_ref[...] = fv.astype(BF16)
    logf = _log_sigmoid(proj[:, _C_GATE:_C_END] + bias_ref[...])
    logf_ref[...] = logf[:, :HEADS]
    lf128_ref[...] = logf


def _proj(h, cos, sin, g_pre, w_in, bias, g_q, w_q, g_kv, w_kv, *, tm):
    n, d = h.shape
    t_blocks = cos.shape[0] // tm
    hw = HEADS * HEAD_DIM
    row = lambda w: pl.BlockSpec((tm, w), lambda i: (i, 0))
    tab = pl.BlockSpec((tm, HEADS * ROPE_DIM), lambda i: (i % t_blocks, 0))
    widths = [(KV_LORA, F32), (ROPE_DIM, F32), (hw, F32), (hw, F32), (HEADS, F32),
              (hw, BF16), (HEADS * ROPE_DIM, BF16), (hw, BF16), (hw, BF16), (LANES, BF16),
              (hw, BF16), (hw, BF16), (hw, BF16), (LANES, F32)]
    return pl.pallas_call(
        _proj_kernel,
        out_shape=[jax.ShapeDtypeStruct((n, w), dt) for w, dt in widths],
        grid=(n // tm,),
        in_specs=[row(d), tab, tab, _const_spec((1, d)), _const_spec(w_in.shape), _const_spec((1, LANES)),
                  _const_spec((1, Q_LORA)), _const_spec(w_q.shape), _const_spec((1, KV_LORA)),
                  _const_spec(w_kv.shape)],
        out_specs=[row(w) for w, _ in widths],
        compiler_params=pltpu.CompilerParams(dimension_semantics=("arbitrary",), vmem_limit_bytes=VMEM_LIMIT),
        name="proj",
    )(h, cos, sin, g_pre, w_in, bias, g_q, w_q, g_kv, w_kv)


def _pastkv_kernel(ckv_ref, kr_ref, wkv_ref, kn_ref, vm_ref, kr4_ref):
    hw = HEADS * HEAD_DIM
    kv = jnp.dot(ckv_ref[...].astype(BF16), wkv_ref[...], preferred_element_type=F32)
    kn_ref[...] = kv[:, :hw].astype(BF16)
    vm_ref[...] = kv[:, hw:].astype(BF16)
    src = lax.broadcasted_iota(jnp.int32, (ROPE_DIM, LANES), 0)
    dst = lax.broadcasted_iota(jnp.int32, (ROPE_DIM, LANES), 1)
    rep = (dst % ROPE_DIM == src).astype(BF16)
    kr4_ref[...] = jnp.dot(kr_ref[...].astype(BF16), rep, preferred_element_type=F32).astype(BF16)


def _pastkv(ckv, krope, w_kv, *, tm):
    n = ckv.shape[0]
    hw = HEADS * HEAD_DIM
    row = lambda w: pl.BlockSpec((tm, w), lambda i: (i, 0))
    return pl.pallas_call(
        _pastkv_kernel,
        out_shape=[jax.ShapeDtypeStruct((n, hw), BF16), jax.ShapeDtypeStruct((n, hw), BF16),
                   jax.ShapeDtypeStruct((n, LANES), BF16)],
        grid=(n // tm,),
        in_specs=[row(KV_LORA), row(ROPE_DIM), _const_spec(w_kv.shape)],
        out_specs=[row(hw), row(hw), row(LANES)],
        compiler_params=pltpu.CompilerParams(dimension_semantics=("arbitrary",), vmem_limit_bytes=VMEM_LIMIT),
        name="pastkv",
    )(ckv, krope, w_kv)


def _split3(y, lane):
    hi = y.astype(BF16).astype(F32)
    r1 = y - hi
    mid = r1.astype(BF16).astype(F32)
    lo = r1 - mid
    j = (lane - LOGF_LANE0) % 3
    sel = jnp.where(j == 0, hi, jnp.where(j == 1, mid, lo))
    used = (lane >= LOGF_LANE0) & (lane < LOGF_LANE0 + 3 * HEADS)
    return jnp.where(used, sel, 0.0).astype(BF16)


def _lsplit_kernel(x_ref, o_ref, carry_ref, *, tc):
    @pl.when(pl.program_id(1) == 0)
    def _():
        carry_ref[...] = jnp.zeros_like(carry_ref)

    x = x_ref[0]
    hi = x.astype(BF16)
    r1 = x - hi.astype(F32)
    mid = r1.astype(BF16)
    lo = (r1 - mid.astype(F32)).astype(BF16)
    r = lax.broadcasted_iota(jnp.int32, (tc, tc), 0)
    c = lax.broadcasted_iota(jnp.int32, (tc, tc), 1)
    tri = (c <= r).astype(BF16)
    cum = (jnp.dot(tri, hi, preferred_element_type=F32) + jnp.dot(tri, mid, preferred_element_type=F32)
           + jnp.dot(tri, lo, preferred_element_type=F32)) + carry_ref[...]
    carry_ref[...] = cum[tc - 1:tc, :]
    lane = lax.broadcasted_iota(jnp.int32, (tc, LANES), 1)
    o_ref[0] = _split3(cum * (-LOG2E), lane)


def _lsplit(lf128, *, tc):
    b, t, _ = lf128.shape
    spec = pl.BlockSpec((1, tc, LANES), lambda i, j: (i, j, 0))
    return pl.pallas_call(
        functools.partial(_lsplit_kernel, tc=tc),
        out_shape=jax.ShapeDtypeStruct((b, t, LANES), BF16),
        grid=(b, t // tc),
        in_specs=[spec],
        out_specs=spec,
        scratch_shapes=[pltpu.VMEM((1, LANES), F32)],
        compiler_params=pltpu.CompilerParams(dimension_semantics=("arbitrary", "arbitrary")),
        name="lsplit",
    )(lf128)


def _attn_kernel(*refs, tq, tk, q_off, kv_valid, fox):
    if fox:
        q1_ref, k1_ref, k2_ref, v_ref, o_ref, m_ref, l_ref, acc_ref = refs
    else:
        q1_ref, q2_ref, k1_ref, k2_ref, v_ref, o_ref, m_ref, l_ref, acc_ref = refs
    p = pl.program_id(1)
    qi = pl.program_id(2)
    lane = lax.broadcasted_iota(jnp.int32, (tq, LANES), 1)
    q1 = q1_ref[0].astype(F32)
    q2 = jnp.ones_like(q1) if fox else q2_ref[0].astype(F32)

    def head_rows(a):
        main = jnp.where((lane >= a * HEAD_DIM) & (lane < (a + 1) * HEAD_DIM), q1, 0.0)
        if fox:
            lo, width = LOGF_LANE0 + 3 * (2 * p + a), 3
        else:
            lo, width = ROPE_DIM * (2 * (p % 2) + a), ROPE_DIM
        aux = jnp.where((lane >= lo) & (lane < lo + width), q2, 0.0)
        return jnp.concatenate([main, aux], axis=1).astype(BF16)

    q = jnp.concatenate([head_rows(0), head_rows(1)], axis=0)

    m_ref[...] = jnp.full(m_ref.shape, NEG, F32)
    l_ref[...] = jnp.zeros(l_ref.shape, F32)
    acc_ref[...] = jnp.zeros(acc_ref.shape, F32)

    q_start = q_off + qi * tq

    def step(kb, masked):
        ks = pl.multiple_of(kb * tk, tk)
        k = jnp.concatenate([k1_ref[0, pl.ds(ks, tk), :].astype(BF16),
                             k2_ref[0, pl.ds(ks, tk), :].astype(BF16)], axis=1)
        s = lax.dot_general(q, k, (((1,), (1,)), ((), ())), preferred_element_type=F32)
        if masked:
            r = lax.broadcasted_iota(jnp.int32, s.shape, 0)
            t_pos = q_start + jnp.where(r >= tq, r - tq, r)
            s_pos = ks + lax.broadcasted_iota(jnp.int32, s.shape, 1)
            if fox:
                vis = s_pos <= t_pos
            else:
                vis = (s_pos // CHUNK) <= (t_pos // CHUNK)
            s = jnp.where(vis & (s_pos < kv_valid), s, NEG)
        m_prev = m_ref[...]
        m_new = jnp.maximum(m_prev, jnp.max(s, axis=1, keepdims=True))
        alpha = jnp.exp2(m_prev - m_new)
        pexp = jnp.exp2(s - m_new)
        l_ref[...] = alpha * l_ref[...] + jnp.sum(pexp, axis=1, keepdims=True)
        pv = jnp.dot(pexp.astype(BF16), v_ref[0, pl.ds(ks, tk), :].astype(BF16), preferred_element_type=F32)
        acc_ref[...] = alpha * acc_ref[...] + pv
        m_ref[...] = m_new

    if fox:
        hi = q_start + tq
    else:
        hi = ((q_start + tq - 1) // CHUNK + 1) * CHUNK
    hi = jnp.minimum(hi, kv_valid)
    n_blocks = (hi + tk - 1) // tk
    n_full = q_start // tk

    def full_body(kb, carry):
        step(kb, False)
        return carry

    def masked_body(kb, carry):
        step(kb, True)
        return carry

    lax.fori_loop(0, n_full, full_body, 0)
    lax.fori_loop(n_full, n_blocks, masked_body, 0)

    out = acc_ref[...] / l_ref[...]
    o_ref[0] = jnp.where(lane < HEAD_DIM, out[:tq], out[tq:]).astype(o_ref.dtype)


def _attn(q1, q2, k1, k2, v, *, tq, tk, q_off, kv_valid, fox):
    b, t_q, hw = q1.shape
    t_k = k1.shape[1]
    pairs = hw // LANES
    qspec = pl.BlockSpec((1, tq, LANES), lambda bi, p, qi: (bi, qi, p))
    kspec = pl.BlockSpec((1, t_k, LANES), lambda bi, p, qi: (bi, 0, p))
    k2spec = pl.BlockSpec((1, t_k, LANES), lambda bi, p, qi: (bi, 0, 0))
    if fox:
        in_specs = [qspec, kspec, k2spec, kspec]
        args = (q1, k1, k2, v)
    else:
        q2spec = pl.BlockSpec((1, tq, LANES), lambda bi, p, qi: (bi, qi, p // 2))
        in_specs = [qspec, q2spec, kspec, k2spec, kspec]
        args = (q1, q2, k1, k2, v)
    return pl.pallas_call(
        functools.partial(_attn_kernel, tq=tq, tk=tk, q_off=q_off, kv_valid=kv_valid, fox=fox),
        out_shape=jax.ShapeDtypeStruct((b, t_q, hw), BF16),
        grid=(b, pairs, t_q // tq),
        in_specs=in_specs,
        out_specs=qspec,
        scratch_shapes=[pltpu.VMEM((2 * tq, 1), F32), pltpu.VMEM((2 * tq, 1), F32),
                        pltpu.VMEM((2 * tq, LANES), F32)],
        compiler_params=pltpu.CompilerParams(dimension_semantics=("arbitrary",) * 3,
                                             vmem_limit_bytes=VMEM_LIMIT),
        name="attn_fox" if fox else "attn_mla",
    )(*args)


def _rope_tables(pos):
    half = ROPE_DIM // 2
    freqs = ROPE_THETA ** (-jnp.arange(half, dtype=F32) / half)
    ang = pos.astype(F32)[:, None] * freqs[None, :]
    cos, sin = jnp.cos(ang), jnp.sin(ang)
    cos_t = jnp.tile(jnp.concatenate([cos, cos], axis=1), (1, HEADS))
    sin_t = jnp.tile(jnp.concatenate([-sin, sin], axis=1), (1, HEADS))
    return cos_t, sin_t


def _swap_halves(w):
    half = w.shape[-1] // 2
    return jnp.concatenate([w[..., half:], w[..., :half]], axis=-1)


def _prep_weights(w_in, b_forget, w_q_up, w_kv_up, w_out):
    d = w_in.shape[0]
    hw = HEADS * HEAD_DIM
    o = 0
    cq = w_in[:, o:o + Q_LORA]; o += Q_LORA
    ckv = w_in[:, o:o + KV_LORA]; o += KV_LORA
    kr = w_in[:, o:o + ROPE_DIM]; o += ROPE_DIM
    fq = w_in[:, o:o + hw]; o += hw
    fk = w_in[:, o:o + hw]; o += hw
    fv = w_in[:, o:o + hw]; o += hw
    gate = w_in[:, o:o + HEADS]
    reps = LANES // ROPE_DIM
    pad = LANES - 4 * HEADS
    gate128 = jnp.concatenate([gate, jnp.repeat(gate, 3, axis=1), jnp.zeros((d, pad), w_in.dtype)], axis=1)
    w_in_p = jnp.concatenate([cq, ckv, fq, fk, fv, jnp.tile(kr, (1, reps)), jnp.tile(_swap_halves(kr), (1, reps)),
                              gate128], axis=1).astype(BF16)
    bias128 = jnp.concatenate([b_forget, jnp.repeat(b_forget, 3), jnp.zeros((pad,), F32)])[None, :]
    wq = w_q_up.reshape(Q_LORA, HEADS, NOPE_DIM + ROPE_DIM)
    wq_rope = wq[:, :, NOPE_DIM:]
    w_q_p = jnp.concatenate([wq[:, :, :NOPE_DIM].reshape(Q_LORA, -1), wq_rope.reshape(Q_LORA, -1),
                             _swap_halves(wq_rope).reshape(Q_LORA, -1)], axis=1).astype(BF16)
    wkv = w_kv_up.reshape(KV_LORA, HEADS, NOPE_DIM + HEAD_DIM)
    w_kv_p = jnp.concatenate([wkv[:, :, :NOPE_DIM].reshape(KV_LORA, -1), wkv[:, :, NOPE_DIM:].reshape(KV_LORA, -1)],
                             axis=1).astype(BF16)
    w_o1 = w_out[:hw].astype(BF16)
    w_o2 = w_out[hw:].astype(BF16)
    return w_in_p, bias128, w_q_p, w_kv_p, w_o1, w_o2


def _expand_logf(lf):
    pad = LANES - 4 * HEADS
    return jnp.concatenate([lf, jnp.repeat(lf, 3, axis=-1), jnp.zeros(lf.shape[:-1] + (pad,), lf.dtype)], axis=-1)


def kernel(x_prompt, x_sample, cache_mla_ckv, cache_mla_krope, cache_fox_k, cache_fox_v, cache_fox_logf,
           g_ffn1_pre, g_ffn1_post, w_ffn1_gu, w_ffn1_down, g_mix_pre, g_mix_post, w_in, b_forget,
           g_q_latent, w_q_up, g_kv_latent, w_kv_up, w_out, g_ffn2_pre, g_ffn2_post, w_ffn2_gu, w_ffn2_down):
    depth = w_in.shape[0]
    bp, tp, d = x_prompt.shape
    bs, ts, _ = x_sample.shape
    past = cache_mla_ckv.shape[2]
    hw = HEADS * HEAD_DIM
    tq_p = tk_p = 256
    tk_s = 256
    tk_pad = -(-(past + ts) // tk_s) * tk_s

    cos_p, sin_p = _rope_tables(jnp.arange(tp))
    cos_s, sin_s = _rope_tables(past + jnp.arange(ts))
    cos_s, sin_s = jnp.tile(cos_s, (bs, 1)), jnp.tile(sin_s, (bs, 1))

    xp = x_prompt.reshape(bp * tp, d)
    xs = x_sample.reshape(bs * ts, d)
    tm_p = 512
    tm_s = bs * ts
    rows_p, rows_s = [], []

    def pad_keys(parts):
        n = sum(a.shape[1] for a in parts)
        parts = list(parts) + [jnp.zeros((bs, tk_pad - n, parts[0].shape[2]), parts[0].dtype)]
        return jnp.concatenate(parts, axis=1)

    for l in range(depth):
        d_ff = w_ffn1_down.shape[1]
        w1 = (w_ffn1_gu[l][:, :d_ff].astype(BF16), w_ffn1_gu[l][:, d_ff:].astype(BF16), w_ffn1_down[l].astype(BF16))
        w2 = (w_ffn2_gu[l][:, :d_ff].astype(BF16), w_ffn2_gu[l][:, d_ff:].astype(BF16), w_ffn2_down[l].astype(BF16))
        w_in_p, bias128, w_q_p, w_kv_p, w_o1, w_o2 = _prep_weights(w_in[l], b_forget[l], w_q_up[l], w_kv_up[l], w_out[l])
        g1 = (g_ffn1_pre[l][None, :], g_ffn1_post[l][None, :])
        g2 = (g_ffn2_pre[l][None, :], g_ffn2_post[l][None, :])
        gm_pre, gm_post = g_mix_pre[l][None, :], g_mix_post[l][None, :]
        gq, gkv = g_q_latent[l][None, :], g_kv_latent[l][None, :]

        hp = _ffn(xp, *g1, *w1, tm=tm_p)
        (ckv, krope, fk, fv, logf, qn, qr, kn, vm, kr4, fqb, fkb, fvb, lf128) = _proj(
            hp, cos_p, sin_p, gm_pre, w_in_p, bias128, gq, w_q_p, gkv, w_kv_p, tm=tm_p)
        sh = lambda a: a.reshape(bp, tp, a.shape[-1])
        ls = _lsplit(sh(lf128), tc=256)
        o_mla = _attn(sh(qn), sh(qr), sh(kn), sh(kr4), sh(vm), tq=tq_p, tk=tk_p, q_off=0, kv_valid=tp, fox=False)
        o_fox = _attn(sh(fqb), None, sh(fkb), ls, sh(fvb), tq=tq_p, tk=tk_p, q_off=0, kv_valid=tp, fox=True)
        xp = _ffn(hp, *g2, *w2, mix=(o_mla.reshape(-1, hw), o_fox.reshape(-1, hw), w_o1, w_o2, gm_post), tm=tm_p)
        rows_p.append((ckv.reshape(bp, tp, KV_LORA), krope.reshape(bp, tp, ROPE_DIM),
                       fk.reshape(bp, tp, HEADS, HEAD_DIM), fv.reshape(bp, tp, HEADS, HEAD_DIM),
                       logf.reshape(bp, tp, HEADS)))

        hs = _ffn(xs, *g1, *w1, tm=tm_s)
        (ckv, krope, fk, fv, logf, qn, qr, kn, vm, kr4, fqb, _, _, lf128) = _proj(
            hs, cos_s, sin_s, gm_pre, w_in_p, bias128, gq, w_q_p, gkv, w_kv_p, tm=tm_s)
        kn_past, vm_past, kr4_past = _pastkv(cache_mla_ckv[l].reshape(bs * past, KV_LORA),
                                             cache_mla_krope[l].reshape(bs * past, ROPE_DIM), w_kv_p, tm=512)
        sh = lambda a: a.reshape(bs, -1, a.shape[-1])
        kn_all = pad_keys([sh(kn_past), sh(kn)])
        vm_all = pad_keys([sh(vm_past), sh(vm)])
        kr4_all = pad_keys([sh(kr4_past), sh(kr4)])
        fk_all = pad_keys([cache_fox_k[l].reshape(bs, past, hw), sh(fk)])
        fv_all = pad_keys([cache_fox_v[l].reshape(bs, past, hw), sh(fv)])
        lf_all = pad_keys([_expand_logf(cache_fox_logf[l].astype(F32)), sh(lf128)])
        ls = _lsplit(lf_all, tc=256)
        kw = dict(tq=ts, tk=tk_s, q_off=past, kv_valid=past + ts)
        o_mla = _attn(sh(qn), sh(qr), kn_all, kr4_all, vm_all, fox=False, **kw)
        o_fox = _attn(sh(fqb), None, fk_all, ls, fv_all, fox=True, **kw)
        xs = _ffn(hs, *g2, *w2, mix=(o_mla.reshape(-1, hw), o_fox.reshape(-1, hw), w_o1, w_o2, gm_post), tm=tm_s)
        rows_s.append((ckv.reshape(bs, ts, KV_LORA), krope.reshape(bs, ts, ROPE_DIM),
                       fk.reshape(bs, ts, HEADS, HEAD_DIM), fv.reshape(bs, ts, HEADS, HEAD_DIM),
                       logf.reshape(bs, ts, HEADS)))

    outs_p = [jnp.stack([r[i] for r in rows_p]) for i in range(5)]
    outs_s = [jnp.stack([r[i] for r in rows_s]) for i in range(5)]
    return (xp.reshape(bp, tp, d), xs.reshape(bs, ts, d), *outs_p, *outs_s)
```

```python
import functools

import jax
import jax.numpy as jnp
from jax import lax
from jax.experimental import pallas as pl
from jax.experimental.pallas import tpu as pltpu

EPS = 1e-6
CHUNK = 64
ROPE_THETA = 10000.0
HEADS = 8
NOPE_DIM = 64
ROPE_DIM = 32
HEAD_DIM = 64
Q_LORA = 384
KV_LORA = 256
LANES = 128
LOG2E = 1.4426950408889634
NEG = -1e30
LOGF_LANE0 = 8
VMEM_LIMIT = 56 * 1024 * 1024

BF16 = jnp.bfloat16
F32 = jnp.float32


def _rms(x, g):
    ms = jnp.mean(x * x, axis=-1, keepdims=True)
    return x * lax.rsqrt(ms + EPS) * g


def _const_spec(shape):
    return pl.BlockSpec(shape, lambda *_: (0,) * len(shape), pipeline_mode=pl.Buffered(1))


def _ffn_kernel(*refs, ff_chunk, with_mix):
    if with_mix:
        (h_ref, o1_ref, o2_ref, wo1_ref, wo2_ref, gmix_ref,
         gpre_ref, gpost_ref, wg_ref, wu_ref, wd_ref, out_ref) = refs
        mix = jnp.dot(o1_ref[...], wo1_ref[...], preferred_element_type=F32)
        mix = mix + jnp.dot(o2_ref[...], wo2_ref[...], preferred_element_type=F32)
        x = h_ref[...] + _rms(mix, gmix_ref[...])
    else:
        x_ref, gpre_ref, gpost_ref, wg_ref, wu_ref, wd_ref, out_ref = refs
        x = x_ref[...]
    n = _rms(x, gpre_ref[...]).astype(BF16)
    d_ff = wg_ref.shape[1]
    acc = jnp.zeros(x.shape, F32)
    for c in range(d_ff // ff_chunk):
        cols = slice(c * ff_chunk, (c + 1) * ff_chunk)
        gate = jnp.dot(n, wg_ref[:, cols], preferred_element_type=F32)
        up = jnp.dot(n, wu_ref[:, cols], preferred_element_type=F32)
        act = (gate * jax.nn.sigmoid(gate) * up).astype(BF16)
        acc = acc + jnp.dot(act, wd_ref[cols, :], preferred_element_type=F32)
    out_ref[...] = x + 0.5 * _rms(acc, gpost_ref[...])


def _ffn(x, g_pre, g_post, w_g, w_u, w_d, mix=None, *, tm):
    n, d = x.shape
    d_ff = w_g.shape[1]
    row = lambda w: pl.BlockSpec((tm, w), lambda i: (i, 0))
    in_specs = [row(d)]
    args = [x]
    if mix is not None:
        o1, o2, wo1, wo2, g_mix = mix
        in_specs += [row(o1.shape[1]), row(o2.shape[1]), _const_spec(wo1.shape), _const_spec(wo2.shape),
                     _const_spec((1, d))]
        args += [o1, o2, wo1, wo2, g_mix]
    in_specs += [_const_spec((1, d)), _const_spec((1, d)), _const_spec(w_g.shape), _const_spec(w_u.shape),
                 _const_spec(w_d.shape)]
    args += [g_pre, g_post, w_g, w_u, w_d]
    return pl.pallas_call(
        functools.partial(_ffn_kernel, ff_chunk=256, with_mix=mix is not None),
        out_shape=jax.ShapeDtypeStruct((n, d), F32),
        grid=(n // tm,),
        in_specs=in_specs,
        out_specs=row(d),
        compiler_params=pltpu.CompilerParams(dimension_semantics=("arbitrary",), vmem_limit_bytes=VMEM_LIMIT),
        name="ffn_mix" if mix is not None else "ffn",
    )(*args)


_C_CQ = 0
_C_CKV = _C_CQ + Q_LORA
_C_FQ = _C_CKV + KV_LORA
_C_FK = _C_FQ + HEADS * HEAD_DIM
_C_FV = _C_FK + HEADS * HEAD_DIM
_C_KR = _C_FV + HEADS * HEAD_DIM
_C_KRS = _C_KR + LANES
_C_GATE = _C_KRS + LANES
_C_END = _C_GATE + LANES


def _log_sigmoid(x):
    return jnp.minimum(x, 0.0) - jnp.log(1.0 + jnp.exp(-jnp.abs(x)))


def _proj_kernel(h_ref, cos_ref, sin_ref, gpre_ref, win_ref, bias_ref, gq_ref, wq_ref, gkv_ref, wkv_ref,
                 ckv_ref, krope_ref, fk_ref, fv_ref, logf_ref,
                 qn_ref, qr_ref, kn_ref, vm_ref, kr4_ref, fqb_ref, fkb_ref, fvb_ref, lf128_ref):
    hw = HEADS * HEAD_DIM
    u = _rms(h_ref[...], gpre_ref[...]).astype(BF16)
    proj = jnp.dot(u, win_ref[...], preferred_element_type=F32)
    cos = cos_ref[...]
    sin = sin_ref[...]

    cq = _rms(proj[:, _C_CQ:_C_CKV], gq_ref[...]).astype(BF16)
    q = jnp.dot(cq, wq_ref[...], preferred_element_type=F32)
    q_scale = (NOPE_DIM + ROPE_DIM) ** -0.5 * LOG2E
    rw = HEADS * ROPE_DIM
    qn_ref[...] = (q[:, :hw] * q_scale).astype(BF16)
    qr_ref[...] = ((q[:, hw:hw + rw] * cos + q[:, hw + rw:] * sin) * q_scale).astype(BF16)

    ckv = _rms(proj[:, _C_CKV:_C_FQ], gkv_ref[...])
    ckv_ref[...] = ckv
    kv = jnp.dot(ckv.astype(BF16), wkv_ref[...], preferred_element_type=F32)
    kn_ref[...] = kv[:, :hw].astype(BF16)
    vm_ref[...] = kv[:, hw:].astype(BF16)
    kr4 = proj[:, _C_KR:_C_KRS] * cos[:, :LANES] + proj[:, _C_KRS:_C_GATE] * sin[:, :LANES]
    krope_ref[...] = kr4[:, :ROPE_DIM]
    kr4_ref[...] = kr4.astype(BF16)

    fq = proj[:, _C_FQ:_C_FK]
    fk = proj[:, _C_FK:_C_FV]
    fv = proj[:, _C_FV:_C_KR]
    fqb_ref[...] = (fq * (HEAD_DIM ** -0.5 * LOG2E)).astype(BF16)
    fk_ref[...] = fk
    fv_ref[...] = fv
    fkb_ref[...] = fk.astype(BF16)
    fvb_ref[...] = fv.astype(BF16)
    logf = _log_sigmoid(proj[:, _C_GATE:_C_END] + bias_ref[...])
    logf_ref[...] = logf[:, :HEADS]
    lf128_ref[...] = logf


def _proj(h, cos, sin, g_pre, w_in, bias, g_q, w_q, g_kv, w_kv, *, tm):
    n, d = h.shape
    t_blocks = cos.shape[0] // tm
    hw = HEADS * HEAD_DIM
    row = lambda w: pl.BlockSpec((tm, w), lambda i: (i, 0))
    tab = pl.BlockSpec((tm, HEADS * ROPE_DIM), lambda i: (i % t_blocks, 0))
    widths = [(KV_LORA, F32), (ROPE_DIM, F32), (hw, F32), (hw, F32), (HEADS, F32),
              (hw, BF16), (HEADS * ROPE_DIM, BF16), (hw, BF16), (hw, BF16), (LANES, BF16),
              (hw, BF16), (hw, BF16), (hw, BF16), (LANES, F32)]
    return pl.pallas_call(
        _proj_kernel,
        out_shape=[jax.ShapeDtypeStruct((n, w), dt) for w, dt in widths],
        grid=(n // tm,),
        in_specs=[row(d), tab, tab, _const_spec((1, d)), _const_spec(w_in.shape), _const_spec((1, LANES)),
                  _const_spec((1, Q_LORA)), _const_spec(w_q.shape), _const_spec((1, KV_LORA)),
                  _const_spec(w_kv.shape)],
        out_specs=[row(w) for w, _ in widths],
        compiler_params=pltpu.CompilerParams(dimension_semantics=("arbitrary",), vmem_limit_bytes=VMEM_LIMIT),
        name="proj",
    )(h, cos, sin, g_pre, w_in, bias, g_q, w_q, g_kv, w_kv)


def _pastkv_kernel(ckv_ref, kr_ref, wkv_ref, kn_ref, vm_ref, kr4_ref):
    hw = HEADS * HEAD_DIM
    kv = jnp.dot(ckv_ref[...].astype(BF16), wkv_ref[...], preferred_element_type=F32)
    kn_ref[...] = kv[:, :hw].astype(BF16)
    vm_ref[...] = kv[:, hw:].astype(BF16)
    src = lax.broadcasted_iota(jnp.int32, (ROPE_DIM, LANES), 0)
    dst = lax.broadcasted_iota(jnp.int32, (ROPE_DIM, LANES), 1)
    rep = (dst % ROPE_DIM == src).astype(BF16)
    kr4_ref[...] = jnp.dot(kr_ref[...].astype(BF16), rep, preferred_element_type=F32).astype(BF16)


def _pastkv(ckv, krope, w_kv, *, tm):
    n = ckv.shape[0]
    hw = HEADS * HEAD_DIM
    row = lambda w: pl.BlockSpec((tm, w), lambda i: (i, 0))
    return pl.pallas_call(
        _pastkv_kernel,
        out_shape=[jax.ShapeDtypeStruct((n, hw), BF16), jax.ShapeDtypeStruct((n, hw), BF16),
                   jax.ShapeDtypeStruct((n, LANES), BF16)],
        grid=(n // tm,),
        in_specs=[row(KV_LORA), row(ROPE_DIM), _const_spec(w_kv.shape)],
        out_specs=[row(hw), row(hw), row(LANES)],
        compiler_params=pltpu.CompilerParams(dimension_semantics=("arbitrary",), vmem_limit_bytes=VMEM_LIMIT),
        name="pastkv",
    )(ckv, krope, w_kv)


def _split3(y, lane):
    hi = y.astype(BF16).astype(F32)
    r1 = y - hi
    mid = r1.astype(BF16).astype(F32)
    lo = r1 - mid
    j = (lane - LOGF_LANE0) % 3
    sel = jnp.where(j == 0, hi, jnp.where(j == 1, mid, lo))
    used = (lane >= LOGF_LANE0) & (lane < LOGF_LANE0 + 3 * HEADS)
    return jnp.where(used, sel, 0.0).astype(BF16)


def _lsplit_kernel(x_ref, o_ref, carry_ref, *, tc):
    @pl.when(pl.program_id(1) == 0)
    def _():
        carry_ref[...] = jnp.zeros_like(carry_ref)

    x = x_ref[0]
    hi = x.astype(BF16)
    r1 = x - hi.astype(F32)
    mid = r1.astype(BF16)
    lo = (r1 - mid.astype(F32)).astype(BF16)
    r = lax.broadcasted_iota(jnp.int32, (tc, tc), 0)
    c = lax.broadcasted_iota(jnp.int32, (tc, tc), 1)
    tri = (c <= r).astype(BF16)
    cum = (jnp.dot(tri, hi, preferred_element_type=F32) + jnp.dot(tri, mid, preferred_element_type=F32)
           + jnp.dot(tri, lo, preferred_element_type=F32)) + carry_ref[...]
    carry_ref[...] = cum[tc - 1:tc, :]
    lane = lax.broadcasted_iota(jnp.int32, (tc, LANES), 1)
    o_ref[0] = _split3(cum * (-LOG2E), lane)


def _lsplit(lf128, *, tc):
    b, t, _ = lf128.shape
    spec = pl.BlockSpec((1, tc, LANES), lambda i, j: (i, j, 0))
    return pl.pallas_call(
        functools.partial(_lsplit_kernel, tc=tc),
        out_shape=jax.ShapeDtypeStruct((b, t, LANES), BF16),
        grid=(b, t // tc),
        in_specs=[spec],
        out_specs=spec,
        scratch_shapes=[pltpu.VMEM((1, LANES), F32)],
        compiler_params=pltpu.CompilerParams(dimension_semantics=("arbitrary", "arbitrary")),
        name="lsplit",
    )(lf128)


def _attn_kernel(*refs, tq, tk, q_off, kv_valid, fox):
    if fox:
        q1_ref, k1_ref, k2_ref, v_ref, o_ref, m_ref, l_ref, acc_ref = refs
    else:
        q1_ref, q2_ref, k1_ref, k2_ref, v_ref, o_ref, m_ref, l_ref, acc_ref = refs
    p = pl.program_id(1)
    qi = pl.program_id(2)
    lane = lax.broadcasted_iota(jnp.int32, (tq, LANES), 1)
    q1 = q1_ref[0].astype(F32)
    q2 = jnp.ones_like(q1) if fox else q2_ref[0].astype(F32)

    def head_rows(a):
        main = jnp.where((lane >= a * HEAD_DIM) & (lane < (a + 1) * HEAD_DIM), q1, 0.0)
        if fox:
            lo, width = LOGF_LANE0 + 3 * (2 * p + a), 3
        else:
            lo, width = ROPE_DIM * (2 * (p % 2) + a), ROPE_DIM
        aux = jnp.where((lane >= lo) & (lane < lo + width), q2, 0.0)
        return jnp.concatenate([main, aux], axis=1).astype(BF16)

    q = jnp.concatenate([head_rows(0), head_rows(1)], axis=0)

    m_ref[...] = jnp.full(m_ref.shape, NEG, F32)
    l_ref[...] = jnp.zeros(l_ref.shape, F32)
    acc_ref[...] = jnp.zeros(acc_ref.shape, F32)

    q_start = q_off + qi * tq

    def step(kb, masked):
        ks = pl.multiple_of(kb * tk, tk)
        k = jnp.concatenate([k1_ref[0, pl.ds(ks, tk), :].astype(BF16),
                             k2_ref[0, pl.ds(ks, tk), :].astype(BF16)], axis=1)
        s = lax.dot_general(q, k, (((1,), (1,)), ((), ())), preferred_element_type=F32)
        if masked:
            r = lax.broadcasted_iota(jnp.int32, s.shape, 0)
            t_pos = q_start + jnp.where(r >= tq, r - tq, r)
            s_pos = ks + lax.broadcasted_iota(jnp.int32, s.shape, 1)
            if fox:
                vis = s_pos <= t_pos
            else:
                vis = (s_pos // CHUNK) <= (t_pos // CHUNK)
            s = jnp.where(vis & (s_pos < kv_valid), s, NEG)
        m_prev = m_ref[...]
        m_new = jnp.maximum(m_prev, jnp.max(s, axis=1, keepdims=True))
        alpha = jnp.exp2(m_prev - m_new)
        pexp = jnp.exp2(s - m_new)
        l_ref[...] = alpha * l_ref[...] + jnp.sum(pexp, axis=1, keepdims=True)
        pv = jnp.dot(pexp.astype(BF16), v_ref[0, pl.ds(ks, tk), :].astype(BF16), preferred_element_type=F32)
        acc_ref[...] = alpha * acc_ref[...] + pv
        m_ref[...] = m_new

    if fox:
        hi = q_start + tq
    else:
        hi = ((q_start + tq - 1) // CHUNK + 1) * CHUNK
    hi = jnp.minimum(hi, kv_valid)
    n_blocks = (hi + tk - 1) // tk
    n_full = q_start // tk

    def full_body(kb, carry):
        step(kb, False)
        return carry

    def masked_body(kb, carry):
        step(kb, True)
        return carry

    lax.fori_loop(0, n_full, full_body, 0)
    lax.fori_loop(n_full, n_blocks, masked_body, 0)

    out = acc_ref[...] / l_ref[...]
    o_ref[0] = jnp.where(lane < HEAD_DIM, out[:tq], out[tq:]).astype(o_ref.dtype)


def _attn(q1, q2, k1, k2, v, *, tq, tk, q_off, kv_valid, fox):
    b, t_q, hw = q1.shape
    t_k = k1.shape[1]
    pairs = hw // LANES
    qspec = pl.BlockSpec((1, tq, LANES), lambda bi, p, qi: (bi, qi, p))
    kspec = pl.BlockSpec((1, t_k, LANES), lambda bi, p, qi: (bi, 0, p))
    k2spec = pl.BlockSpec((1, t_k, LANES), lambda bi, p, qi: (bi, 0, 0))
    if fox:
        in_specs = [qspec, kspec, k2spec, kspec]
        args = (q1, k1, k2, v)
    else:
        q2spec = pl.BlockSpec((1, tq, LANES), lambda bi, p, qi: (bi, qi, p // 2))
        in_specs = [qspec, q2spec, kspec, k2spec, kspec]
        args = (q1, q2, k1, k2, v)
    return pl.pallas_call(
        functools.partial(_attn_kernel, tq=tq, tk=tk, q_off=q_off, kv_valid=kv_valid, fox=fox),
        out_shape=jax.ShapeDtypeStruct((b, t_q, hw), BF16),
        grid=(b, pairs, t_q // tq),
        in_specs=in_specs,
        out_specs=qspec,
        scratch_shapes=[pltpu.VMEM((2 * tq, 1), F32), pltpu.VMEM((2 * tq, 1), F32),
                        pltpu.VMEM((2 * tq, LANES), F32)],
        compiler_params=pltpu.CompilerParams(dimension_semantics=("arbitrary",) * 3,
                                             vmem_limit_bytes=VMEM_LIMIT),
        name="attn_fox" if fox else "attn_mla",
    )(*args)


def _attn_t_kernel(*refs, tq, tk, fox):
    if fox:
        q1_ref, k1_ref, k2_ref, v_ref, o_ref, m_ref, l_ref, acc_ref = refs
    else:
        q1_ref, q2_ref, k1_ref, k2_ref, v_ref, o_ref, m_ref, l_ref, acc_ref = refs
    p = pl.program_id(1)
    qi = pl.program_id(2)
    lane = lax.broadcasted_iota(jnp.int32, (tq, LANES), 1)
    q1 = q1_ref[0].astype(F32)
    q2 = jnp.ones_like(q1) if fox else q2_ref[0].astype(F32)

    def head_rows(a):
        main = jnp.where((lane >= a * HEAD_DIM) & (lane < (a + 1) * HEAD_DIM), q1, 0.0)
        if fox:
            lo, width = LOGF_LANE0 + 3 * (2 * p + a), 3
        else:
            lo, width = ROPE_DIM * (2 * (p % 2) + a), ROPE_DIM
        aux = jnp.where((lane >= lo) & (lane < lo + width), q2, 0.0)
        return jnp.concatenate([main, aux], axis=1)

    q_t = jnp.concatenate([head_rows(0), head_rows(1)], axis=0).T.astype(BF16)

    m_ref[...] = jnp.full(m_ref.shape, NEG, F32)
    l_ref[...] = jnp.zeros(l_ref.shape, F32)
    acc_ref[...] = jnp.zeros(acc_ref.shape, F32)

    q_start = qi * tq

    def step(kb, masked):
        ks = pl.multiple_of(kb * tk, tk)
        k = jnp.concatenate([k1_ref[0, pl.ds(ks, tk), :], k2_ref[0, pl.ds(ks, tk), :]], axis=1)
        s = jnp.dot(k, q_t, preferred_element_type=F32)
        if masked:
            c = lax.broadcasted_iota(jnp.int32, s.shape, 1)
            t_pos = q_start + jnp.where(c >= tq, c - tq, c)
            s_pos = ks + lax.broadcasted_iota(jnp.int32, s.shape, 0)
            if fox:
                vis = s_pos <= t_pos
            else:
                vis = (s_pos // CHUNK) <= (t_pos // CHUNK)
            s = jnp.where(vis, s, NEG)
        m_prev = m_ref[...]
        m_new = jnp.maximum(m_prev, jnp.max(s, axis=0, keepdims=True))
        alpha = jnp.exp2(m_prev - m_new)
        pexp = jnp.exp2(s - m_new)
        l_ref[...] = alpha * l_ref[...] + jnp.sum(pexp, axis=0, keepdims=True)
        pv = lax.dot_general(v_ref[0, pl.ds(ks, tk), :], pexp.astype(BF16), (((0,), (0,)), ((), ())),
                             preferred_element_type=F32)
        acc_ref[...] = alpha * acc_ref[...] + pv
        m_ref[...] = m_new

    n_full = q_start // tk
    n_blocks = (q_start + tq + tk - 1) // tk

    def full_body(kb, carry):
        step(kb, False)
        return carry

    def masked_body(kb, carry):
        step(kb, True)
        return carry

    lax.fori_loop(0, n_full, full_body, 0)
    lax.fori_loop(n_full, n_blocks, masked_body, 0)

    out = acc_ref[...] / l_ref[...]
    out_t = jnp.concatenate([out[:HEAD_DIM, :tq], out[HEAD_DIM:, tq:]], axis=0)
    o_ref[0] = out_t.T.astype(o_ref.dtype)


def _attn_t(q1, q2, k1, k2, v, *, tq, tk, fox):
    b, t, hw = q1.shape
    pairs = hw // LANES
    qspec = pl.BlockSpec((1, tq, LANES), lambda bi, p, qi: (bi, qi, p))
    kspec = pl.BlockSpec((1, t, LANES), lambda bi, p, qi: (bi, 0, p))
    k2spec = pl.BlockSpec((1, t, LANES), lambda bi, p, qi: (bi, 0, 0))
    if fox:
        in_specs = [qspec, kspec, k2spec, kspec]
        args = (q1, k1, k2, v)
    else:
        q2spec = pl.BlockSpec((1, tq, LANES), lambda bi, p, qi: (bi, qi, p // 2))
        in_specs = [qspec, q2spec, kspec, k2spec, kspec]
        args = (q1, q2, k1, k2, v)
    return pl.pallas_call(
        functools.partial(_attn_t_kernel, tq=tq, tk=tk, fox=fox),
        out_shape=jax.ShapeDtypeStruct((b, t, hw), BF16),
        grid=(b, pairs, t // tq),
        in_specs=in_specs,
        out_specs=qspec,
        scratch_shapes=[pltpu.VMEM((1, 2 * tq), F32), pltpu.VMEM((1, 2 * tq), F32),
                        pltpu.VMEM((LANES, 2 * tq), F32)],
        compiler_params=pltpu.CompilerParams(dimension_semantics=("arbitrary",) * 3,
                                             vmem_limit_bytes=VMEM_LIMIT),
        name="attn_t_fox" if fox else "attn_t_mla",
    )(*args)


def _rope_tables(pos):
    half = ROPE_DIM // 2
    freqs = ROPE_THETA ** (-jnp.arange(half, dtype=F32) / half)
    ang = pos.astype(F32)[:, None] * freqs[None, :]
    cos, sin = jnp.cos(ang), jnp.sin(ang)
    cos_t = jnp.tile(jnp.concatenate([cos, cos], axis=1), (1, HEADS))
    sin_t = jnp.tile(jnp.concatenate([-sin, sin], axis=1), (1, HEADS))
    return cos_t, sin_t


def _swap_halves(w):
    half = w.shape[-1] // 2
    return jnp.concatenate([w[..., half:], w[..., :half]], axis=-1)


def _prep_weights(w_in, b_forget, w_q_up, w_kv_up, w_out):
    d = w_in.shape[0]
    hw = HEADS * HEAD_DIM
    o = 0
    cq = w_in[:, o:o + Q_LORA]; o += Q_LORA
    ckv = w_in[:, o:o + KV_LORA]; o += KV_LORA
    kr = w_in[:, o:o + ROPE_DIM]; o += ROPE_DIM
    fq = w_in[:, o:o + hw]; o += hw
    fk = w_in[:, o:o + hw]; o += hw
    fv = w_in[:, o:o + hw]; o += hw
    gate = w_in[:, o:o + HEADS]
    reps = LANES // ROPE_DIM
    pad = LANES - 4 * HEADS
    gate128 = jnp.concatenate([gate, jnp.repeat(gate, 3, axis=1), jnp.zeros((d, pad), w_in.dtype)], axis=1)
    w_in_p = jnp.concatenate([cq, ckv, fq, fk, fv, jnp.tile(kr, (1, reps)), jnp.tile(_swap_halves(kr), (1, reps)),
                              gate128], axis=1).astype(BF16)
    bias128 = jnp.concatenate([b_forget, jnp.repeat(b_forget, 3), jnp.zeros((pad,), F32)])[None, :]
    wq = w_q_up.reshape(Q_LORA, HEADS, NOPE_DIM + ROPE_DIM)
    wq_rope = wq[:, :, NOPE_DIM:]
    w_q_p = jnp.concatenate([wq[:, :, :NOPE_DIM].reshape(Q_LORA, -1), wq_rope.reshape(Q_LORA, -1),
                             _swap_halves(wq_rope).reshape(Q_LORA, -1)], axis=1).astype(BF16)
    wkv = w_kv_up.reshape(KV_LORA, HEADS, NOPE_DIM + HEAD_DIM)
    w_kv_p = jnp.concatenate([wkv[:, :, :NOPE_DIM].reshape(KV_LORA, -1), wkv[:, :, NOPE_DIM:].reshape(KV_LORA, -1)],
                             axis=1).astype(BF16)
    w_o1 = w_out[:hw].astype(BF16)
    w_o2 = w_out[hw:].astype(BF16)
    return w_in_p, bias128, w_q_p, w_kv_p, w_o1, w_o2


def _expand_logf(lf):
    pad = LANES - 4 * HEADS
    return jnp.concatenate([lf, jnp.repeat(lf, 3, axis=-1), jnp.zeros(lf.shape[:-1] + (pad,), lf.dtype)], axis=-1)


def kernel(x_prompt, x_sample, cache_mla_ckv, cache_mla_krope, cache_fox_k, cache_fox_v, cache_fox_logf,
           g_ffn1_pre, g_ffn1_post, w_ffn1_gu, w_ffn1_down, g_mix_pre, g_mix_post, w_in, b_forget,
           g_q_latent, w_q_up, g_kv_latent, w_kv_up, w_out, g_ffn2_pre, g_ffn2_post, w_ffn2_gu, w_ffn2_down):
    depth = w_in.shape[0]
    bp, tp, d = x_prompt.shape
    bs, ts, _ = x_sample.shape
    past = cache_mla_ckv.shape[2]
    hw = HEADS * HEAD_DIM
    tq_p, tk_p = 512, 512
    tk_s = 256
    tk_pad = -(-(past + ts) // tk_s) * tk_s

    cos_p, sin_p = _rope_tables(jnp.arange(tp))
    cos_s, sin_s = _rope_tables(past + jnp.arange(ts))
    cos_s, sin_s = jnp.tile(cos_s, (bs, 1)), jnp.tile(sin_s, (bs, 1))

    xp = x_prompt.reshape(bp * tp, d)
    xs = x_sample.reshape(bs * ts, d)
    tm_p = 512
    tm_s = bs * ts
    rows_p, rows_s = [], []

    def pad_keys(parts):
        n = sum(a.shape[1] for a in parts)
        parts = list(parts) + [jnp.zeros((bs, tk_pad - n, parts[0].shape[2]), parts[0].dtype)]
        return jnp.concatenate(parts, axis=1)

    for l in range(depth):
        d_ff = w_ffn1_down.shape[1]
        w1 = (w_ffn1_gu[l][:, :d_ff].astype(BF16), w_ffn1_gu[l][:, d_ff:].astype(BF16), w_ffn1_down[l].astype(BF16))
        w2 = (w_ffn2_gu[l][:, :d_ff].astype(BF16), w_ffn2_gu[l][:, d_ff:].astype(BF16), w_ffn2_down[l].astype(BF16))
        w_in_p, bias128, w_q_p, w_kv_p, w_o1, w_o2 = _prep_weights(w_in[l], b_forget[l], w_q_up[l], w_kv_up[l], w_out[l])
        g1 = (g_ffn1_pre[l][None, :], g_ffn1_post[l][None, :])
        g2 = (g_ffn2_pre[l][None, :], g_ffn2_post[l][None, :])
        gm_pre, gm_post = g_mix_pre[l][None, :], g_mix_post[l][None, :]
        gq, gkv = g_q_latent[l][None, :], g_kv_latent[l][None, :]

        hp = _ffn(xp, *g1, *w1, tm=tm_p)
        (ckv, krope, fk, fv, logf, qn, qr, kn, vm, kr4, fqb, fkb, fvb, lf128) = _proj(
            hp, cos_p, sin_p, gm_pre, w_in_p, bias128, gq, w_q_p, gkv, w_kv_p, tm=tm_p)
        sh = lambda a: a.reshape(bp, tp, a.shape[-1])
        ls = _lsplit(sh(lf128), tc=256)
        o_mla = _attn_t(sh(qn), sh(qr), sh(kn), sh(kr4), sh(vm), tq=tq_p, tk=tk_p, fox=False)
        o_fox = _attn_t(sh(fqb), None, sh(fkb), ls, sh(fvb), tq=tq_p, tk=tk_p, fox=True)
        xp = _ffn(hp, *g2, *w2, mix=(o_mla.reshape(-1, hw), o_fox.reshape(-1, hw), w_o1, w_o2, gm_post), tm=tm_p)
        rows_p.append((ckv.reshape(bp, tp, KV_LORA), krope.reshape(bp, tp, ROPE_DIM),
                       fk.reshape(bp, tp, HEADS, HEAD_DIM), fv.reshape(bp, tp, HEADS, HEAD_DIM),
                       logf.reshape(bp, tp, HEADS)))

        hs = _ffn(xs, *g1, *w1, tm=tm_s)
        (ckv, krope, fk, fv, logf, qn, qr, kn, vm, kr4, fqb, _, _, lf128) = _proj(
            hs, cos_s, sin_s, gm_pre, w_in_p, bias128, gq, w_q_p, gkv, w_kv_p, tm=tm_s)
        kn_past, vm_past, kr4_past = _pastkv(cache_mla_ckv[l].reshape(bs * past, KV_LORA),
                                             cache_mla_krope[l].reshape(bs * past, ROPE_DIM), w_kv_p, tm=512)
        sh = lambda a: a.reshape(bs, -1, a.shape[-1])
        kn_all = pad_keys([sh(kn_past), sh(kn)])
        vm_all = pad_keys([sh(vm_past), sh(vm)])
        kr4_all = pad_keys([sh(kr4_past), sh(kr4)])
        fk_all = pad_keys([cache_fox_k[l].reshape(bs, past, hw), sh(fk)])
        fv_all = pad_keys([cache_fox_v[l].reshape(bs, past, hw), sh(fv)])
        lf_all = pad_keys([_expand_logf(cache_fox_logf[l].astype(F32)), sh(lf128)])
        ls = _lsplit(lf_all, tc=256)
        kw = dict(tq=ts, tk=tk_s, q_off=past, kv_valid=past + ts)
        o_mla = _attn(sh(qn), sh(qr), kn_all, kr4_all, vm_all, fox=False, **kw)
        o_fox = _attn(sh(fqb), None, fk_all, ls, fv_all, fox=True, **kw)
        xs = _ffn(hs, *g2, *w2, mix=(o_mla.reshape(-1, hw), o_fox.reshape(-1, hw), w_o1, w_o2, gm_post), tm=tm_s)
        rows_s.append((ckv.reshape(bs, ts, KV_LORA), krope.reshape(bs, ts, ROPE_DIM),
                       fk.reshape(bs, ts, HEADS, HEAD_DIM), fv.reshape(bs, ts, HEADS, HEAD_DIM),
                       logf.reshape(bs, ts, HEADS)))

    outs_p = [jnp.stack([r[i] for r in rows_p]) for i in range(5)]
    outs_s = [jnp.stack([r[i] for r in rows_s]) for i in range(5)]
    return (xp.reshape(bp, tp, d), xs.reshape(bs, ts, d), *outs_p, *outs_s)
```

```python
import functools

import jax
import jax.numpy as jnp
from jax import lax
from jax.experimental import pallas as pl
from jax.experimental.pallas import tpu as pltpu

EPS = 1e-6
CHUNK = 64
ROPE_THETA = 10000.0
HEADS = 8
NOPE_DIM = 64
ROPE_DIM = 32
HEAD_DIM = 64
Q_LORA = 384
KV_LORA = 256
LANES = 128
COL_TILE = 256
ONES_ROWS = 16
ACC_ROWS = HEAD_DIM + 8
LOG2E = 1.4426950408889634
NEG = -1e30
LOGF_LANE0 = 8
VMEM_LIMIT = 56 * 1024 * 1024

BF16 = jnp.bfloat16
F32 = jnp.float32


def _rms(x, g):
    ms = jnp.mean(x * x, axis=-1, keepdims=True)
    return x * lax.rsqrt(ms + EPS) * g


def _const_spec(shape):
    return pl.BlockSpec(shape, lambda *_: (0,) * len(shape), pipeline_mode=pl.Buffered(1))


def _ffn_kernel(*refs, ff_chunk, with_mix):
    if with_mix:
        (h_ref, o1_ref, o2_ref, wo1_ref, wo2_ref, gmix_ref,
         gpre_ref, gpost_ref, wg_ref, wu_ref, wd_ref, out_ref) = refs
        mix = jnp.dot(o1_ref[...], wo1_ref[...], preferred_element_type=F32)
        mix = mix + jnp.dot(o2_ref[...], wo2_ref[...], preferred_element_type=F32)
        x = h_ref[...] + _rms(mix, gmix_ref[...])
    else:
        x_ref, gpre_ref, gpost_ref, wg_ref, wu_ref, wd_ref, out_ref = refs
        x = x_ref[...]
    n = _rms(x, gpre_ref[...]).astype(BF16)
    d_ff = wg_ref.shape[1]
    acc = jnp.zeros(x.shape, F32)
    for c in range(d_ff // ff_chunk):
        cols = slice(c * ff_chunk, (c + 1) * ff_chunk)
        gate = jnp.dot(n, wg_ref[:, cols], preferred_element_type=F32)
        up = jnp.dot(n, wu_ref[:, cols], preferred_element_type=F32)
        act = (gate * jax.nn.sigmoid(gate) * up).astype(BF16)
        acc = acc + jnp.dot(act, wd_ref[cols, :], preferred_element_type=F32)
    out_ref[...] = x + 0.5 * _rms(acc, gpost_ref[...])


def _ffn(x, g_pre, g_post, w_g, w_u, w_d, mix=None, *, tm):
    n, d = x.shape
    d_ff = w_g.shape[1]
    row = lambda w: pl.BlockSpec((tm, w), lambda i: (i, 0))
    in_specs = [row(d)]
    args = [x]
    if mix is not None:
        o1, o2, wo1, wo2, g_mix = mix
        in_specs += [row(o1.shape[1]), row(o2.shape[1]), _const_spec(wo1.shape), _const_spec(wo2.shape),
                     _const_spec((1, d))]
        args += [o1, o2, wo1, wo2, g_mix]
    in_specs += [_const_spec((1, d)), _const_spec((1, d)), _const_spec(w_g.shape), _const_spec(w_u.shape),
                 _const_spec(w_d.shape)]
    args += [g_pre, g_post, w_g, w_u, w_d]
    return pl.pallas_call(
        functools.partial(_ffn_kernel, ff_chunk=256, with_mix=mix is not None),
        out_shape=jax.ShapeDtypeStruct((n, d), F32),
        grid=(n // tm,),
        in_specs=in_specs,
        out_specs=row(d),
        compiler_params=pltpu.CompilerParams(dimension_semantics=("arbitrary",), vmem_limit_bytes=VMEM_LIMIT),
        name="ffn_mix" if mix is not None else "ffn",
    )(*args)


_C_CQ = 0
_C_CKV = _C_CQ + Q_LORA
_C_FQ = _C_CKV + KV_LORA
_C_FK = _C_FQ + HEADS * HEAD_DIM
_C_FV = _C_FK + HEADS * HEAD_DIM
_C_KR = _C_FV + HEADS * HEAD_DIM
_C_KRS = _C_KR + LANES
_C_GATE = _C_KRS + LANES
_C_END = _C_GATE + LANES


def _log_sigmoid(x):
    return jnp.minimum(x, 0.0) - jnp.log(1.0 + jnp.exp(-jnp.abs(x)))


def _proj_kernel(h_ref, cos_ref, sin_ref, gpre_ref, win_ref, bias_ref, gq_ref, wq_ref, gkv_ref, wkv_ref,
                 ckv_ref, krope_ref, fk_ref, fv_ref, logf_ref,
                 qn_ref, qr_ref, kn_ref, vm_ref, kr4_ref, fqb_ref, fkb_ref, fvb_ref, lf128_ref):
    hw = HEADS * HEAD_DIM
    u = _rms(h_ref[...], gpre_ref[...]).astype(BF16)
    proj = jnp.dot(u, win_ref[...], preferred_element_type=F32)
    cos = cos_ref[...]
    sin = sin_ref[...]

    cq = _rms(proj[:, _C_CQ:_C_CKV], gq_ref[...]).astype(BF16)
    q = jnp.dot(cq, wq_ref[...], preferred_element_type=F32)
    q_scale = (NOPE_DIM + ROPE_DIM) ** -0.5 * LOG2E
    rw = HEADS * ROPE_DIM
    qn_ref[...] = (q[:, :hw] * q_scale).astype(BF16)
    qr_ref[...] = ((q[:, hw:hw + rw] * cos + q[:, hw + rw:] * sin) * q_scale).astype(BF16)

    ckv = _rms(proj[:, _C_CKV:_C_FQ], gkv_ref[...])
    ckv_ref[...] = ckv
    kv = jnp.dot(ckv.astype(BF16), wkv_ref[...], preferred_element_type=F32)
    kn_ref[...] = kv[:, :hw].astype(BF16)
    vm_ref[...] = kv[:, hw:].astype(BF16)
    kr4 = proj[:, _C_KR:_C_KRS] * cos[:, :LANES] + proj[:, _C_KRS:_C_GATE] * sin[:, :LANES]
    krope_ref[...] = kr4[:, :ROPE_DIM]
    kr4_ref[...] = kr4.astype(BF16)

    fq = proj[:, _C_FQ:_C_FK]
    fk = proj[:, _C_FK:_C_FV]
    fv = proj[:, _C_FV:_C_KR]
    fqb_ref[...] = (fq * (HEAD_DIM ** -0.5 * LOG2E)).astype(BF16)
    fk_ref[...] = fk
    fv_ref[...] = fv
    fkb_ref[...] = fk.astype(BF16)
    fvb_ref[...] = fv.astype(BF16)
    logf = _log_sigmoid(proj[:, _C_GATE:_C_END] + bias_ref[...])
    logf_ref[...] = logf[:, :HEADS]
    lf128_ref[...] = logf


def _proj(h, cos, sin, g_pre, w_in, bias, g_q, w_q, g_kv, w_kv, *, tm):
    n, d = h.shape
    t_blocks = cos.shape[0] // tm
    hw = HEADS * HEAD_DIM
    row = lambda w: pl.BlockSpec((tm, w), lambda i: (i, 0))
    tab = pl.BlockSpec((tm, HEADS * ROPE_DIM), lambda i: (i % t_blocks, 0))
    widths = [(KV_LORA, F32), (ROPE_DIM, F32), (hw, F32), (hw, F32), (HEADS, F32),
              (hw, BF16), (HEADS * ROPE_DIM, BF16), (hw, BF16), (hw, BF16), (LANES, BF16),
              (hw, BF16), (hw, BF16), (hw, BF16), (LANES, F32)]
    return pl.pallas_call(
        _proj_kernel,
        out_shape=[jax.ShapeDtypeStruct((n, w), dt) for w, dt in widths],
        grid=(n // tm,),
        in_specs=[row(d), tab, tab, _const_spec((1, d)), _const_spec(w_in.shape), _const_spec((1, LANES)),
                  _const_spec((1, Q_LORA)), _const_spec(w_q.shape), _const_spec((1, KV_LORA)),
                  _const_spec(w_kv.shape)],
        out_specs=[row(w) for w, _ in widths],
        compiler_params=pltpu.CompilerParams(dimension_semantics=("arbitrary",), vmem_limit_bytes=VMEM_LIMIT),
        name="proj",
    )(h, cos, sin, g_pre, w_in, bias, g_q, w_q, g_kv, w_kv)


def _pastkv_kernel(ckv_ref, kr_ref, wkv_ref, kn_ref, vm_ref, kr4_ref):
    hw = HEADS * HEAD_DIM
    kv = jnp.dot(ckv_ref[...].astype(BF16), wkv_ref[...], preferred_element_type=F32)
    kn_ref[...] = kv[:, :hw].astype(BF16)
    vm_ref[...] = kv[:, hw:].astype(BF16)
    src = lax.broadcasted_iota(jnp.int32, (ROPE_DIM, LANES), 0)
    dst = lax.broadcasted_iota(jnp.int32, (ROPE_DIM, LANES), 1)
    rep = (dst % ROPE_DIM == src).astype(BF16)
    kr4_ref[...] = jnp.dot(kr_ref[...].astype(BF16), rep, preferred_element_type=F32).astype(BF16)


def _pastkv(ckv, krope, w_kv, *, tm):
    n = ckv.shape[0]
    hw = HEADS * HEAD_DIM
    row = lambda w: pl.BlockSpec((tm, w), lambda i: (i, 0))
    return pl.pallas_call(
        _pastkv_kernel,
        out_shape=[jax.ShapeDtypeStruct((n, hw), BF16), jax.ShapeDtypeStruct((n, hw), BF16),
                   jax.ShapeDtypeStruct((n, LANES), BF16)],
        grid=(n // tm,),
        in_specs=[row(KV_LORA), row(ROPE_DIM), _const_spec(w_kv.shape)],
        out_specs=[row(hw), row(hw), row(LANES)],
        compiler_params=pltpu.CompilerParams(dimension_semantics=("arbitrary",), vmem_limit_bytes=VMEM_LIMIT),
        name="pastkv",
    )(ckv, krope, w_kv)


def _split3(y, lane):
    hi = y.astype(BF16).astype(F32)
    r1 = y - hi
    mid = r1.astype(BF16).astype(F32)
    lo = r1 - mid
    j = (lane - LOGF_LANE0) % 3
    sel = jnp.where(j == 0, hi, jnp.where(j == 1, mid, lo))
    used = (lane >= LOGF_LANE0) & (lane < LOGF_LANE0 + 3 * HEADS)
    return jnp.where(used, sel, 0.0).astype(BF16)


def _lsplit_kernel(x_ref, o_ref, carry_ref, *, tc):
    @pl.when(pl.program_id(1) == 0)
    def _():
        carry_ref[...] = jnp.zeros_like(carry_ref)

    x = x_ref[0]
    hi = x.astype(BF16)
    r1 = x - hi.astype(F32)
    mid = r1.astype(BF16)
    lo = (r1 - mid.astype(F32)).astype(BF16)
    r = lax.broadcasted_iota(jnp.int32, (tc, tc), 0)
    c = lax.broadcasted_iota(jnp.int32, (tc, tc), 1)
    tri = (c <= r).astype(BF16)
    cum = (jnp.dot(tri, hi, preferred_element_type=F32) + jnp.dot(tri, mid, preferred_element_type=F32)
           + jnp.dot(tri, lo, preferred_element_type=F32)) + carry_ref[...]
    carry_ref[...] = cum[tc - 1:tc, :]
    lane = lax.broadcasted_iota(jnp.int32, (tc, LANES), 1)
    o_ref[0] = _split3(cum * (-LOG2E), lane)


def _lsplit(lf128, *, tc):
    b, t, _ = lf128.shape
    spec = pl.BlockSpec((1, tc, LANES), lambda i, j: (i, j, 0))
    return pl.pallas_call(
        functools.partial(_lsplit_kernel, tc=tc),
        out_shape=jax.ShapeDtypeStruct((b, t, LANES), BF16),
        grid=(b, t // tc),
        in_specs=[spec],
        out_specs=spec,
        scratch_shapes=[pltpu.VMEM((1, LANES), F32)],
        compiler_params=pltpu.CompilerParams(dimension_semantics=("arbitrary", "arbitrary")),
        name="lsplit",
    )(lf128)


def _attn_kernel(*refs, tq, tk, q_off, kv_valid, fox):
    if fox:
        q1_ref, k1_ref, k2_ref, v_ref, o_ref, m_ref, l_ref, acc_ref = refs
    else:
        q1_ref, q2_ref, k1_ref, k2_ref, v_ref, o_ref, m_ref, l_ref, acc_ref = refs
    p = pl.program_id(1)
    qi = pl.program_id(2)
    lane = lax.broadcasted_iota(jnp.int32, (tq, LANES), 1)
    q1 = q1_ref[0].astype(F32)
    q2 = jnp.ones_like(q1) if fox else q2_ref[0].astype(F32)

    def head_rows(a):
        main = jnp.where((lane >= a * HEAD_DIM) & (lane < (a + 1) * HEAD_DIM), q1, 0.0)
        if fox:
            lo, width = LOGF_LANE0 + 3 * (2 * p + a), 3
        else:
            lo, width = ROPE_DIM * (2 * (p % 2) + a), ROPE_DIM
        aux = jnp.where((lane >= lo) & (lane < lo + width), q2, 0.0)
        return jnp.concatenate([main, aux], axis=1).astype(BF16)

    q = jnp.concatenate([head_rows(0), head_rows(1)], axis=0)

    m_ref[...] = jnp.full(m_ref.shape, NEG, F32)
    l_ref[...] = jnp.zeros(l_ref.shape, F32)
    acc_ref[...] = jnp.zeros(acc_ref.shape, F32)

    q_start = q_off + qi * tq

    def step(kb, masked):
        ks = pl.multiple_of(kb * tk, tk)
        k = jnp.concatenate([k1_ref[0, pl.ds(ks, tk), :].astype(BF16),
                             k2_ref[0, pl.ds(ks, tk), :].astype(BF16)], axis=1)
        s = lax.dot_general(q, k, (((1,), (1,)), ((), ())), preferred_element_type=F32)
        if masked:
            r = lax.broadcasted_iota(jnp.int32, s.shape, 0)
            t_pos = q_start + jnp.where(r >= tq, r - tq, r)
            s_pos = ks + lax.broadcasted_iota(jnp.int32, s.shape, 1)
            if fox:
                vis = s_pos <= t_pos
            else:
                vis = (s_pos // CHUNK) <= (t_pos // CHUNK)
            s = jnp.where(vis & (s_pos < kv_valid), s, NEG)
        m_prev = m_ref[...]
        m_new = jnp.maximum(m_prev, jnp.max(s, axis=1, keepdims=True))
        alpha = jnp.exp2(m_prev - m_new)
        pexp = jnp.exp2(s - m_new)
        l_ref[...] = alpha * l_ref[...] + jnp.sum(pexp, axis=1, keepdims=True)
        pv = jnp.dot(pexp.astype(BF16), v_ref[0, pl.ds(ks, tk), :].astype(BF16), preferred_element_type=F32)
        acc_ref[...] = alpha * acc_ref[...] + pv
        m_ref[...] = m_new

    if fox:
        hi = q_start + tq
    else:
        hi = ((q_start + tq - 1) // CHUNK + 1) * CHUNK
    hi = jnp.minimum(hi, kv_valid)
    n_blocks = (hi + tk - 1) // tk
    n_full = q_start // tk

    def full_body(kb, carry):
        step(kb, False)
        return carry

    def masked_body(kb, carry):
        step(kb, True)
        return carry

    lax.fori_loop(0, n_full, full_body, 0)
    lax.fori_loop(n_full, n_blocks, masked_body, 0)

    out = acc_ref[...] / l_ref[...]
    o_ref[0] = jnp.where(lane < HEAD_DIM, out[:tq], out[tq:]).astype(o_ref.dtype)


def _attn(q1, q2, k1, k2, v, *, tq, tk, q_off, kv_valid, fox):
    b, t_q, hw = q1.shape
    t_k = k1.shape[1]
    pairs = hw // LANES
    qspec = pl.BlockSpec((1, tq, LANES), lambda bi, p, qi: (bi, qi, p))
    kspec = pl.BlockSpec((1, t_k, LANES), lambda bi, p, qi: (bi, 0, p))
    k2spec = pl.BlockSpec((1, t_k, LANES), lambda bi, p, qi: (bi, 0, 0))
    if fox:
        in_specs = [qspec, kspec, k2spec, kspec]
        args = (q1, k1, k2, v)
    else:
        q2spec = pl.BlockSpec((1, tq, LANES), lambda bi, p, qi: (bi, qi, p // 2))
        in_specs = [qspec, q2spec, kspec, k2spec, kspec]
        args = (q1, q2, k1, k2, v)
    return pl.pallas_call(
        functools.partial(_attn_kernel, tq=tq, tk=tk, q_off=q_off, kv_valid=kv_valid, fox=fox),
        out_shape=jax.ShapeDtypeStruct((b, t_q, hw), BF16),
        grid=(b, pairs, t_q // tq),
        in_specs=in_specs,
        out_specs=qspec,
        scratch_shapes=[pltpu.VMEM((2 * tq, 1), F32), pltpu.VMEM((2 * tq, 1), F32),
                        pltpu.VMEM((2 * tq, LANES), F32)],
        compiler_params=pltpu.CompilerParams(dimension_semantics=("arbitrary",) * 3,
                                             vmem_limit_bytes=VMEM_LIMIT),
        name="attn_fox" if fox else "attn_mla",
    )(*args)


def _attn_t_kernel(*refs, tq, tk, fox):
    n_in = 4 if fox else 5
    q1_ref = refs[0]
    q2_ref = None if fox else refs[1]
    k1_ref, k2_ref, v_ref, o_ref = refs[n_in - 3:n_in + 1]
    (qt_ref, m_ref, acc_ref, s0_ref, s1_ref, p0_ref, p1_ref, a0_ref, a1_ref,
     bm0_ref, bm1_ref) = refs[n_in + 1:]
    s_refs, p_refs, a_refs, bm_refs = (s0_ref, s1_ref), (p0_ref, p1_ref), (a0_ref, a1_ref), (bm0_ref, bm1_ref)
    p = pl.program_id(1)
    qi = pl.program_id(2)
    lane = lax.broadcasted_iota(jnp.int32, (tq, LANES), 1)
    q1 = q1_ref[0].astype(F32)
    q2 = jnp.ones_like(q1) if fox else q2_ref[0].astype(F32)

    def head_rows(a):
        main = jnp.where((lane >= a * HEAD_DIM) & (lane < (a + 1) * HEAD_DIM), q1, 0.0)
        if fox:
            lo, width = LOGF_LANE0 + 3 * (2 * p + a), 3
        else:
            lo, width = ROPE_DIM * (2 * (p % 2) + a), ROPE_DIM
        aux = jnp.where((lane >= lo) & (lane < lo + width), q2, 0.0)
        return jnp.concatenate([main, aux], axis=1)

    qt_ref[...] = jnp.concatenate([head_rows(0), head_rows(1)], axis=0).T.astype(BF16)
    m_ref[...] = jnp.full(m_ref.shape, NEG, F32)
    acc_ref[...] = jnp.zeros(acc_ref.shape, F32)
    p1_ref[...] = jnp.zeros(p1_ref.shape, BF16)
    a1_ref[...] = jnp.zeros(a1_ref.shape, F32)

    q_start = qi * tq

    col_tiles = [slice(j, j + COL_TILE) for j in range(0, 2 * tq, COL_TILE)]

    def key_block(t):
        ks = pl.multiple_of(t * tk, tk)
        return jnp.concatenate([k1_ref[0, pl.ds(ks, tk), :], k2_ref[0, pl.ds(ks, tk), :]], axis=1)

    def value_block_t(kb):
        ks = pl.multiple_of(kb * tk, tk)
        return v_ref[0, pl.ds(ks, tk), :].astype(F32).T.astype(BF16)

    def scores(t, k, par, cols, masked):
        s = jnp.dot(k, qt_ref[:, cols], preferred_element_type=F32)
        if masked:
            c = cols.start + lax.broadcasted_iota(jnp.int32, s.shape, 1)
            t_pos = q_start + jnp.where(c >= tq, c - tq, c)
            s_pos = t * tk + lax.broadcasted_iota(jnp.int32, s.shape, 0)
            vis = (s_pos <= t_pos) if fox else ((s_pos // CHUNK) <= (t_pos // CHUNK))
            s = jnp.where(vis, s, NEG)
        s_refs[par][:, cols] = s
        bm_refs[par][:, cols] = jnp.max(s, axis=0, keepdims=True)

    def softmax(par, cols):
        m_prev = m_ref[:, cols]
        m_new = jnp.maximum(m_prev, bm_refs[par][:, cols])
        a_refs[par][:, cols] = jnp.exp2(m_prev - m_new)
        m_ref[:, cols] = m_new
        p_refs[par][:, cols] = jnp.exp2((s_refs[par][:, cols] - m_new).astype(BF16))

    ones_rows = jnp.ones((ONES_ROWS, tk), BF16)

    def values(v_t, par, cols):
        head = cols.start // tq
        lhs = jnp.concatenate([v_t[head * HEAD_DIM:(head + 1) * HEAD_DIM], ones_rows], axis=0)
        pv = jnp.dot(lhs, p_refs[par][:, cols], preferred_element_type=F32)
        acc_ref[:, cols] = a_refs[par][:, cols] * acc_ref[:, cols] + pv[:ACC_ROWS]

    def tick(t, masked=False, with_scores=True, with_softmax=True):
        par = t % 2 if isinstance(t, int) else t.par
        tv = t if isinstance(t, int) else t.value
        k = key_block(tv) if with_scores else None
        v_t = value_block_t(jnp.maximum(tv - 2, 0))
        for cols in col_tiles:
            if with_softmax:
                softmax(1 - par, cols)
            if with_scores:
                scores(tv, k, par, cols, masked)
            values(v_t, par, cols)

    class Tick:
        def __init__(self, value, par):
            self.value, self.par = value, par

    last = qi

    @pl.when(qi == 0)
    def _():
        k0 = key_block(0)
        for cols in col_tiles:
            scores(0, k0, 0, cols, True)
        tick(1, with_scores=False)
        tick(2, with_scores=False, with_softmax=False)

    @pl.when(qi > 0)
    def _():
        k0 = key_block(0)
        for cols in col_tiles:
            scores(0, k0, 0, cols, False)
        unmasked = last - 1

        def tick_pair(j, carry):
            tick(Tick(2 * j + 1, 1))
            tick(Tick(2 * j + 2, 0))
            return carry

        lax.fori_loop(0, unmasked // 2, tick_pair, 0)

        @pl.when(unmasked % 2 == 1)
        def _():
            tick(Tick(last - 1, 1))
            tick(Tick(last, 0), masked=True)
            tick(Tick(last + 1, 1), with_scores=False)
            tick(Tick(last + 2, 0), with_scores=False, with_softmax=False)

        @pl.when(unmasked % 2 == 0)
        def _():
            tick(Tick(last, 1), masked=True)
            tick(Tick(last + 1, 0), with_scores=False)
            tick(Tick(last + 2, 1), with_scores=False, with_softmax=False)

    acc = acc_ref[...]
    out_t = jnp.concatenate([acc[:HEAD_DIM, :tq] / acc[HEAD_DIM:HEAD_DIM + 1, :tq],
                             acc[:HEAD_DIM, tq:] / acc[HEAD_DIM:HEAD_DIM + 1, tq:]], axis=0)
    o_ref[0] = out_t.T.astype(o_ref.dtype)


def _attn_t(q1, q2, k1, k2, v, *, tq, fox):
    b, t, hw = q1.shape
    tk = tq
    m2 = 2 * tq
    pairs = hw // LANES
    qspec = pl.BlockSpec((1, tq, LANES), lambda bi, p, qi: (bi, qi, p))
    kspec = pl.BlockSpec((1, t, LANES), lambda bi, p, qi: (bi, 0, p))
    k2spec = pl.BlockSpec((1, t, LANES), lambda bi, p, qi: (bi, 0, 0))
    if fox:
        in_specs = [qspec, kspec, k2spec, kspec]
        args = (q1, k1, k2, v)
    else:
        q2spec = pl.BlockSpec((1, tq, LANES), lambda bi, p, qi: (bi, qi, p // 2))
        in_specs = [qspec, q2spec, kspec, k2spec, kspec]
        args = (q1, q2, k1, k2, v)
    return pl.pallas_call(
        functools.partial(_attn_t_kernel, tq=tq, tk=tk, fox=fox),
        out_shape=jax.ShapeDtypeStruct((b, t, hw), BF16),
        grid=(b, pairs, t // tq),
        in_specs=in_specs,
        out_specs=qspec,
        scratch_shapes=[pltpu.VMEM((2 * LANES, m2), BF16),
                        pltpu.VMEM((1, m2), F32),
                        pltpu.VMEM((ACC_ROWS, m2), F32),
                        pltpu.VMEM((tk, m2), F32), pltpu.VMEM((tk, m2), F32),
                        pltpu.VMEM((tk, m2), BF16), pltpu.VMEM((tk, m2), BF16),
                        pltpu.VMEM((1, m2), F32), pltpu.VMEM((1, m2), F32),
                        pltpu.VMEM((1, m2), F32), pltpu.VMEM((1, m2), F32)],
        compiler_params=pltpu.CompilerParams(dimension_semantics=("arbitrary",) * 3,
                                             vmem_limit_bytes=VMEM_LIMIT),
        name="attn_t_fox" if fox else "attn_t_mla",
    )(*args)


def _rope_tables(pos):
    half = ROPE_DIM // 2
    freqs = ROPE_THETA ** (-jnp.arange(half, dtype=F32) / half)
    ang = pos.astype(F32)[:, None] * freqs[None, :]
    cos, sin = jnp.cos(ang), jnp.sin(ang)
    cos_t = jnp.tile(jnp.concatenate([cos, cos], axis=1), (1, HEADS))
    sin_t = jnp.tile(jnp.concatenate([-sin, sin], axis=1), (1, HEADS))
    return cos_t, sin_t


def _swap_halves(w):
    half = w.shape[-1] // 2
    return jnp.concatenate([w[..., half:], w[..., :half]], axis=-1)


def _prep_weights(w_in, b_forget, w_q_up, w_kv_up, w_out):
    d = w_in.shape[0]
    hw = HEADS * HEAD_DIM
    o = 0
    cq = w_in[:, o:o + Q_LORA]; o += Q_LORA
    ckv = w_in[:, o:o + KV_LORA]; o += KV_LORA
    kr = w_in[:, o:o + ROPE_DIM]; o += ROPE_DIM
    fq = w_in[:, o:o + hw]; o += hw
    fk = w_in[:, o:o + hw]; o += hw
    fv = w_in[:, o:o + hw]; o += hw
    gate = w_in[:, o:o + HEADS]
    reps = LANES // ROPE_DIM
    pad = LANES - 4 * HEADS
    gate128 = jnp.concatenate([gate, jnp.repeat(gate, 3, axis=1), jnp.zeros((d, pad), w_in.dtype)], axis=1)
    w_in_p = jnp.concatenate([cq, ckv, fq, fk, fv, jnp.tile(kr, (1, reps)), jnp.tile(_swap_halves(kr), (1, reps)),
                              gate128], axis=1).astype(BF16)
    bias128 = jnp.concatenate([b_forget, jnp.repeat(b_forget, 3), jnp.zeros((pad,), F32)])[None, :]
    wq = w_q_up.reshape(Q_LORA, HEADS, NOPE_DIM + ROPE_DIM)
    wq_rope = wq[:, :, NOPE_DIM:]
    w_q_p = jnp.concatenate([wq[:, :, :NOPE_DIM].reshape(Q_LORA, -1), wq_rope.reshape(Q_LORA, -1),
                             _swap_halves(wq_rope).reshape(Q_LORA, -1)], axis=1).astype(BF16)
    wkv = w_kv_up.reshape(KV_LORA, HEADS, NOPE_DIM + HEAD_DIM)
    w_kv_p = jnp.concatenate([wkv[:, :, :NOPE_DIM].reshape(KV_LORA, -1), wkv[:, :, NOPE_DIM:].reshape(KV_LORA, -1)],
                             axis=1).astype(BF16)
    w_o1 = w_out[:hw].astype(BF16)
    w_o2 = w_out[hw:].astype(BF16)
    return w_in_p, bias128, w_q_p, w_kv_p, w_o1, w_o2


def _expand_logf(lf):
    pad = LANES - 4 * HEADS
    return jnp.concatenate([lf, jnp.repeat(lf, 3, axis=-1), jnp.zeros(lf.shape[:-1] + (pad,), lf.dtype)], axis=-1)


def kernel(x_prompt, x_sample, cache_mla_ckv, cache_mla_krope, cache_fox_k, cache_fox_v, cache_fox_logf,
           g_ffn1_pre, g_ffn1_post, w_ffn1_gu, w_ffn1_down, g_mix_pre, g_mix_post, w_in, b_forget,
           g_q_latent, w_q_up, g_kv_latent, w_kv_up, w_out, g_ffn2_pre, g_ffn2_post, w_ffn2_gu, w_ffn2_down):
    depth = w_in.shape[0]
    bp, tp, d = x_prompt.shape
    bs, ts, _ = x_sample.shape
    past = cache_mla_ckv.shape[2]
    hw = HEADS * HEAD_DIM
    tq_p = 512
    tk_s = 256
    tk_pad = -(-(past + ts) // tk_s) * tk_s

    cos_p, sin_p = _rope_tables(jnp.arange(tp))
    cos_s, sin_s = _rope_tables(past + jnp.arange(ts))
    cos_s, sin_s = jnp.tile(cos_s, (bs, 1)), jnp.tile(sin_s, (bs, 1))

    xp = x_prompt.reshape(bp * tp, d)
    xs = x_sample.reshape(bs * ts, d)
    tm_p = 512
    tm_s = bs * ts
    rows_p, rows_s = [], []

    def pad_keys(parts):
        n = sum(a.shape[1] for a in parts)
        parts = list(parts) + [jnp.zeros((bs, tk_pad - n, parts[0].shape[2]), parts[0].dtype)]
        return jnp.concatenate(parts, axis=1)

    for l in range(depth):
        d_ff = w_ffn1_down.shape[1]
        w1 = (w_ffn1_gu[l][:, :d_ff].astype(BF16), w_ffn1_gu[l][:, d_ff:].astype(BF16), w_ffn1_down[l].astype(BF16))
        w2 = (w_ffn2_gu[l][:, :d_ff].astype(BF16), w_ffn2_gu[l][:, d_ff:].astype(BF16), w_ffn2_down[l].astype(BF16))
        w_in_p, bias128, w_q_p, w_kv_p, w_o1, w_o2 = _prep_weights(w_in[l], b_forget[l], w_q_up[l], w_kv_up[l], w_out[l])
        g1 = (g_ffn1_pre[l][None, :], g_ffn1_post[l][None, :])
        g2 = (g_ffn2_pre[l][None, :], g_ffn2_post[l][None, :])
        gm_pre, gm_post = g_mix_pre[l][None, :], g_mix_post[l][None, :]
        gq, gkv = g_q_latent[l][None, :], g_kv_latent[l][None, :]

        hp = _ffn(xp, *g1, *w1, tm=tm_p)
        (ckv, krope, fk, fv, logf, qn, qr, kn, vm, kr4, fqb, fkb, fvb, lf128) = _proj(
            hp, cos_p, sin_p, gm_pre, w_in_p, bias128, gq, w_q_p, gkv, w_kv_p, tm=tm_p)
        sh = lambda a: a.reshape(bp, tp, a.shape[-1])
        ls = _lsplit(sh(lf128), tc=256)
        o_mla = _attn_t(sh(qn), sh(qr), sh(kn), sh(kr4), sh(vm), tq=tq_p, fox=False)
        o_fox = _attn_t(sh(fqb), None, sh(fkb), ls, sh(fvb), tq=tq_p, fox=True)
        xp = _ffn(hp, *g2, *w2, mix=(o_mla.reshape(-1, hw), o_fox.reshape(-1, hw), w_o1, w_o2, gm_post), tm=tm_p)
        rows_p.append((ckv.reshape(bp, tp, KV_LORA), krope.reshape(bp, tp, ROPE_DIM),
                       fk.reshape(bp, tp, HEADS, HEAD_DIM), fv.reshape(bp, tp, HEADS, HEAD_DIM),
                       logf.reshape(bp, tp, HEADS)))

        hs = _ffn(xs, *g1, *w1, tm=tm_s)
        (ckv, krope, fk, fv, logf, qn, qr, kn, vm, kr4, fqb, _, _, lf128) = _proj(
            hs, cos_s, sin_s, gm_pre, w_in_p, bias128, gq, w_q_p, gkv, w_kv_p, tm=tm_s)
        kn_past, vm_past, kr4_past = _pastkv(cache_mla_ckv[l].reshape(bs * past, KV_LORA),
                                             cache_mla_krope[l].reshape(bs * past, ROPE_DIM), w_kv_p, tm=512)
        sh = lambda a: a.reshape(bs, -1, a.shape[-1])
        kn_all = pad_keys([sh(kn_past), sh(kn)])
        vm_all = pad_keys([sh(vm_past), sh(vm)])
        kr4_all = pad_keys([sh(kr4_past), sh(kr4)])
        fk_all = pad_keys([cache_fox_k[l].reshape(bs, past, hw), sh(fk)])
        fv_all = pad_keys([cache_fox_v[l].reshape(bs, past, hw), sh(fv)])
        lf_all = pad_keys([_expand_logf(cache_fox_logf[l].astype(F32)), sh(lf128)])
        ls = _lsplit(lf_all, tc=256)
        kw = dict(tq=ts, tk=tk_s, q_off=past, kv_valid=past + ts)
        o_mla = _attn(sh(qn), sh(qr), kn_all, kr4_all, vm_all, fox=False, **kw)
        o_fox = _attn(sh(fqb), None, fk_all, ls, fv_all, fox=True, **kw)
        xs = _ffn(hs, *g2, *w2, mix=(o_mla.reshape(-1, hw), o_fox.reshape(-1, hw), w_o1, w_o2, gm_post), tm=tm_s)
        rows_s.append((ckv.reshape(bs, ts, KV_LORA), krope.reshape(bs, ts, ROPE_DIM),
                       fk.reshape(bs, ts, HEADS, HEAD_DIM), fv.reshape(bs, ts, HEADS, HEAD_DIM),
                       logf.reshape(bs, ts, HEADS)))

    outs_p = [jnp.stack([r[i] for r in rows_p]) for i in range(5)]
    outs_s = [jnp.stack([r[i] for r in rows_s]) for i in range(5)]
    return (xp.reshape(bp, tp, d), xs.reshape(bs, ts, d), *outs_p, *outs_s)
```

```python
import functools

import jax
import jax.numpy as jnp
from jax import lax
from jax.experimental import pallas as pl
from jax.experimental.pallas import tpu as pltpu

EPS = 1e-6
CHUNK = 64
ROPE_THETA = 10000.0
HEADS = 8
NOPE_DIM = 64
ROPE_DIM = 32
HEAD_DIM = 64
Q_LORA = 384
KV_LORA = 256
LANES = 128
COL_TILE = 256
ONES_ROWS = 16
ACC_ROWS = HEAD_DIM + 8
LOG2E = 1.4426950408889634
NEG = -1e30
LOGF_LANE0 = 8
VMEM_LIMIT = 56 * 1024 * 1024

BF16 = jnp.bfloat16
F32 = jnp.float32


def _rms(x, g):
    ms = jnp.mean(x * x, axis=-1, keepdims=True)
    return x * lax.rsqrt(ms + EPS) * g


def _const_spec(shape):
    return pl.BlockSpec(shape, lambda *_: (0,) * len(shape), pipeline_mode=pl.Buffered(1))


def _ffn_kernel(*refs, ff_chunk, with_mix):
    if with_mix:
        (h_ref, o1_ref, o2_ref, wo1_ref, wo2_ref, gmix_ref,
         gpre_ref, gpost_ref, wg_ref, wu_ref, wd_ref, out_ref) = refs
        mix = jnp.dot(o1_ref[...], wo1_ref[...], preferred_element_type=F32)
        mix = mix + jnp.dot(o2_ref[...], wo2_ref[...], preferred_element_type=F32)
        x = h_ref[...] + _rms(mix, gmix_ref[...])
    else:
        x_ref, gpre_ref, gpost_ref, wg_ref, wu_ref, wd_ref, out_ref = refs
        x = x_ref[...]
    n = _rms(x, gpre_ref[...]).astype(BF16)
    d_ff = wg_ref.shape[1]
    acc = jnp.zeros(x.shape, F32)
    for c in range(d_ff // ff_chunk):
        cols = slice(c * ff_chunk, (c + 1) * ff_chunk)
        gate = jnp.dot(n, wg_ref[:, cols], preferred_element_type=F32)
        up = jnp.dot(n, wu_ref[:, cols], preferred_element_type=F32)
        act = (gate * jax.nn.sigmoid(gate) * up).astype(BF16)
        acc = acc + jnp.dot(act, wd_ref[cols, :], preferred_element_type=F32)
    out_ref[...] = x + 0.5 * _rms(acc, gpost_ref[...])


def _ffn(x, g_pre, g_post, w_g, w_u, w_d, mix=None, *, tm):
    n, d = x.shape
    d_ff = w_g.shape[1]
    row = lambda w: pl.BlockSpec((tm, w), lambda i: (i, 0))
    in_specs = [row(d)]
    args = [x]
    if mix is not None:
        o1, o2, wo1, wo2, g_mix = mix
        in_specs += [row(o1.shape[1]), row(o2.shape[1]), _const_spec(wo1.shape), _const_spec(wo2.shape),
                     _const_spec((1, d))]
        args += [o1, o2, wo1, wo2, g_mix]
    in_specs += [_const_spec((1, d)), _const_spec((1, d)), _const_spec(w_g.shape), _const_spec(w_u.shape),
                 _const_spec(w_d.shape)]
    args += [g_pre, g_post, w_g, w_u, w_d]
    return pl.pallas_call(
        functools.partial(_ffn_kernel, ff_chunk=256, with_mix=mix is not None),
        out_shape=jax.ShapeDtypeStruct((n, d), F32),
        grid=(n // tm,),
        in_specs=in_specs,
        out_specs=row(d),
        compiler_params=pltpu.CompilerParams(dimension_semantics=("arbitrary",), vmem_limit_bytes=VMEM_LIMIT),
        name="ffn_mix" if mix is not None else "ffn",
    )(*args)


_C_CQ = 0
_C_CKV = _C_CQ + Q_LORA
_C_FQ = _C_CKV + KV_LORA
_C_FK = _C_FQ + HEADS * HEAD_DIM
_C_FV = _C_FK + HEADS * HEAD_DIM
_C_KR = _C_FV + HEADS * HEAD_DIM
_C_KRS = _C_KR + LANES
_C_GATE = _C_KRS + LANES
_C_END = _C_GATE + LANES


def _log_sigmoid(x):
    return jnp.minimum(x, 0.0) - jnp.log(1.0 + jnp.exp(-jnp.abs(x)))


def _proj_kernel(h_ref, cos_ref, sin_ref, gpre_ref, win_ref, bias_ref, gq_ref, wq_ref, gkv_ref, wkv_ref,
                 ckv_ref, krope_ref, fk_ref, fv_ref, logf_ref,
                 qn_ref, qr_ref, kn_ref, vm_ref, kr4_ref, fqb_ref, fkb_ref, fvb_ref, lf128_ref):
    hw = HEADS * HEAD_DIM
    u = _rms(h_ref[...], gpre_ref[...]).astype(BF16)
    proj = jnp.dot(u, win_ref[...], preferred_element_type=F32)
    cos = cos_ref[...]
    sin = sin_ref[...]

    cq = _rms(proj[:, _C_CQ:_C_CKV], gq_ref[...]).astype(BF16)
    q = jnp.dot(cq, wq_ref[...], preferred_element_type=F32)
    q_scale = (NOPE_DIM + ROPE_DIM) ** -0.5 * LOG2E
    rw = HEADS * ROPE_DIM
    qn_ref[...] = (q[:, :hw] * q_scale).astype(BF16)
    qr_ref[...] = ((q[:, hw:hw + rw] * cos + q[:, hw + rw:] * sin) * q_scale).astype(BF16)

    ckv = _rms(proj[:, _C_CKV:_C_FQ], gkv_ref[...])
    ckv_ref[...] = ckv
    kv = jnp.dot(ckv.astype(BF16), wkv_ref[...], preferred_element_type=F32)
    kn_ref[...] = kv[:, :hw].astype(BF16)
    vm_ref[...] = kv[:, hw:].astype(BF16)
    kr4 = proj[:, _C_KR:_C_KRS] * cos[:, :LANES] + proj[:, _C_KRS:_C_GATE] * sin[:, :LANES]
    krope_ref[...] = kr4[:, :ROPE_DIM]
    kr4_ref[...] = kr4.astype(BF16)

    fq = proj[:, _C_FQ:_C_FK]
    fk = proj[:, _C_FK:_C_FV]
    fv = proj[:, _C_FV:_C_KR]
    fqb_ref[...] = (fq * (HEAD_DIM ** -0.5 * LOG2E)).astype(BF16)
    fk_ref[...] = fk
    fv_ref[...] = fv
    fkb_ref[...] = fk.astype(BF16)
    fvb_ref[...] = fv.astype(BF16)
    logf = _log_sigmoid(proj[:, _C_GATE:_C_END] + bias_ref[...])
    logf_ref[...] = logf[:, :HEADS]
    lf128_ref[...] = logf


def _proj(h, cos, sin, g_pre, w_in, bias, g_q, w_q, g_kv, w_kv, *, tm):
    n, d = h.shape
    t_blocks = cos.shape[0] // tm
    hw = HEADS * HEAD_DIM
    row = lambda w: pl.BlockSpec((tm, w), lambda i: (i, 0))
    tab = pl.BlockSpec((tm, HEADS * ROPE_DIM), lambda i: (i % t_blocks, 0))
    widths = [(KV_LORA, F32), (ROPE_DIM, F32), (hw, F32), (hw, F32), (HEADS, F32),
              (hw, BF16), (HEADS * ROPE_DIM, BF16), (hw, BF16), (hw, BF16), (LANES, BF16),
              (hw, BF16), (hw, BF16), (hw, BF16), (LANES, F32)]
    return pl.pallas_call(
        _proj_kernel,
        out_shape=[jax.ShapeDtypeStruct((n, w), dt) for w, dt in widths],
        grid=(n // tm,),
        in_specs=[row(d), tab, tab, _const_spec((1, d)), _const_spec(w_in.shape), _const_spec((1, LANES)),
                  _const_spec((1, Q_LORA)), _const_spec(w_q.shape), _const_spec((1, KV_LORA)),
                  _const_spec(w_kv.shape)],
        out_specs=[row(w) for w, _ in widths],
        compiler_params=pltpu.CompilerParams(dimension_semantics=("arbitrary",), vmem_limit_bytes=VMEM_LIMIT),
        name="proj",
    )(h, cos, sin, g_pre, w_in, bias, g_q, w_q, g_kv, w_kv)


def _pastkv_kernel(ckv_ref, kr_ref, wkv_ref, kn_ref, vm_ref, kr4_ref):
    hw = HEADS * HEAD_DIM
    kv = jnp.dot(ckv_ref[...].astype(BF16), wkv_ref[...], preferred_element_type=F32)
    kn_ref[...] = kv[:, :hw].astype(BF16)
    vm_ref[...] = kv[:, hw:].astype(BF16)
    src = lax.broadcasted_iota(jnp.int32, (ROPE_DIM, LANES), 0)
    dst = lax.broadcasted_iota(jnp.int32, (ROPE_DIM, LANES), 1)
    rep = (dst % ROPE_DIM == src).astype(BF16)
    kr4_ref[...] = jnp.dot(kr_ref[...].astype(BF16), rep, preferred_element_type=F32).astype(BF16)


def _pastkv(ckv, krope, w_kv, *, tm):
    n = ckv.shape[0]
    hw = HEADS * HEAD_DIM
    row = lambda w: pl.BlockSpec((tm, w), lambda i: (i, 0))
    return pl.pallas_call(
        _pastkv_kernel,
        out_shape=[jax.ShapeDtypeStruct((n, hw), BF16), jax.ShapeDtypeStruct((n, hw), BF16),
                   jax.ShapeDtypeStruct((n, LANES), BF16)],
        grid=(n // tm,),
        in_specs=[row(KV_LORA), row(ROPE_DIM), _const_spec(w_kv.shape)],
        out_specs=[row(hw), row(hw), row(LANES)],
        compiler_params=pltpu.CompilerParams(dimension_semantics=("arbitrary",), vmem_limit_bytes=VMEM_LIMIT),
        name="pastkv",
    )(ckv, krope, w_kv)


def _split3(y, lane):
    hi = y.astype(BF16).astype(F32)
    r1 = y - hi
    mid = r1.astype(BF16).astype(F32)
    lo = r1 - mid
    j = (lane - LOGF_LANE0) % 3
    sel = jnp.where(j == 0, hi, jnp.where(j == 1, mid, lo))
    used = (lane >= LOGF_LANE0) & (lane < LOGF_LANE0 + 3 * HEADS)
    return jnp.where(used, sel, 0.0).astype(BF16)


def _lsplit_kernel(x_ref, o_ref, carry_ref, *, tc):
    @pl.when(pl.program_id(1) == 0)
    def _():
        carry_ref[...] = jnp.zeros_like(carry_ref)

    x = x_ref[0]
    hi = x.astype(BF16)
    r1 = x - hi.astype(F32)
    mid = r1.astype(BF16)
    lo = (r1 - mid.astype(F32)).astype(BF16)
    r = lax.broadcasted_iota(jnp.int32, (tc, tc), 0)
    c = lax.broadcasted_iota(jnp.int32, (tc, tc), 1)
    tri = (c <= r).astype(BF16)
    cum = (jnp.dot(tri, hi, preferred_element_type=F32) + jnp.dot(tri, mid, preferred_element_type=F32)
           + jnp.dot(tri, lo, preferred_element_type=F32)) + carry_ref[...]
    carry_ref[...] = cum[tc - 1:tc, :]
    lane = lax.broadcasted_iota(jnp.int32, (tc, LANES), 1)
    o_ref[0] = _split3(cum * (-LOG2E), lane)


def _lsplit(lf128, *, tc):
    b, t, _ = lf128.shape
    spec = pl.BlockSpec((1, tc, LANES), lambda i, j: (i, j, 0))
    return pl.pallas_call(
        functools.partial(_lsplit_kernel, tc=tc),
        out_shape=jax.ShapeDtypeStruct((b, t, LANES), BF16),
        grid=(b, t // tc),
        in_specs=[spec],
        out_specs=spec,
        scratch_shapes=[pltpu.VMEM((1, LANES), F32)],
        compiler_params=pltpu.CompilerParams(dimension_semantics=("arbitrary", "arbitrary")),
        name="lsplit",
    )(lf128)


def _attn_kernel(*refs, tq, tk, q_off, kv_valid, fox):
    if fox:
        q1_ref, k1_ref, k2_ref, v_ref, o_ref, m_ref, l_ref, acc_ref = refs
    else:
        q1_ref, q2_ref, k1_ref, k2_ref, v_ref, o_ref, m_ref, l_ref, acc_ref = refs
    p = pl.program_id(1)
    qi = pl.program_id(2)
    lane = lax.broadcasted_iota(jnp.int32, (tq, LANES), 1)
    q1 = q1_ref[0].astype(F32)
    q2 = jnp.ones_like(q1) if fox else q2_ref[0].astype(F32)

    def head_rows(a):
        main = jnp.where((lane >= a * HEAD_DIM) & (lane < (a + 1) * HEAD_DIM), q1, 0.0)
        if fox:
            lo, width = LOGF_LANE0 + 3 * (2 * p + a), 3
        else:
            lo, width = ROPE_DIM * (2 * (p % 2) + a), ROPE_DIM
        aux = jnp.where((lane >= lo) & (lane < lo + width), q2, 0.0)
        return jnp.concatenate([main, aux], axis=1).astype(BF16)

    q = jnp.concatenate([head_rows(0), head_rows(1)], axis=0)

    m_ref[...] = jnp.full(m_ref.shape, NEG, F32)
    l_ref[...] = jnp.zeros(l_ref.shape, F32)
    acc_ref[...] = jnp.zeros(acc_ref.shape, F32)

    q_start = q_off + qi * tq

    def step(kb, masked):
        ks = pl.multiple_of(kb * tk, tk)
        k = jnp.concatenate([k1_ref[0, pl.ds(ks, tk), :].astype(BF16),
                             k2_ref[0, pl.ds(ks, tk), :].astype(BF16)], axis=1)
        s = lax.dot_general(q, k, (((1,), (1,)), ((), ())), preferred_element_type=F32)
        if masked:
            r = lax.broadcasted_iota(jnp.int32, s.shape, 0)
            t_pos = q_start + jnp.where(r >= tq, r - tq, r)
            s_pos = ks + lax.broadcasted_iota(jnp.int32, s.shape, 1)
            if fox:
                vis = s_pos <= t_pos
            else:
                vis = (s_pos // CHUNK) <= (t_pos // CHUNK)
            s = jnp.where(vis & (s_pos < kv_valid), s, NEG)
        m_prev = m_ref[...]
        m_new = jnp.maximum(m_prev, jnp.max(s, axis=1, keepdims=True))
        alpha = jnp.exp2(m_prev - m_new)
        pexp = jnp.exp2(s - m_new)
        l_ref[...] = alpha * l_ref[...] + jnp.sum(pexp, axis=1, keepdims=True)
        pv = jnp.dot(pexp.astype(BF16), v_ref[0, pl.ds(ks, tk), :].astype(BF16), preferred_element_type=F32)
        acc_ref[...] = alpha * acc_ref[...] + pv
        m_ref[...] = m_new

    if fox:
        hi = q_start + tq
    else:
        hi = ((q_start + tq - 1) // CHUNK + 1) * CHUNK
    hi = jnp.minimum(hi, kv_valid)
    n_blocks = (hi + tk - 1) // tk
    n_full = q_start // tk

    def full_body(kb, carry):
        step(kb, False)
        return carry

    def masked_body(kb, carry):
        step(kb, True)
        return carry

    lax.fori_loop(0, n_full, full_body, 0)
    lax.fori_loop(n_full, n_blocks, masked_body, 0)

    out = acc_ref[...] / l_ref[...]
    o_ref[0] = jnp.where(lane < HEAD_DIM, out[:tq], out[tq:]).astype(o_ref.dtype)


def _attn(q1, q2, k1, k2, v, *, tq, tk, q_off, kv_valid, fox):
    b, t_q, hw = q1.shape
    t_k = k1.shape[1]
    pairs = hw // LANES
    qspec = pl.BlockSpec((1, tq, LANES), lambda bi, p, qi: (bi, qi, p))
    kspec = pl.BlockSpec((1, t_k, LANES), lambda bi, p, qi: (bi, 0, p))
    k2spec = pl.BlockSpec((1, t_k, LANES), lambda bi, p, qi: (bi, 0, 0))
    if fox:
        in_specs = [qspec, kspec, k2spec, kspec]
        args = (q1, k1, k2, v)
    else:
        q2spec = pl.BlockSpec((1, tq, LANES), lambda bi, p, qi: (bi, qi, p // 2))
        in_specs = [qspec, q2spec, kspec, k2spec, kspec]
        args = (q1, q2, k1, k2, v)
    return pl.pallas_call(
        functools.partial(_attn_kernel, tq=tq, tk=tk, q_off=q_off, kv_valid=kv_valid, fox=fox),
        out_shape=jax.ShapeDtypeStruct((b, t_q, hw), BF16),
        grid=(b, pairs, t_q // tq),
        in_specs=in_specs,
        out_specs=qspec,
        scratch_shapes=[pltpu.VMEM((2 * tq, 1), F32), pltpu.VMEM((2 * tq, 1), F32),
                        pltpu.VMEM((2 * tq, LANES), F32)],
        compiler_params=pltpu.CompilerParams(dimension_semantics=("arbitrary",) * 3,
                                             vmem_limit_bytes=VMEM_LIMIT),
        name="attn_fox" if fox else "attn_mla",
    )(*args)


def _attn_t_kernel(*refs, tq, tk, fox):
    n_in = 4 if fox else 5
    q1_ref = refs[0]
    q2_ref = None if fox else refs[1]
    k1_ref, k2_ref, v_ref, o_ref = refs[n_in - 3:n_in + 1]
    (qt_ref, m_ref, acc_ref, s0_ref, s1_ref, p0_ref, p1_ref, a0_ref, a1_ref,
     bm0_ref, bm1_ref) = refs[n_in + 1:]
    s_refs, p_refs, a_refs, bm_refs = (s0_ref, s1_ref), (p0_ref, p1_ref), (a0_ref, a1_ref), (bm0_ref, bm1_ref)
    p = pl.program_id(1)
    qi = pl.program_id(2)
    lane = lax.broadcasted_iota(jnp.int32, (tq, LANES), 1)
    q1 = q1_ref[0].astype(F32)
    q2 = jnp.ones_like(q1) if fox else q2_ref[0].astype(F32)

    def head_rows(a):
        main = jnp.where((lane >= a * HEAD_DIM) & (lane < (a + 1) * HEAD_DIM), q1, 0.0)
        if fox:
            lo, width = LOGF_LANE0 + 3 * (2 * p + a), 3
        else:
            lo, width = ROPE_DIM * (2 * (p % 2) + a), ROPE_DIM
        aux = jnp.where((lane >= lo) & (lane < lo + width), q2, 0.0)
        return jnp.concatenate([main, aux], axis=1)

    qt_ref[...] = jnp.concatenate([head_rows(0), head_rows(1)], axis=0).T.astype(BF16)
    m_ref[...] = jnp.full(m_ref.shape, NEG, F32)
    acc_ref[...] = jnp.zeros(acc_ref.shape, F32)
    p1_ref[...] = jnp.zeros(p1_ref.shape, BF16)
    a1_ref[...] = jnp.zeros(a1_ref.shape, F32)

    q_start = qi * tq

    col_tiles = [slice(j, j + COL_TILE) for j in range(0, 2 * tq, COL_TILE)]

    def key_block(t):
        ks = pl.multiple_of(t * tk, tk)
        return jnp.concatenate([k1_ref[0, pl.ds(ks, tk), :], k2_ref[0, pl.ds(ks, tk), :]], axis=1)

    def value_block_t(kb):
        ks = pl.multiple_of(kb * tk, tk)
        return v_ref[0, pl.ds(ks, tk), :].astype(F32).T.astype(BF16)

    def scores(t, k, par, cols, masked):
        s = jnp.dot(k, qt_ref[:, cols], preferred_element_type=F32)
        if masked:
            c = cols.start + lax.broadcasted_iota(jnp.int32, s.shape, 1)
            t_pos = q_start + jnp.where(c >= tq, c - tq, c)
            s_pos = t * tk + lax.broadcasted_iota(jnp.int32, s.shape, 0)
            vis = (s_pos <= t_pos) if fox else ((s_pos // CHUNK) <= (t_pos // CHUNK))
            s = jnp.where(vis, s, NEG)
        s_refs[par][:, cols] = s
        bm_refs[par][:, cols] = jnp.max(s, axis=0, keepdims=True)

    def softmax(par, cols):
        m_prev = m_ref[:, cols]
        m_new = jnp.maximum(m_prev, bm_refs[par][:, cols])
        a_refs[par][:, cols] = jnp.exp2(m_prev - m_new)
        m_ref[:, cols] = m_new
        p_refs[par][:, cols] = jnp.exp2((s_refs[par][:, cols] - m_new).astype(BF16))

    ones_rows = jnp.ones((ONES_ROWS, tk), BF16)

    def values(v_t, par, cols):
        head = cols.start // tq
        lhs = jnp.concatenate([v_t[head * HEAD_DIM:(head + 1) * HEAD_DIM], ones_rows], axis=0)
        pv = jnp.dot(lhs, p_refs[par][:, cols], preferred_element_type=F32)
        acc_ref[:, cols] = a_refs[par][:, cols] * acc_ref[:, cols] + pv[:ACC_ROWS]

    def tick(t, masked=False, with_scores=True, with_softmax=True):
        par = t % 2 if isinstance(t, int) else t.par
        tv = t if isinstance(t, int) else t.value
        k = key_block(tv) if with_scores else None
        v_t = value_block_t(jnp.maximum(tv - 2, 0))
        for cols in col_tiles:
            if with_softmax:
                softmax(1 - par, cols)
            if with_scores:
                scores(tv, k, par, cols, masked)
            values(v_t, par, cols)

    class Tick:
        def __init__(self, value, par):
            self.value, self.par = value, par

    last = qi

    @pl.when(qi == 0)
    def _():
        k0 = key_block(0)
        for cols in col_tiles:
            scores(0, k0, 0, cols, True)
        tick(1, with_scores=False)
        tick(2, with_scores=False, with_softmax=False)

    @pl.when(qi > 0)
    def _():
        k0 = key_block(0)
        for cols in col_tiles:
            scores(0, k0, 0, cols, False)
        unmasked = last - 1

        def tick_pair(j, carry):
            tick(Tick(2 * j + 1, 1))
            tick(Tick(2 * j + 2, 0))
            return carry

        lax.fori_loop(0, unmasked // 2, tick_pair, 0)

        @pl.when(unmasked % 2 == 1)
        def _():
            tick(Tick(last - 1, 1))
            tick(Tick(last, 0), masked=True)
            tick(Tick(last + 1, 1), with_scores=False)
            tick(Tick(last + 2, 0), with_scores=False, with_softmax=False)

        @pl.when(unmasked % 2 == 0)
        def _():
            tick(Tick(last, 1), masked=True)
            tick(Tick(last + 1, 0), with_scores=False)
            tick(Tick(last + 2, 1), with_scores=False, with_softmax=False)

    acc = acc_ref[...]
    out_t = jnp.concatenate([acc[:HEAD_DIM, :tq] / acc[HEAD_DIM:HEAD_DIM + 1, :tq],
                             acc[:HEAD_DIM, tq:] / acc[HEAD_DIM:HEAD_DIM + 1, tq:]], axis=0)
    o_ref[0] = out_t.T.astype(o_ref.dtype)


def _attn_t(q1, q2, k1, k2, v, *, tq, fox):
    b, t, hw = q1.shape
    tk = tq
    m2 = 2 * tq
    pairs = hw // LANES
    qspec = pl.BlockSpec((1, tq, LANES), lambda bi, p, qi: (bi, qi, p))
    kspec = pl.BlockSpec((1, t, LANES), lambda bi, p, qi: (bi, 0, p))
    k2spec = pl.BlockSpec((1, t, LANES), lambda bi, p, qi: (bi, 0, 0))
    if fox:
        in_specs = [qspec, kspec, k2spec, kspec]
        args = (q1, k1, k2, v)
    else:
        q2spec = pl.BlockSpec((1, tq, LANES), lambda bi, p, qi: (bi, qi, p // 2))
        in_specs = [qspec, q2spec, kspec, k2spec, kspec]
        args = (q1, q2, k1, k2, v)
    return pl.pallas_call(
        functools.partial(_attn_t_kernel, tq=tq, tk=tk, fox=fox),
        out_shape=jax.ShapeDtypeStruct((b, t, hw), BF16),
        grid=(b, pairs, t // tq),
        in_specs=in_specs,
        out_specs=qspec,
        scratch_shapes=[pltpu.VMEM((2 * LANES, m2), BF16),
                        pltpu.VMEM((1, m2), F32),
                        pltpu.VMEM((ACC_ROWS, m2), F32),
                        pltpu.VMEM((tk, m2), F32), pltpu.VMEM((tk, m2), F32),
                        pltpu.VMEM((tk, m2), BF16), pltpu.VMEM((tk, m2), BF16),
                        pltpu.VMEM((1, m2), F32), pltpu.VMEM((1, m2), F32),
                        pltpu.VMEM((1, m2), F32), pltpu.VMEM((1, m2), F32)],
        compiler_params=pltpu.CompilerParams(dimension_semantics=("arbitrary",) * 3,
                                             vmem_limit_bytes=VMEM_LIMIT),
        name="attn_t_fox" if fox else "attn_t_mla",
    )(*args)


def _rope_tables(pos):
    half = ROPE_DIM // 2
    freqs = ROPE_THETA ** (-jnp.arange(half, dtype=F32) / half)
    ang = pos.astype(F32)[:, None] * freqs[None, :]
    cos, sin = jnp.cos(ang), jnp.sin(ang)
    cos_t = jnp.tile(jnp.concatenate([cos, cos], axis=1), (1, HEADS))
    sin_t = jnp.tile(jnp.concatenate([-sin, sin], axis=1), (1, HEADS))
    return cos_t, sin_t


def _swap_halves(w):
    half = w.shape[-1] // 2
    return jnp.concatenate([w[..., half:], w[..., :half]], axis=-1)


def _prep_weights(w_in, b_forget, w_q_up, w_kv_up, w_out):
    d = w_in.shape[0]
    hw = HEADS * HEAD_DIM
    o = 0
    cq = w_in[:, o:o + Q_LORA]; o += Q_LORA
    ckv = w_in[:, o:o + KV_LORA]; o += KV_LORA
    kr = w_in[:, o:o + ROPE_DIM]; o += ROPE_DIM
    fq = w_in[:, o:o + hw]; o += hw
    fk = w_in[:, o:o + hw]; o += hw
    fv = w_in[:, o:o + hw]; o += hw
    gate = w_in[:, o:o + HEADS]
    reps = LANES // ROPE_DIM
    pad = LANES - 4 * HEADS
    gate128 = jnp.concatenate([gate, jnp.repeat(gate, 3, axis=1), jnp.zeros((d, pad), w_in.dtype)], axis=1)
    w_in_p = jnp.concatenate([cq, ckv, fq, fk, fv, jnp.tile(kr, (1, reps)), jnp.tile(_swap_halves(kr), (1, reps)),
                              gate128], axis=1).astype(BF16)
    bias128 = jnp.concatenate([b_forget, jnp.repeat(b_forget, 3), jnp.zeros((pad,), F32)])[None, :]
    wq = w_q_up.reshape(Q_LORA, HEADS, NOPE_DIM + ROPE_DIM)
    wq_rope = wq[:, :, NOPE_DIM:]
    w_q_p = jnp.concatenate([wq[:, :, :NOPE_DIM].reshape(Q_LORA, -1), wq_rope.reshape(Q_LORA, -1),
                             _swap_halves(wq_rope).reshape(Q_LORA, -1)], axis=1).astype(BF16)
    wkv = w_kv_up.reshape(KV_LORA, HEADS, NOPE_DIM + HEAD_DIM)
    w_kv_p = jnp.concatenate([wkv[:, :, :NOPE_DIM].reshape(KV_LORA, -1), wkv[:, :, NOPE_DIM:].reshape(KV_LORA, -1)],
                             axis=1).astype(BF16)
    w_o1 = w_out[:hw].astype(BF16)
    w_o2 = w_out[hw:].astype(BF16)
    return w_in_p, bias128, w_q_p, w_kv_p, w_o1, w_o2


def _expand_logf(lf):
    pad = LANES - 4 * HEADS
    return jnp.concatenate([lf, jnp.repeat(lf, 3, axis=-1), jnp.zeros(lf.shape[:-1] + (pad,), lf.dtype)], axis=-1)


def kernel(x_prompt, x_sample, cache_mla_ckv, cache_mla_krope, cache_fox_k, cache_fox_v, cache_fox_logf,
           g_ffn1_pre, g_ffn1_post, w_ffn1_gu, w_ffn1_down, g_mix_pre, g_mix_post, w_in, b_forget,
           g_q_latent, w_q_up, g_kv_latent, w_kv_up, w_out, g_ffn2_pre, g_ffn2_post, w_ffn2_gu, w_ffn2_down):
    depth = w_in.shape[0]
    bp, tp, d = x_prompt.shape
    bs, ts, _ = x_sample.shape
    past = cache_mla_ckv.shape[2]
    hw = HEADS * HEAD_DIM
    tq_p = 512
    tc_p = tc_s = 512
    tk_pad = -(-(past + ts) // tc_s) * tc_s

    cos_p, sin_p = _rope_tables(jnp.arange(tp))
    cos_s, sin_s = _rope_tables(past + jnp.arange(ts))
    cos_s, sin_s = jnp.tile(cos_s, (bs, 1)), jnp.tile(sin_s, (bs, 1))

    xp = x_prompt.reshape(bp * tp, d)
    xs = x_sample.reshape(bs * ts, d)
    tm_p = 512
    tm_s = bs * ts
    rows_p, rows_s = [], []

    def pad_keys(parts):
        n = sum(a.shape[1] for a in parts)
        parts = list(parts) + [jnp.zeros((bs, tk_pad - n, parts[0].shape[2]), parts[0].dtype)]
        return jnp.concatenate(parts, axis=1)

    for l in range(depth):
        d_ff = w_ffn1_down.shape[1]
        w1 = (w_ffn1_gu[l][:, :d_ff].astype(BF16), w_ffn1_gu[l][:, d_ff:].astype(BF16), w_ffn1_down[l].astype(BF16))
        w2 = (w_ffn2_gu[l][:, :d_ff].astype(BF16), w_ffn2_gu[l][:, d_ff:].astype(BF16), w_ffn2_down[l].astype(BF16))
        w_in_p, bias128, w_q_p, w_kv_p, w_o1, w_o2 = _prep_weights(w_in[l], b_forget[l], w_q_up[l], w_kv_up[l], w_out[l])
        g1 = (g_ffn1_pre[l][None, :], g_ffn1_post[l][None, :])
        g2 = (g_ffn2_pre[l][None, :], g_ffn2_post[l][None, :])
        gm_pre, gm_post = g_mix_pre[l][None, :], g_mix_post[l][None, :]
        gq, gkv = g_q_latent[l][None, :], g_kv_latent[l][None, :]

        hp = _ffn(xp, *g1, *w1, tm=tm_p)
        (ckv, krope, fk, fv, logf, qn, qr, kn, vm, kr4, fqb, fkb, fvb, lf128) = _proj(
            hp, cos_p, sin_p, gm_pre, w_in_p, bias128, gq, w_q_p, gkv, w_kv_p, tm=tm_p)
        sh = lambda a: a.reshape(bp, tp, a.shape[-1])
        ls = _lsplit(sh(lf128), tc=tc_p)
        o_mla = _attn_t(sh(qn), sh(qr), sh(kn), sh(kr4), sh(vm), tq=tq_p, fox=False)
        o_fox = _attn_t(sh(fqb), None, sh(fkb), ls, sh(fvb), tq=tq_p, fox=True)
        xp = _ffn(hp, *g2, *w2, mix=(o_mla.reshape(-1, hw), o_fox.reshape(-1, hw), w_o1, w_o2, gm_post), tm=tm_p)
        rows_p.append((ckv.reshape(bp, tp, KV_LORA), krope.reshape(bp, tp, ROPE_DIM),
                       fk.reshape(bp, tp, HEADS, HEAD_DIM), fv.reshape(bp, tp, HEADS, HEAD_DIM),
                       logf.reshape(bp, tp, HEADS)))

        hs = _ffn(xs, *g1, *w1, tm=tm_s)
        (ckv, krope, fk, fv, logf, qn, qr, kn, vm, kr4, fqb, _, _, lf128) = _proj(
            hs, cos_s, sin_s, gm_pre, w_in_p, bias128, gq, w_q_p, gkv, w_kv_p, tm=tm_s)
        kn_past, vm_past, kr4_past = _pastkv(cache_mla_ckv[l].reshape(bs * past, KV_LORA),
                                             cache_mla_krope[l].reshape(bs * past, ROPE_DIM), w_kv_p, tm=512)
        sh = lambda a: a.reshape(bs, -1, a.shape[-1])
        kn_all = pad_keys([sh(kn_past), sh(kn)])
        vm_all = pad_keys([sh(vm_past), sh(vm)])
        kr4_all = pad_keys([sh(kr4_past), sh(kr4)])
        fk_all = pad_keys([cache_fox_k[l].reshape(bs, past, hw), sh(fk)])
        fv_all = pad_keys([cache_fox_v[l].reshape(bs, past, hw), sh(fv)])
        lf_all = pad_keys([_expand_logf(cache_fox_logf[l].astype(F32)), sh(lf128)])
        ls = _lsplit(lf_all, tc=tc_s)
        kw = dict(tq=ts, tk=tk_pad, q_off=past, kv_valid=past + ts)
        o_mla = _attn(sh(qn), sh(qr), kn_all, kr4_all, vm_all, fox=False, **kw)
        o_fox = _attn(sh(fqb), None, fk_all, ls, fv_all, fox=True, **kw)
        xs = _ffn(hs, *g2, *w2, mix=(o_mla.reshape(-1, hw), o_fox.reshape(-1, hw), w_o1, w_o2, gm_post), tm=tm_s)
        rows_s.append((ckv.reshape(bs, ts, KV_LORA), krope.reshape(bs, ts, ROPE_DIM),
                       fk.reshape(bs, ts, HEADS, HEAD_DIM), fv.reshape(bs, ts, HEADS, HEAD_DIM),
                       logf.reshape(bs, ts, HEADS)))

    outs_p = [jnp.stack([r[i] for r in rows_p]) for i in range(5)]
    outs_s = [jnp.stack([r[i] for r in rows_s]) for i in range(5)]
    return (xp.reshape(bp, tp, d), xs.reshape(bs, ts, d), *outs_p, *outs_s)
```

```python
import functools

import jax
import jax.numpy as jnp
from jax import lax
from jax.experimental import pallas as pl
from jax.experimental.pallas import tpu as pltpu

EPS = 1e-6
CHUNK = 64
ROPE_THETA = 10000.0
HEADS = 8
NOPE_DIM = 64
ROPE_DIM = 32
HEAD_DIM = 64
Q_LORA = 384
KV_LORA = 256
LANES = 128
COL_TILE = 256
ONES_ROWS = 16
ACC_ROWS = HEAD_DIM + 8
LOG2E = 1.4426950408889634
NEG = -1e30
LOGF_LANE0 = 8
VMEM_LIMIT = 56 * 1024 * 1024

BF16 = jnp.bfloat16
F32 = jnp.float32


def _rms(x, g):
    ms = jnp.mean(x * x, axis=-1, keepdims=True)
    return x * lax.rsqrt(ms + EPS) * g


def _const_spec(shape):
    return pl.BlockSpec(shape, lambda *_: (0,) * len(shape), pipeline_mode=pl.Buffered(1))


def _ffn_kernel(*refs, ff_chunk, with_mix):
    if with_mix:
        (h_ref, o1_ref, o2_ref, wo1_ref, wo2_ref, gmix_ref,
         gpre_ref, gpost_ref, wg_ref, wu_ref, wd_ref, out_ref) = refs
        mix = jnp.dot(o1_ref[...], wo1_ref[...], preferred_element_type=F32)
        mix = mix + jnp.dot(o2_ref[...], wo2_ref[...], preferred_element_type=F32)
        x = h_ref[...] + _rms(mix, gmix_ref[...])
    else:
        x_ref, gpre_ref, gpost_ref, wg_ref, wu_ref, wd_ref, out_ref = refs
        x = x_ref[...]
    n = _rms(x, gpre_ref[...]).astype(BF16)
    d_ff = wg_ref.shape[1]
    acc = jnp.zeros(x.shape, F32)
    for c in range(d_ff // ff_chunk):
        cols = slice(c * ff_chunk, (c + 1) * ff_chunk)
        gate = jnp.dot(n, wg_ref[:, cols], preferred_element_type=F32)
        up = jnp.dot(n, wu_ref[:, cols], preferred_element_type=F32)
        act = (gate * jax.nn.sigmoid(gate) * up).astype(BF16)
        acc = acc + jnp.dot(act, wd_ref[cols, :], preferred_element_type=F32)
    out_ref[...] = x + 0.5 * _rms(acc, gpost_ref[...])


def _ffn(x, g_pre, g_post, w_g, w_u, w_d, mix=None, *, tm):
    n, d = x.shape
    d_ff = w_g.shape[1]
    row = lambda w: pl.BlockSpec((tm, w), lambda i: (i, 0))
    in_specs = [row(d)]
    args = [x]
    if mix is not None:
        o1, o2, wo1, wo2, g_mix = mix
        in_specs += [row(o1.shape[1]), row(o2.shape[1]), _const_spec(wo1.shape), _const_spec(wo2.shape),
                     _const_spec((1, d))]
        args += [o1, o2, wo1, wo2, g_mix]
    in_specs += [_const_spec((1, d)), _const_spec((1, d)), _const_spec(w_g.shape), _const_spec(w_u.shape),
                 _const_spec(w_d.shape)]
    args += [g_pre, g_post, w_g, w_u, w_d]
    return pl.pallas_call(
        functools.partial(_ffn_kernel, ff_chunk=256, with_mix=mix is not None),
        out_shape=jax.ShapeDtypeStruct((n, d), F32),
        grid=(n // tm,),
        in_specs=in_specs,
        out_specs=row(d),
        compiler_params=pltpu.CompilerParams(dimension_semantics=("arbitrary",), vmem_limit_bytes=VMEM_LIMIT),
        name="ffn_mix" if mix is not None else "ffn",
    )(*args)


_C_CQ = 0
_C_CKV = _C_CQ + Q_LORA
_C_FQ = _C_CKV + KV_LORA
_C_FK = _C_FQ + HEADS * HEAD_DIM
_C_FV = _C_FK + HEADS * HEAD_DIM
_C_KR = _C_FV + HEADS * HEAD_DIM
_C_KRS = _C_KR + LANES
_C_GATE = _C_KRS + LANES
_C_END = _C_GATE + LANES


def _log_sigmoid(x):
    return jnp.minimum(x, 0.0) - jnp.log(1.0 + jnp.exp(-jnp.abs(x)))


def _proj_kernel(*refs, values_t, stream_blocks):
    (h_ref, cos_ref, sin_ref, gpre_ref, win_ref, bias_ref, gq_ref, wq_ref, gkv_ref, wkv_ref) = refs[:10]
    n_in = 12 if values_t else 10
    (ckv_ref, krope_ref, fk_ref, fv_ref, logf_ref,
     qn_ref, qr_ref, kn_ref, vm_ref, kr4_ref, fqb_ref, fkb_ref, fvb_ref, lf128_ref) = refs[n_in:n_in + 14]
    nt_dims = (((1,), (1,)), ((), ()))
    hw = HEADS * HEAD_DIM
    u = _rms(h_ref[...], gpre_ref[...]).astype(BF16)
    proj = jnp.dot(u, win_ref[...], preferred_element_type=F32)
    cos = cos_ref[...]
    sin = sin_ref[...]

    cq = _rms(proj[:, _C_CQ:_C_CKV], gq_ref[...]).astype(BF16)
    q = jnp.dot(cq, wq_ref[...], preferred_element_type=F32)
    q_scale = (NOPE_DIM + ROPE_DIM) ** -0.5 * LOG2E
    rw = HEADS * ROPE_DIM
    qn_ref[...] = (q[:, :hw] * q_scale).astype(BF16)
    qr_ref[...] = ((q[:, hw:hw + rw] * cos + q[:, hw + rw:] * sin) * q_scale).astype(BF16)

    ckv = _rms(proj[:, _C_CKV:_C_FQ], gkv_ref[...])
    ckv_ref[...] = ckv
    ckv_b = ckv.astype(BF16)
    if values_t:
        kn_ref[...] = jnp.dot(ckv_b, wkv_ref[:, :hw], preferred_element_type=F32).astype(BF16)
        vm_ref[0] = lax.dot_general(refs[10][...], ckv_b, nt_dims, preferred_element_type=F32).astype(BF16)
    else:
        kv = jnp.dot(ckv_b, wkv_ref[...], preferred_element_type=F32)
        kn_ref[...] = kv[:, :hw].astype(BF16)
        vm_ref[...] = kv[:, hw:].astype(BF16)
    kr4 = proj[:, _C_KR:_C_KRS] * cos[:, :LANES] + proj[:, _C_KRS:_C_GATE] * sin[:, :LANES]
    krope_ref[...] = kr4[:, :ROPE_DIM]
    kr4_ref[...] = kr4.astype(BF16)

    fq = proj[:, _C_FQ:_C_FK]
    fk = proj[:, _C_FK:_C_FV]
    fv = proj[:, _C_FV:_C_KR]
    fqb_ref[...] = (fq * (HEAD_DIM ** -0.5 * LOG2E)).astype(BF16)
    if values_t:
        def by_head(x):
            heads = jnp.stack([x[:, h * HEAD_DIM:(h + 1) * HEAD_DIM] for h in range(HEADS)], axis=0)
            return jnp.swapaxes(heads, 0, 1)
        fk_ref[...] = by_head(fk)
        fv_ref[...] = by_head(fv)
    else:
        fk_ref[...] = fk
        fv_ref[...] = fv
    fkb_ref[...] = fk.astype(BF16)
    if values_t:
        fvb_ref[0] = lax.dot_general(refs[11][...], u, nt_dims, preferred_element_type=F32).astype(BF16)
    else:
        fvb_ref[...] = fv.astype(BF16)
    logf = _log_sigmoid(proj[:, _C_GATE:_C_END] + bias_ref[...])
    logf_ref[...] = logf[:, :HEADS]
    if values_t:
        carry_ref = refs[-1]

        @pl.when(pl.program_id(0) % stream_blocks == 0)
        def _():
            carry_ref[...] = jnp.zeros_like(carry_ref)

        lf128_ref[...] = _cumsum_split(logf, carry_ref)
    else:
        lf128_ref[...] = logf


def _proj(h, cos, sin, g_pre, w_in, bias, g_q, w_q, g_kv, w_kv, values_t_weights=None, *, tm):
    n, d = h.shape
    t_blocks = cos.shape[0] // tm
    hw = HEADS * HEAD_DIM
    values_t = values_t_weights is not None
    row = lambda w: pl.BlockSpec((tm, w), lambda i: (i, 0))
    tab = pl.BlockSpec((tm, HEADS * ROPE_DIM), lambda i: (i % t_blocks, 0))
    widths = [(KV_LORA, F32), (ROPE_DIM, F32), (hw, F32), (hw, F32), (HEADS, F32),
              (hw, BF16), (HEADS * ROPE_DIM, BF16), (hw, BF16), (hw, BF16), (LANES, BF16),
              (hw, BF16), (hw, BF16), (hw, BF16), (LANES, F32)]
    out_shape = [jax.ShapeDtypeStruct((n, w), dt) for w, dt in widths]
    out_specs = [row(w) for w, _ in widths]
    in_specs = [row(d), tab, tab, _const_spec((1, d)), _const_spec(w_in.shape), _const_spec((1, LANES)),
                _const_spec((1, Q_LORA)), _const_spec(w_q.shape), _const_spec((1, KV_LORA)),
                _const_spec(w_kv.shape)]
    args = [h, cos, sin, g_pre, w_in, bias, g_q, w_q, g_kv, w_kv]
    if values_t:
        for i in (8, 12):
            out_shape[i] = jax.ShapeDtypeStruct((n // tm, hw, tm), BF16)
            out_specs[i] = pl.BlockSpec((1, hw, tm), lambda i: (i, 0, 0))
        out_shape[13] = jax.ShapeDtypeStruct((n, LANES), BF16)
        for i in (2, 3):
            out_shape[i] = jax.ShapeDtypeStruct((n, HEADS, HEAD_DIM), F32)
            out_specs[i] = pl.BlockSpec((tm, HEADS, HEAD_DIM), lambda i: (i, 0, 0))
        in_specs += [_const_spec(w.shape) for w in values_t_weights]
        args += list(values_t_weights)
    return pl.pallas_call(
        functools.partial(_proj_kernel, values_t=values_t, stream_blocks=t_blocks),
        out_shape=out_shape,
        grid=(n // tm,),
        in_specs=in_specs,
        out_specs=out_specs,
        scratch_shapes=[pltpu.VMEM((1, LANES), F32)] if values_t else [],
        compiler_params=pltpu.CompilerParams(dimension_semantics=("arbitrary",), vmem_limit_bytes=VMEM_LIMIT),
        name="proj",
    )(*args)


def _pastkv_kernel(ckv_ref, kr_ref, wkv_ref, kn_ref, vm_ref, kr4_ref):
    hw = HEADS * HEAD_DIM
    kv = jnp.dot(ckv_ref[...].astype(BF16), wkv_ref[...], preferred_element_type=F32)
    kn_ref[...] = kv[:, :hw].astype(BF16)
    vm_ref[...] = kv[:, hw:].astype(BF16)
    src = lax.broadcasted_iota(jnp.int32, (ROPE_DIM, LANES), 0)
    dst = lax.broadcasted_iota(jnp.int32, (ROPE_DIM, LANES), 1)
    rep = (dst % ROPE_DIM == src).astype(BF16)
    kr4_ref[...] = jnp.dot(kr_ref[...].astype(BF16), rep, preferred_element_type=F32).astype(BF16)


def _pastkv(ckv, krope, w_kv, *, tm):
    n = ckv.shape[0]
    hw = HEADS * HEAD_DIM
    row = lambda w: pl.BlockSpec((tm, w), lambda i: (i, 0))
    return pl.pallas_call(
        _pastkv_kernel,
        out_shape=[jax.ShapeDtypeStruct((n, hw), BF16), jax.ShapeDtypeStruct((n, hw), BF16),
                   jax.ShapeDtypeStruct((n, LANES), BF16)],
        grid=(n // tm,),
        in_specs=[row(KV_LORA), row(ROPE_DIM), _const_spec(w_kv.shape)],
        out_specs=[row(hw), row(hw), row(LANES)],
        compiler_params=pltpu.CompilerParams(dimension_semantics=("arbitrary",), vmem_limit_bytes=VMEM_LIMIT),
        name="pastkv",
    )(ckv, krope, w_kv)


def _split3(y, lane):
    hi = y.astype(BF16).astype(F32)
    r1 = y - hi
    mid = r1.astype(BF16).astype(F32)
    lo = r1 - mid
    j = (lane - LOGF_LANE0) % 3
    sel = jnp.where(j == 0, hi, jnp.where(j == 1, mid, lo))
    used = (lane >= LOGF_LANE0) & (lane < LOGF_LANE0 + 3 * HEADS)
    return jnp.where(used, sel, 0.0).astype(BF16)


def _cumsum_split(x, carry_ref):
    tc = x.shape[0]
    hi = x.astype(BF16)
    r1 = x - hi.astype(F32)
    mid = r1.astype(BF16)
    lo = (r1 - mid.astype(F32)).astype(BF16)
    r = lax.broadcasted_iota(jnp.int32, (tc, tc), 0)
    c = lax.broadcasted_iota(jnp.int32, (tc, tc), 1)
    tri = (c <= r).astype(BF16)
    cum = (jnp.dot(tri, hi, preferred_element_type=F32) + jnp.dot(tri, mid, preferred_element_type=F32)
           + jnp.dot(tri, lo, preferred_element_type=F32)) + carry_ref[...]
    carry_ref[...] = cum[tc - 1:tc, :]
    lane = lax.broadcasted_iota(jnp.int32, (tc, LANES), 1)
    return _split3(cum * (-LOG2E), lane)


def _lsplit_kernel(x_ref, o_ref, carry_ref):
    @pl.when(pl.program_id(1) == 0)
    def _():
        carry_ref[...] = jnp.zeros_like(carry_ref)

    o_ref[0] = _cumsum_split(x_ref[0], carry_ref)


def _lsplit(lf128, *, tc):
    b, t, _ = lf128.shape
    spec = pl.BlockSpec((1, tc, LANES), lambda i, j: (i, j, 0))
    return pl.pallas_call(
        _lsplit_kernel,
        out_shape=jax.ShapeDtypeStruct((b, t, LANES), BF16),
        grid=(b, t // tc),
        in_specs=[spec],
        out_specs=spec,
        scratch_shapes=[pltpu.VMEM((1, LANES), F32)],
        compiler_params=pltpu.CompilerParams(dimension_semantics=("arbitrary", "arbitrary")),
        name="lsplit",
    )(lf128)


def _attn_kernel(*refs, tq, tk, q_off, kv_valid, fox):
    if fox:
        q1_ref, k1_ref, k2_ref, v_ref, o_ref, m_ref, l_ref, acc_ref = refs
    else:
        q1_ref, q2_ref, k1_ref, k2_ref, v_ref, o_ref, m_ref, l_ref, acc_ref = refs
    p = pl.program_id(1)
    qi = pl.program_id(2)
    lane = lax.broadcasted_iota(jnp.int32, (tq, LANES), 1)
    q1 = q1_ref[0].astype(F32)
    q2 = jnp.ones_like(q1) if fox else q2_ref[0].astype(F32)

    def head_rows(a):
        main = jnp.where((lane >= a * HEAD_DIM) & (lane < (a + 1) * HEAD_DIM), q1, 0.0)
        if fox:
            lo, width = LOGF_LANE0 + 3 * (2 * p + a), 3
        else:
            lo, width = ROPE_DIM * (2 * (p % 2) + a), ROPE_DIM
        aux = jnp.where((lane >= lo) & (lane < lo + width), q2, 0.0)
        return jnp.concatenate([main, aux], axis=1).astype(BF16)

    q = jnp.concatenate([head_rows(0), head_rows(1)], axis=0)

    m_ref[...] = jnp.full(m_ref.shape, NEG, F32)
    l_ref[...] = jnp.zeros(l_ref.shape, F32)
    acc_ref[...] = jnp.zeros(acc_ref.shape, F32)

    q_start = q_off + qi * tq

    def step(kb, masked):
        ks = pl.multiple_of(kb * tk, tk)
        k = jnp.concatenate([k1_ref[0, pl.ds(ks, tk), :].astype(BF16),
                             k2_ref[0, pl.ds(ks, tk), :].astype(BF16)], axis=1)
        s = lax.dot_general(q, k, (((1,), (1,)), ((), ())), preferred_element_type=F32)
        if masked:
            r = lax.broadcasted_iota(jnp.int32, s.shape, 0)
            t_pos = q_start + jnp.where(r >= tq, r - tq, r)
            s_pos = ks + lax.broadcasted_iota(jnp.int32, s.shape, 1)
            if fox:
                vis = s_pos <= t_pos
            else:
                vis = (s_pos // CHUNK) <= (t_pos // CHUNK)
            s = jnp.where(vis & (s_pos < kv_valid), s, NEG)
        m_prev = m_ref[...]
        m_new = jnp.maximum(m_prev, jnp.max(s, axis=1, keepdims=True))
        alpha = jnp.exp2(m_prev - m_new)
        pexp = jnp.exp2(s - m_new)
        l_ref[...] = alpha * l_ref[...] + jnp.sum(pexp, axis=1, keepdims=True)
        pv = jnp.dot(pexp.astype(BF16), v_ref[0, pl.ds(ks, tk), :].astype(BF16), preferred_element_type=F32)
        acc_ref[...] = alpha * acc_ref[...] + pv
        m_ref[...] = m_new

    if fox:
        hi = q_start + tq
    else:
        hi = ((q_start + tq - 1) // CHUNK + 1) * CHUNK
    hi = jnp.minimum(hi, kv_valid)
    n_blocks = (hi + tk - 1) // tk
    n_full = q_start // tk

    def full_body(kb, carry):
        step(kb, False)
        return carry

    def masked_body(kb, carry):
        step(kb, True)
        return carry

    lax.fori_loop(0, n_full, full_body, 0)
    lax.fori_loop(n_full, n_blocks, masked_body, 0)

    out = acc_ref[...] / l_ref[...]
    o_ref[0] = jnp.where(lane < HEAD_DIM, out[:tq], out[tq:]).astype(o_ref.dtype)


def _attn(q1, q2, k1, k2, v, *, tq, tk, q_off, kv_valid, fox):
    b, t_q, hw = q1.shape
    t_k = k1.shape[1]
    pairs = hw // LANES
    qspec = pl.BlockSpec((1, tq, LANES), lambda bi, p, qi: (bi, qi, p))
    kspec = pl.BlockSpec((1, t_k, LANES), lambda bi, p, qi: (bi, 0, p))
    k2spec = pl.BlockSpec((1, t_k, LANES), lambda bi, p, qi: (bi, 0, 0))
    if fox:
        in_specs = [qspec, kspec, k2spec, kspec]
        args = (q1, k1, k2, v)
    else:
        q2spec = pl.BlockSpec((1, tq, LANES), lambda bi, p, qi: (bi, qi, p // 2))
        in_specs = [qspec, q2spec, kspec, k2spec, kspec]
        args = (q1, q2, k1, k2, v)
    return pl.pallas_call(
        functools.partial(_attn_kernel, tq=tq, tk=tk, q_off=q_off, kv_valid=kv_valid, fox=fox),
        out_shape=jax.ShapeDtypeStruct((b, t_q, hw), BF16),
        grid=(b, pairs, t_q // tq),
        in_specs=in_specs,
        out_specs=qspec,
        scratch_shapes=[pltpu.VMEM((2 * tq, 1), F32), pltpu.VMEM((2 * tq, 1), F32),
                        pltpu.VMEM((2 * tq, LANES), F32)],
        compiler_params=pltpu.CompilerParams(dimension_semantics=("arbitrary",) * 3,
                                             vmem_limit_bytes=VMEM_LIMIT),
        name="attn_fox" if fox else "attn_mla",
    )(*args)


def _attn_t_kernel(*refs, tq, tk, fox):
    n_in = 4 if fox else 5
    q1_ref = refs[0]
    q2_ref = None if fox else refs[1]
    k1_ref, k2_ref, v_ref, o_ref = refs[n_in - 3:n_in + 1]
    (qt_ref, m_ref, acc_ref, s0_ref, s1_ref, p0_ref, p1_ref, a0_ref, a1_ref,
     bm0_ref, bm1_ref) = refs[n_in + 1:]
    s_refs, p_refs, a_refs, bm_refs = (s0_ref, s1_ref), (p0_ref, p1_ref), (a0_ref, a1_ref), (bm0_ref, bm1_ref)
    p = pl.program_id(1)
    qi = pl.program_id(2)
    lane = lax.broadcasted_iota(jnp.int32, (tq, LANES), 1)
    q1 = q1_ref[0].astype(F32)
    q2 = jnp.ones_like(q1) if fox else q2_ref[0].astype(F32)

    def head_rows(a):
        main = jnp.where((lane >= a * HEAD_DIM) & (lane < (a + 1) * HEAD_DIM), q1, 0.0)
        if fox:
            lo, width = LOGF_LANE0 + 3 * (2 * p + a), 3
        else:
            lo, width = ROPE_DIM * (2 * (p % 2) + a), ROPE_DIM
        aux = jnp.where((lane >= lo) & (lane < lo + width), q2, 0.0)
        return jnp.concatenate([main, aux], axis=1)

    qt_ref[...] = jnp.concatenate([head_rows(0), head_rows(1)], axis=0).T.astype(BF16)
    m_ref[...] = jnp.full(m_ref.shape, NEG, F32)
    acc_ref[...] = jnp.zeros(acc_ref.shape, F32)
    p1_ref[...] = jnp.zeros(p1_ref.shape, BF16)
    a1_ref[...] = jnp.zeros(a1_ref.shape, F32)

    q_start = qi * tq

    col_tiles = [slice(j, j + COL_TILE) for j in range(0, 2 * tq, COL_TILE)]

    def key_block(t):
        ks = pl.multiple_of(t * tk, tk)
        return jnp.concatenate([k1_ref[0, pl.ds(ks, tk), :], k2_ref[0, pl.ds(ks, tk), :]], axis=1)

    def value_block_t(kb):
        ks = pl.multiple_of(kb * tk, tk)
        return v_ref[0, pl.ds(ks, tk), :].astype(F32).T.astype(BF16)

    def scores(t, k, par, cols, masked):
        s = jnp.dot(k, qt_ref[:, cols], preferred_element_type=F32)
        if masked:
            c = cols.start + lax.broadcasted_iota(jnp.int32, s.shape, 1)
            t_pos = q_start + jnp.where(c >= tq, c - tq, c)
            s_pos = t * tk + lax.broadcasted_iota(jnp.int32, s.shape, 0)
            vis = (s_pos <= t_pos) if fox else ((s_pos // CHUNK) <= (t_pos // CHUNK))
            s = jnp.where(vis, s, NEG)
        s_refs[par][:, cols] = s
        bm_refs[par][:, cols] = jnp.max(s, axis=0, keepdims=True)

    def softmax(par, cols):
        m_prev = m_ref[:, cols]
        m_new = jnp.maximum(m_prev, bm_refs[par][:, cols])
        a_refs[par][:, cols] = jnp.exp2(m_prev - m_new)
        m_ref[:, cols] = m_new
        p_refs[par][:, cols] = jnp.exp2((s_refs[par][:, cols] - m_new).astype(BF16))

    ones_rows = jnp.ones((ONES_ROWS, tk), BF16)

    def values(v_t, par, cols):
        head = cols.start // tq
        lhs = jnp.concatenate([v_t[head * HEAD_DIM:(head + 1) * HEAD_DIM], ones_rows], axis=0)
        pv = jnp.dot(lhs, p_refs[par][:, cols], preferred_element_type=F32)
        acc_ref[:, cols] = a_refs[par][:, cols] * acc_ref[:, cols] + pv[:ACC_ROWS]

    def tick(t, masked=False, with_scores=True, with_softmax=True):
        par = t % 2 if isinstance(t, int) else t.par
        tv = t if isinstance(t, int) else t.value
        k = key_block(tv) if with_scores else None
        v_t = value_block_t(jnp.maximum(tv - 2, 0))
        for cols in col_tiles:
            if with_softmax:
                softmax(1 - par, cols)
            if with_scores:
                scores(tv, k, par, cols, masked)
            values(v_t, par, cols)

    class Tick:
        def __init__(self, value, par):
            self.value, self.par = value, par

    last = qi

    @pl.when(qi == 0)
    def _():
        k0 = key_block(0)
        for cols in col_tiles:
            scores(0, k0, 0, cols, True)
        tick(1, with_scores=False)
        tick(2, with_scores=False, with_softmax=False)

    @pl.when(qi > 0)
    def _():
        k0 = key_block(0)
        for cols in col_tiles:
            scores(0, k0, 0, cols, False)
        unmasked = last - 1

        def tick_pair(j, carry):
            tick(Tick(2 * j + 1, 1))
            tick(Tick(2 * j + 2, 0))
            return carry

        lax.fori_loop(0, unmasked // 2, tick_pair, 0)

        @pl.when(unmasked % 2 == 1)
        def _():
            tick(Tick(last - 1, 1))
            tick(Tick(last, 0), masked=True)
            tick(Tick(last + 1, 1), with_scores=False)
            tick(Tick(last + 2, 0), with_scores=False, with_softmax=False)

        @pl.when(unmasked % 2 == 0)
        def _():
            tick(Tick(last, 1), masked=True)
            tick(Tick(last + 1, 0), with_scores=False)
            tick(Tick(last + 2, 1), with_scores=False, with_softmax=False)

    acc = acc_ref[...]
    out_t = jnp.concatenate([acc[:HEAD_DIM, :tq] / acc[HEAD_DIM:HEAD_DIM + 1, :tq],
                             acc[:HEAD_DIM, tq:] / acc[HEAD_DIM:HEAD_DIM + 1, tq:]], axis=0)
    o_ref[0] = out_t.T.astype(o_ref.dtype)


def _attn_t(q1, q2, k1, k2, v, *, tq, fox):
    b, t, hw = q1.shape
    tk = tq
    m2 = 2 * tq
    pairs = hw // LANES
    qspec = pl.BlockSpec((1, tq, LANES), lambda bi, p, qi: (bi, qi, p))
    kspec = pl.BlockSpec((1, t, LANES), lambda bi, p, qi: (bi, 0, p))
    k2spec = pl.BlockSpec((1, t, LANES), lambda bi, p, qi: (bi, 0, 0))
    if fox:
        in_specs = [qspec, kspec, k2spec, kspec]
        args = (q1, k1, k2, v)
    else:
        q2spec = pl.BlockSpec((1, tq, LANES), lambda bi, p, qi: (bi, qi, p // 2))
        in_specs = [qspec, q2spec, kspec, k2spec, kspec]
        args = (q1, q2, k1, k2, v)
    return pl.pallas_call(
        functools.partial(_attn_t_kernel, tq=tq, tk=tk, fox=fox),
        out_shape=jax.ShapeDtypeStruct((b, t, hw), BF16),
        grid=(b, pairs, t // tq),
        in_specs=in_specs,
        out_specs=qspec,
        scratch_shapes=[pltpu.VMEM((2 * LANES, m2), BF16),
                        pltpu.VMEM((1, m2), F32),
                        pltpu.VMEM((ACC_ROWS, m2), F32),
                        pltpu.VMEM((tk, m2), F32), pltpu.VMEM((tk, m2), F32),
                        pltpu.VMEM((tk, m2), BF16), pltpu.VMEM((tk, m2), BF16),
                        pltpu.VMEM((1, m2), F32), pltpu.VMEM((1, m2), F32),
                        pltpu.VMEM((1, m2), F32), pltpu.VMEM((1, m2), F32)],
        compiler_params=pltpu.CompilerParams(dimension_semantics=("arbitrary",) * 3,
                                             vmem_limit_bytes=VMEM_LIMIT),
        name="attn_t_fox" if fox else "attn_t_mla",
    )(*args)


class _Item:
    def __init__(self, q, kb, qpar, kpar, first=False, last=False):
        self.q, self.kb, self.qpar, self.kpar, self.first, self.last = q, kb, qpar, kpar, first, last
        self.slot = 2 * qpar + kpar


def _attn_flat_kernel(*refs, tq, nq, fox):
    n_in = 4 if fox else 5
    q1_ref = refs[0]
    q2_ref = None if fox else refs[1]
    k1_ref, k2_ref, vt_ref, o_ref = refs[n_in - 3:n_in + 1]
    scratch = refs[n_in + 1:]
    qt_refs, m_refs, acc_refs = scratch[0:2], scratch[2:4], scratch[4:6]
    s_refs, p_refs, a_refs, bm_refs = scratch[6:10], scratch[10:14], scratch[14:18], scratch[18:22]
    tk = tq
    pair = pl.program_id(1)
    col_tiles = [slice(j, j + COL_TILE) for j in range(0, 2 * tq, COL_TILE)]
    ones_rows = jnp.ones((ONES_ROWS, tk), BF16)

    def rows_of(blk):
        start = blk * tq
        return pl.ds(start if isinstance(start, int) else pl.multiple_of(start, tq), tq)

    def setup(q, qpar):
        lane = lax.broadcasted_iota(jnp.int32, (tq, LANES), 1)
        q1 = q1_ref[0, rows_of(q), :].astype(F32)
        q2 = jnp.ones_like(q1) if fox else q2_ref[0, rows_of(q), :].astype(F32)

        def head_rows(a):
            main = jnp.where((lane >= a * HEAD_DIM) & (lane < (a + 1) * HEAD_DIM), q1, 0.0)
            if fox:
                lo, width = LOGF_LANE0 + 3 * (2 * pair + a), 3
            else:
                lo, width = ROPE_DIM * (2 * (pair % 2) + a), ROPE_DIM
            aux = jnp.where((lane >= lo) & (lane < lo + width), q2, 0.0)
            return jnp.concatenate([main, aux], axis=1)

        qt_refs[qpar][...] = jnp.concatenate([head_rows(0), head_rows(1)], axis=0).T.astype(BF16)

    def key_block(kb):
        return jnp.concatenate([k1_ref[0, rows_of(kb), :], k2_ref[0, rows_of(kb), :]], axis=1)

    def scores(x, cols):
        s = jnp.dot(key_block(x.kb), qt_refs[x.qpar][:, cols], preferred_element_type=F32)
        if x.last:
            c = (cols.start % tq) + lax.broadcasted_iota(jnp.int32, s.shape, 1)
            r = lax.broadcasted_iota(jnp.int32, s.shape, 0)
            vis = (r <= c) if fox else ((r // CHUNK) <= (c // CHUNK))
            s = jnp.where(vis, s, NEG)
        s_refs[x.slot][cols.start // COL_TILE] = s
        bm_refs[x.slot][:, cols] = jnp.max(s, axis=0, keepdims=True)

    def softmax(x, cols):
        bm = bm_refs[x.slot][:, cols]
        if x.first:
            m_new = bm
            a_refs[x.slot][:, cols] = jnp.zeros_like(bm)
        else:
            m_prev = m_refs[x.qpar][:, cols]
            m_new = jnp.maximum(m_prev, bm)
            a_refs[x.slot][:, cols] = jnp.exp2(m_prev - m_new)
        m_refs[x.qpar][:, cols] = m_new
        ct = cols.start // COL_TILE
        p_refs[x.slot][ct] = jnp.exp2(s_refs[x.slot][ct] - m_new).astype(BF16)

    def values(x, cols):
        head = cols.start // tq
        v_t = vt_ref[0, x.kb, head * HEAD_DIM:(head + 1) * HEAD_DIM, :]
        lhs = jnp.concatenate([v_t, ones_rows], axis=0)
        pv = jnp.dot(lhs, p_refs[x.slot][cols.start // COL_TILE], preferred_element_type=F32)[:ACC_ROWS]
        acc_ref = acc_refs[x.qpar]
        if x.first:
            acc_ref[:, cols] = pv
        else:
            acc_ref[:, cols] = a_refs[x.slot][:, cols] * acc_ref[:, cols] + pv

    def finalize(x):
        acc = acc_refs[x.qpar][...]
        out_t = jnp.concatenate([acc[:HEAD_DIM, :tq] / acc[HEAD_DIM:HEAD_DIM + 1, :tq],
                                 acc[:HEAD_DIM, tq:] / acc[HEAD_DIM:HEAD_DIM + 1, tq:]], axis=0)
        o_ref[0, rows_of(x.q), :] = out_t.T.astype(o_ref.dtype)

    def tick(xs, xm, xv, next_q=None):
        for cols in col_tiles:
            if xs is not None:
                scores(xs, cols)
            if xv is not None:
                values(xv, cols)
            if xm is not None:
                softmax(xm, cols)
        if xv is not None and xv.last:
            finalize(xv)
        if next_q is not None:
            setup(*next_q)

    def generic_pairs(q, qpar, count):
        def body(i, carry):
            kb = 3 + 2 * i
            x_a, x_b = _Item(q, kb, qpar, 1), _Item(q, kb + 1, qpar, 0)
            tick(x_a, _Item(q, kb - 1, qpar, 0), _Item(q, kb - 2, qpar, 1))
            tick(x_b, x_a, _Item(q, kb - 1, qpar, 0))
            return carry
        lax.fori_loop(0, count, body, 0)

    def query_block(q, qpar, is_last_q=False):
        p2 = _Item(q - 1, q - 2, 1 - qpar, qpar)
        p1 = _Item(q - 1, q - 1, 1 - qpar, 1 - qpar, last=True)
        x0 = _Item(q, 0, qpar, 0, first=True)
        x1 = _Item(q, 1, qpar, 1)
        x2 = _Item(q, 2, qpar, 0)
        tick(x0, p1, p2)
        tick(x1, x0, p1)
        tick(x2, x1, x0)
        nxt = None if is_last_q else (q + 1, 1 - qpar)
        if qpar == 1:
            generic_pairs(q, qpar, (q - 3) // 2)
            a2, a1 = _Item(q, q - 2, qpar, 1), _Item(q, q - 1, qpar, 0)
        else:
            generic_pairs(q, qpar, (q - 4) // 2)
            a1 = _Item(q, q - 1, qpar, 1)
            a2 = _Item(q, q - 2, qpar, 0)
            tick(a1, a2, _Item(q, q - 3, qpar, 1))
        xl = _Item(q, q, qpar, qpar, last=True)
        tick(xl, a1, a2, next_q=nxt)
        return a1, xl

    setup(0, 0)
    x00 = _Item(0, 0, 0, 0, first=True, last=True)
    x10, x11 = _Item(1, 0, 1, 0, first=True), _Item(1, 1, 1, 1, last=True)
    x20, x21, x22 = _Item(2, 0, 0, 0, first=True), _Item(2, 1, 0, 1), _Item(2, 2, 0, 0, last=True)
    tick(x00, None, None, next_q=(1, 1))
    tick(x10, x00, None)
    tick(x11, x10, x00, next_q=(2, 0))
    tick(x20, x11, x10)
    tick(x21, x20, x11)
    tick(x22, x21, x20, next_q=(3, 1))

    def block_pair(j, carry):
        query_block(3 + 2 * j, 1)
        query_block(4 + 2 * j, 0)
        return carry

    lax.fori_loop(0, (nq - 4) // 2, block_pair, 0)
    a1, xl = query_block(nq - 1, 1, is_last_q=True)
    tick(None, xl, a1)
    tick(None, None, xl)


def _attn_flat(q1, q2, k1, k2, v_t, *, tq, fox):
    b, t, hw = q1.shape
    nq = t // tq
    assert t % tq == 0 and nq % 2 == 0 and nq >= 4, (t, tq)
    assert v_t.shape == (b, nq, hw, tq), v_t.shape
    m2 = 2 * tq
    pairs = hw // LANES
    spec = pl.BlockSpec((1, t, LANES), lambda bi, p: (bi, 0, p))
    k2spec = pl.BlockSpec((1, t, LANES), lambda bi, p: (bi, 0, 0))
    vspec = pl.BlockSpec((1, nq, LANES, tq), lambda bi, p: (bi, 0, p, 0))
    if fox:
        in_specs = [spec, spec, k2spec, vspec]
        args = (q1, k1, k2, v_t)
    else:
        q2spec = pl.BlockSpec((1, t, LANES), lambda bi, p: (bi, 0, p // 2))
        in_specs = [spec, q2spec, spec, k2spec, vspec]
        args = (q1, q2, k1, k2, v_t)
    vmem = lambda shape, dt, n: [pltpu.VMEM(shape, dt) for _ in range(n)]
    return pl.pallas_call(
        functools.partial(_attn_flat_kernel, tq=tq, nq=nq, fox=fox),
        out_shape=jax.ShapeDtypeStruct((b, t, hw), BF16),
        grid=(b, pairs),
        in_specs=in_specs,
        out_specs=spec,
        scratch_shapes=(vmem((2 * LANES, m2), BF16, 2)
                        + vmem((1, m2), F32, 2)
                        + vmem((ACC_ROWS, m2), F32, 2)
                        + vmem((m2 // COL_TILE, tq, COL_TILE), F32, 4)
                        + vmem((m2 // COL_TILE, tq, COL_TILE), BF16, 4)
                        + vmem((1, m2), F32, 4)
                        + vmem((1, m2), F32, 4)),
        compiler_params=pltpu.CompilerParams(dimension_semantics=("arbitrary",) * 2,
                                             vmem_limit_bytes=VMEM_LIMIT),
        name="attn_flat_fox" if fox else "attn_flat_mla",
    )(*args)


def _rope_tables(pos):
    half = ROPE_DIM // 2
    freqs = ROPE_THETA ** (-jnp.arange(half, dtype=F32) / half)
    ang = pos.astype(F32)[:, None] * freqs[None, :]
    cos, sin = jnp.cos(ang), jnp.sin(ang)
    cos_t = jnp.tile(jnp.concatenate([cos, cos], axis=1), (1, HEADS))
    sin_t = jnp.tile(jnp.concatenate([-sin, sin], axis=1), (1, HEADS))
    return cos_t, sin_t


def _swap_halves(w):
    half = w.shape[-1] // 2
    return jnp.concatenate([w[..., half:], w[..., :half]], axis=-1)


def _prep_weights(w_in, b_forget, w_q_up, w_kv_up, w_out):
    d = w_in.shape[0]
    hw = HEADS * HEAD_DIM
    o = 0
    cq = w_in[:, o:o + Q_LORA]; o += Q_LORA
    ckv = w_in[:, o:o + KV_LORA]; o += KV_LORA
    kr = w_in[:, o:o + ROPE_DIM]; o += ROPE_DIM
    fq = w_in[:, o:o + hw]; o += hw
    fk = w_in[:, o:o + hw]; o += hw
    fv = w_in[:, o:o + hw]; o += hw
    gate = w_in[:, o:o + HEADS]
    reps = LANES // ROPE_DIM
    pad = LANES - 4 * HEADS
    gate128 = jnp.concatenate([gate, jnp.repeat(gate, 3, axis=1), jnp.zeros((d, pad), w_in.dtype)], axis=1)
    w_in_p = jnp.concatenate([cq, ckv, fq, fk, fv, jnp.tile(kr, (1, reps)), jnp.tile(_swap_halves(kr), (1, reps)),
                              gate128], axis=1).astype(BF16)
    bias128 = jnp.concatenate([b_forget, jnp.repeat(b_forget, 3), jnp.zeros((pad,), F32)])[None, :]
    wq = w_q_up.reshape(Q_LORA, HEADS, NOPE_DIM + ROPE_DIM)
    wq_rope = wq[:, :, NOPE_DIM:]
    w_q_p = jnp.concatenate([wq[:, :, :NOPE_DIM].reshape(Q_LORA, -1), wq_rope.reshape(Q_LORA, -1),
                             _swap_halves(wq_rope).reshape(Q_LORA, -1)], axis=1).astype(BF16)
    wkv = w_kv_up.reshape(KV_LORA, HEADS, NOPE_DIM + HEAD_DIM)
    w_kv_p = jnp.concatenate([wkv[:, :, :NOPE_DIM].reshape(KV_LORA, -1), wkv[:, :, NOPE_DIM:].reshape(KV_LORA, -1)],
                             axis=1).astype(BF16)
    w_o1 = w_out[:hw].astype(BF16)
    w_o2 = w_out[hw:].astype(BF16)
    return w_in_p, bias128, w_q_p, w_kv_p, w_o1, w_o2


def _expand_logf(lf):
    pad = LANES - 4 * HEADS
    return jnp.concatenate([lf, jnp.repeat(lf, 3, axis=-1), jnp.zeros(lf.shape[:-1] + (pad,), lf.dtype)], axis=-1)


def kernel(x_prompt, x_sample, cache_mla_ckv, cache_mla_krope, cache_fox_k, cache_fox_v, cache_fox_logf,
           g_ffn1_pre, g_ffn1_post, w_ffn1_gu, w_ffn1_down, g_mix_pre, g_mix_post, w_in, b_forget,
           g_q_latent, w_q_up, g_kv_latent, w_kv_up, w_out, g_ffn2_pre, g_ffn2_post, w_ffn2_gu, w_ffn2_down):
    depth = w_in.shape[0]
    bp, tp, d = x_prompt.shape
    bs, ts, _ = x_sample.shape
    past = cache_mla_ckv.shape[2]
    hw = HEADS * HEAD_DIM
    tq_p = 512
    tc_s = 512
    tk_pad = -(-(past + ts) // tc_s) * tc_s

    cos_p, sin_p = _rope_tables(jnp.arange(tp))
    cos_s, sin_s = _rope_tables(past + jnp.arange(ts))
    cos_s, sin_s = jnp.tile(cos_s, (bs, 1)), jnp.tile(sin_s, (bs, 1))

    xp = x_prompt.reshape(bp * tp, d)
    xs = x_sample.reshape(bs * ts, d)
    tm_p = 512
    tm_s = bs * ts
    rows_p, rows_s = [], []

    def pad_keys(parts):
        n = sum(a.shape[1] for a in parts)
        parts = list(parts) + [jnp.zeros((bs, tk_pad - n, parts[0].shape[2]), parts[0].dtype)]
        return jnp.concatenate(parts, axis=1)

    for l in range(depth):
        d_ff = w_ffn1_down.shape[1]
        w1 = (w_ffn1_gu[l][:, :d_ff].astype(BF16), w_ffn1_gu[l][:, d_ff:].astype(BF16), w_ffn1_down[l].astype(BF16))
        w2 = (w_ffn2_gu[l][:, :d_ff].astype(BF16), w_ffn2_gu[l][:, d_ff:].astype(BF16), w_ffn2_down[l].astype(BF16))
        w_in_p, bias128, w_q_p, w_kv_p, w_o1, w_o2 = _prep_weights(w_in[l], b_forget[l], w_q_up[l], w_kv_up[l], w_out[l])
        g1 = (g_ffn1_pre[l][None, :], g_ffn1_post[l][None, :])
        g2 = (g_ffn2_pre[l][None, :], g_ffn2_post[l][None, :])
        gm_pre, gm_post = g_mix_pre[l][None, :], g_mix_post[l][None, :]
        gq, gkv = g_q_latent[l][None, :], g_kv_latent[l][None, :]

        hp = _ffn(xp, *g1, *w1, tm=tm_p)
        w_vt = (w_kv_p[:, hw:].T, w_in_p[:, _C_FV:_C_KR].T)
        (ckv, krope, fk, fv, logf, qn, qr, kn, vm_t, kr4, fqb, fkb, fv_t, ls) = _proj(
            hp, cos_p, sin_p, gm_pre, w_in_p, bias128, gq, w_q_p, gkv, w_kv_p, w_vt, tm=tq_p)
        sh = lambda a: a.reshape(bp, tp, a.shape[-1])
        sh_t = lambda a: a.reshape(bp, tp // tq_p, hw, tq_p)
        o_mla = _attn_flat(sh(qn), sh(qr), sh(kn), sh(kr4), sh_t(vm_t), tq=tq_p, fox=False)
        o_fox = _attn_flat(sh(fqb), None, sh(fkb), sh(ls), sh_t(fv_t), tq=tq_p, fox=True)
        xp = _ffn(hp, *g2, *w2, mix=(o_mla.reshape(-1, hw), o_fox.reshape(-1, hw), w_o1, w_o2, gm_post), tm=tm_p)
        rows_p.append((ckv.reshape(bp, tp, KV_LORA), krope.reshape(bp, tp, ROPE_DIM),
                       fk.reshape(bp, tp, HEADS, HEAD_DIM), fv.reshape(bp, tp, HEADS, HEAD_DIM),
                       logf.reshape(bp, tp, HEADS)))

        hs = _ffn(xs, *g1, *w1, tm=tm_s)
        (ckv, krope, fk, fv, logf, qn, qr, kn, vm, kr4, fqb, _, _, lf128) = _proj(
            hs, cos_s, sin_s, gm_pre, w_in_p, bias128, gq, w_q_p, gkv, w_kv_p, tm=tm_s)
        kn_past, vm_past, kr4_past = _pastkv(cache_mla_ckv[l].reshape(bs * past, KV_LORA),
                                             cache_mla_krope[l].reshape(bs * past, ROPE_DIM), w_kv_p, tm=512)
        sh = lambda a: a.reshape(bs, -1, a.shape[-1])
        kn_all = pad_keys([sh(kn_past), sh(kn)])
        vm_all = pad_keys([sh(vm_past), sh(vm)])
        kr4_all = pad_keys([sh(kr4_past), sh(kr4)])
        fk_all = pad_keys([cache_fox_k[l].reshape(bs, past, hw), sh(fk)])
        fv_all = pad_keys([cache_fox_v[l].reshape(bs, past, hw), sh(fv)])
        lf_all = pad_keys([_expand_logf(cache_fox_logf[l].astype(F32)), sh(lf128)])
        ls = _lsplit(lf_all, tc=tc_s)
        kw = dict(tq=ts, tk=tk_pad, q_off=past, kv_valid=past + ts)
        o_mla = _attn(sh(qn), sh(qr), kn_all, kr4_all, vm_all, fox=False, **kw)
        o_fox = _attn(sh(fqb), None, fk_all, ls, fv_all, fox=True, **kw)
        xs = _ffn(hs, *g2, *w2, mix=(o_mla.reshape(-1, hw), o_fox.reshape(-1, hw), w_o1, w_o2, gm_post), tm=tm_s)
        rows_s.append((ckv.reshape(bs, ts, KV_LORA), krope.reshape(bs, ts, ROPE_DIM),
                       fk.reshape(bs, ts, HEADS, HEAD_DIM), fv.reshape(bs, ts, HEADS, HEAD_DIM),
                       logf.reshape(bs, ts, HEADS)))

    outs_p = [jnp.stack([r[i] for r in rows_p]) for i in range(5)]
    outs_s = [jnp.stack([r[i] for r in rows_s]) for i in range(5)]
    return (xp.reshape(bp, tp, d), xs.reshape(bs, ts, d), *outs_p, *outs_s)
```

```python
import functools

import jax
import jax.numpy as jnp
from jax import lax
from jax.experimental import pallas as pl
from jax.experimental.pallas import tpu as pltpu

EPS = 1e-6
CHUNK = 64
ROPE_THETA = 10000.0
HEADS = 8
NOPE_DIM = 64
ROPE_DIM = 32
HEAD_DIM = 64
Q_LORA = 384
KV_LORA = 256
LANES = 128
COL_TILE = 256
ONES_ROWS = 16
ACC_ROWS = HEAD_DIM + 8
LOG2E = 1.4426950408889634
NEG = -1e30
LOGF_LANE0 = 8
VMEM_LIMIT = 56 * 1024 * 1024

BF16 = jnp.bfloat16
F32 = jnp.float32


def _rms(x, g):
    ms = jnp.mean(x * x, axis=-1, keepdims=True)
    return x * lax.rsqrt(ms + EPS) * g


def _const_spec(shape):
    return pl.BlockSpec(shape, lambda *_: (0,) * len(shape), pipeline_mode=pl.Buffered(1))


def _ffn_kernel(*refs, ff_chunk, with_mix):
    if with_mix:
        (h_ref, o1_ref, o2_ref, wo1_ref, wo2_ref, gmix_ref,
         gpre_ref, gpost_ref, wg_ref, wu_ref, wd_ref, out_ref) = refs
        def rows(o_ref):
            if len(o_ref.shape) == 2:
                return o_ref[...]
            return jnp.concatenate([o_ref[g] for g in range(o_ref.shape[0])], axis=1)
        mix = jnp.dot(rows(o1_ref), wo1_ref[...], preferred_element_type=F32)
        mix = mix + jnp.dot(rows(o2_ref), wo2_ref[...], preferred_element_type=F32)
        x = h_ref[...] + _rms(mix, gmix_ref[...])
    else:
        x_ref, gpre_ref, gpost_ref, wg_ref, wu_ref, wd_ref, out_ref = refs
        x = x_ref[...]
    n = _rms(x, gpre_ref[...]).astype(BF16)
    d_ff = wg_ref.shape[1]
    acc = jnp.zeros(x.shape, F32)
    for c in range(d_ff // ff_chunk):
        cols = slice(c * ff_chunk, (c + 1) * ff_chunk)
        gate = jnp.dot(n, wg_ref[:, cols], preferred_element_type=F32)
        up = jnp.dot(n, wu_ref[:, cols], preferred_element_type=F32)
        act = (gate * jax.nn.sigmoid(gate) * up).astype(BF16)
        acc = acc + jnp.dot(act, wd_ref[cols, :], preferred_element_type=F32)
    out_ref[...] = x + 0.5 * _rms(acc, gpost_ref[...])


def _ffn(x, g_pre, g_post, w_g, w_u, w_d, mix=None, *, tm):
    n, d = x.shape
    d_ff = w_g.shape[1]
    row = lambda w: pl.BlockSpec((tm, w), lambda i: (i, 0))
    in_specs = [row(d)]
    args = [x]
    if mix is not None:
        o1, o2, wo1, wo2, g_mix = mix
        grouped = lambda o: pl.BlockSpec((o.shape[0], tm, LANES), lambda i: (0, i, 0))
        in_specs += [row(o.shape[1]) if o.ndim == 2 else grouped(o) for o in (o1, o2)]
        in_specs += [_const_spec(wo1.shape), _const_spec(wo2.shape), _const_spec((1, d))]
        args += [o1, o2, wo1, wo2, g_mix]
    in_specs += [_const_spec((1, d)), _const_spec((1, d)), _const_spec(w_g.shape), _const_spec(w_u.shape),
                 _const_spec(w_d.shape)]
    args += [g_pre, g_post, w_g, w_u, w_d]
    return pl.pallas_call(
        functools.partial(_ffn_kernel, ff_chunk=256, with_mix=mix is not None),
        out_shape=jax.ShapeDtypeStruct((n, d), F32),
        grid=(n // tm,),
        in_specs=in_specs,
        out_specs=row(d),
        compiler_params=pltpu.CompilerParams(dimension_semantics=("arbitrary",), vmem_limit_bytes=VMEM_LIMIT),
        name="ffn_mix" if mix is not None else "ffn",
    )(*args)


_C_CQ = 0
_C_CKV = _C_CQ + Q_LORA
_C_FQ = _C_CKV + KV_LORA
_C_FK = _C_FQ + HEADS * HEAD_DIM
_C_FV = _C_FK + HEADS * HEAD_DIM
_C_KR = _C_FV + HEADS * HEAD_DIM
_C_KRS = _C_KR + LANES
_C_GATE = _C_KRS + LANES
_C_END = _C_GATE + LANES


def _log_sigmoid(x):
    return jnp.minimum(x, 0.0) - jnp.log(1.0 + jnp.exp(-jnp.abs(x)))


def _proj_kernel(*refs, values_t, stream_blocks):
    (h_ref, cos_ref, sin_ref, gpre_ref, win_ref, bias_ref, gq_ref, wq_ref, gkv_ref, wkv_ref) = refs[:10]
    n_in = 12 if values_t else 10
    (ckv_ref, krope_ref, fk_ref, fv_ref, logf_ref,
     qn_ref, qr_ref, kn_ref, vm_ref, kr4_ref, fqb_ref, fkb_ref, fvb_ref, lf128_ref) = refs[n_in:n_in + 14]
    nt_dims = (((1,), (1,)), ((), ()))
    hw = HEADS * HEAD_DIM
    u = _rms(h_ref[...], gpre_ref[...]).astype(BF16)
    proj = jnp.dot(u, win_ref[...], preferred_element_type=F32)
    cos = cos_ref[...]
    sin = sin_ref[...]

    cq = _rms(proj[:, _C_CQ:_C_CKV], gq_ref[...]).astype(BF16)
    q = jnp.dot(cq, wq_ref[...], preferred_element_type=F32)
    q_scale = (NOPE_DIM + ROPE_DIM) ** -0.5 * LOG2E
    rw = HEADS * ROPE_DIM
    def put(ref, x):
        if values_t:
            for g in range(x.shape[1] // LANES):
                ref[g] = x[:, g * LANES:(g + 1) * LANES]
        else:
            ref[...] = x

    put(qn_ref, (q[:, :hw] * q_scale).astype(BF16))
    put(qr_ref, ((q[:, hw:hw + rw] * cos + q[:, hw + rw:] * sin) * q_scale).astype(BF16))

    ckv = _rms(proj[:, _C_CKV:_C_FQ], gkv_ref[...])
    ckv_ref[...] = ckv
    ckv_b = ckv.astype(BF16)
    if values_t:
        put(kn_ref, jnp.dot(ckv_b, wkv_ref[:, :hw], preferred_element_type=F32).astype(BF16))
        vm_ref[0] = lax.dot_general(refs[10][...], ckv_b, nt_dims, preferred_element_type=F32).astype(BF16)
    else:
        kv = jnp.dot(ckv_b, wkv_ref[...], preferred_element_type=F32)
        kn_ref[...] = kv[:, :hw].astype(BF16)
        vm_ref[...] = kv[:, hw:].astype(BF16)
    kr4 = proj[:, _C_KR:_C_KRS] * cos[:, :LANES] + proj[:, _C_KRS:_C_GATE] * sin[:, :LANES]
    krope_ref[...] = kr4[:, :ROPE_DIM]
    kr4_ref[...] = kr4.astype(BF16)

    fq = proj[:, _C_FQ:_C_FK]
    fk = proj[:, _C_FK:_C_FV]
    fv = proj[:, _C_FV:_C_KR]
    put(fqb_ref, (fq * (HEAD_DIM ** -0.5 * LOG2E)).astype(BF16))
    if values_t:
        def by_head(x):
            heads = jnp.stack([x[:, h * HEAD_DIM:(h + 1) * HEAD_DIM] for h in range(HEADS)], axis=0)
            return jnp.swapaxes(heads, 0, 1)
        fk_ref[...] = by_head(fk)
        fv_ref[...] = by_head(fv)
    else:
        fk_ref[...] = fk
        fv_ref[...] = fv
    put(fkb_ref, fk.astype(BF16))
    if values_t:
        fvb_ref[0] = lax.dot_general(refs[11][...], u, nt_dims, preferred_element_type=F32).astype(BF16)
    else:
        fvb_ref[...] = fv.astype(BF16)
    logf = _log_sigmoid(proj[:, _C_GATE:_C_END] + bias_ref[...])
    logf_ref[...] = logf[:, :HEADS]
    if values_t:
        carry_ref = refs[-1]

        @pl.when(pl.program_id(0) % stream_blocks == 0)
        def _():
            carry_ref[...] = jnp.zeros_like(carry_ref)

        lf128_ref[...] = _cumsum_split(logf, carry_ref)
    else:
        lf128_ref[...] = logf


def _proj(h, cos, sin, g_pre, w_in, bias, g_q, w_q, g_kv, w_kv, values_t_weights=None, *, tm):
    n, d = h.shape
    t_blocks = cos.shape[0] // tm
    hw = HEADS * HEAD_DIM
    values_t = values_t_weights is not None
    row = lambda w: pl.BlockSpec((tm, w), lambda i: (i, 0))
    tab = pl.BlockSpec((tm, HEADS * ROPE_DIM), lambda i: (i % t_blocks, 0))
    widths = [(KV_LORA, F32), (ROPE_DIM, F32), (hw, F32), (hw, F32), (HEADS, F32),
              (hw, BF16), (HEADS * ROPE_DIM, BF16), (hw, BF16), (hw, BF16), (LANES, BF16),
              (hw, BF16), (hw, BF16), (hw, BF16), (LANES, F32)]
    out_shape = [jax.ShapeDtypeStruct((n, w), dt) for w, dt in widths]
    out_specs = [row(w) for w, _ in widths]
    in_specs = [row(d), tab, tab, _const_spec((1, d)), _const_spec(w_in.shape), _const_spec((1, LANES)),
                _const_spec((1, Q_LORA)), _const_spec(w_q.shape), _const_spec((1, KV_LORA)),
                _const_spec(w_kv.shape)]
    args = [h, cos, sin, g_pre, w_in, bias, g_q, w_q, g_kv, w_kv]
    if values_t:
        for i in (8, 12):
            out_shape[i] = jax.ShapeDtypeStruct((n // tm, hw, tm), BF16)
            out_specs[i] = pl.BlockSpec((1, hw, tm), lambda i: (i, 0, 0))
        out_shape[13] = jax.ShapeDtypeStruct((n, LANES), BF16)
        for i in (5, 6, 7, 10, 11):
            groups = widths[i][0] // LANES
            out_shape[i] = jax.ShapeDtypeStruct((groups, n, LANES), BF16)
            out_specs[i] = pl.BlockSpec((groups, tm, LANES), lambda i: (0, i, 0))
        for i in (2, 3):
            out_shape[i] = jax.ShapeDtypeStruct((n, HEADS, HEAD_DIM), F32)
            out_specs[i] = pl.BlockSpec((tm, HEADS, HEAD_DIM), lambda i: (i, 0, 0))
        in_specs += [_const_spec(w.shape) for w in values_t_weights]
        args += list(values_t_weights)
    return pl.pallas_call(
        functools.partial(_proj_kernel, values_t=values_t, stream_blocks=t_blocks),
        out_shape=out_shape,
        grid=(n // tm,),
        in_specs=in_specs,
        out_specs=out_specs,
        scratch_shapes=[pltpu.VMEM((1, LANES), F32)] if values_t else [],
        compiler_params=pltpu.CompilerParams(dimension_semantics=("arbitrary",), vmem_limit_bytes=VMEM_LIMIT),
        name="proj",
    )(*args)


def _pastkv_kernel(ckv_ref, kr_ref, wkv_ref, kn_ref, vm_ref, kr4_ref):
    hw = HEADS * HEAD_DIM
    kv = jnp.dot(ckv_ref[...].astype(BF16), wkv_ref[...], preferred_element_type=F32)
    kn_ref[...] = kv[:, :hw].astype(BF16)
    vm_ref[...] = kv[:, hw:].astype(BF16)
    src = lax.broadcasted_iota(jnp.int32, (ROPE_DIM, LANES), 0)
    dst = lax.broadcasted_iota(jnp.int32, (ROPE_DIM, LANES), 1)
    rep = (dst % ROPE_DIM == src).astype(BF16)
    kr4_ref[...] = jnp.dot(kr_ref[...].astype(BF16), rep, preferred_element_type=F32).astype(BF16)


def _pastkv(ckv, krope, w_kv, *, tm):
    n = ckv.shape[0]
    hw = HEADS * HEAD_DIM
    row = lambda w: pl.BlockSpec((tm, w), lambda i: (i, 0))
    return pl.pallas_call(
        _pastkv_kernel,
        out_shape=[jax.ShapeDtypeStruct((n, hw), BF16), jax.ShapeDtypeStruct((n, hw), BF16),
                   jax.ShapeDtypeStruct((n, LANES), BF16)],
        grid=(n // tm,),
        in_specs=[row(KV_LORA), row(ROPE_DIM), _const_spec(w_kv.shape)],
        out_specs=[row(hw), row(hw), row(LANES)],
        compiler_params=pltpu.CompilerParams(dimension_semantics=("arbitrary",), vmem_limit_bytes=VMEM_LIMIT),
        name="pastkv",
    )(ckv, krope, w_kv)


def _split3(y, lane):
    hi = y.astype(BF16).astype(F32)
    r1 = y - hi
    mid = r1.astype(BF16).astype(F32)
    lo = r1 - mid
    j = (lane - LOGF_LANE0) % 3
    sel = jnp.where(j == 0, hi, jnp.where(j == 1, mid, lo))
    used = (lane >= LOGF_LANE0) & (lane < LOGF_LANE0 + 3 * HEADS)
    return jnp.where(used, sel, 0.0).astype(BF16)


def _cumsum_split(x, carry_ref):
    tc = x.shape[0]
    hi = x.astype(BF16)
    r1 = x - hi.astype(F32)
    mid = r1.astype(BF16)
    lo = (r1 - mid.astype(F32)).astype(BF16)
    r = lax.broadcasted_iota(jnp.int32, (tc, tc), 0)
    c = lax.broadcasted_iota(jnp.int32, (tc, tc), 1)
    tri = (c <= r).astype(BF16)
    cum = (jnp.dot(tri, hi, preferred_element_type=F32) + jnp.dot(tri, mid, preferred_element_type=F32)
           + jnp.dot(tri, lo, preferred_element_type=F32)) + carry_ref[...]
    carry_ref[...] = cum[tc - 1:tc, :]
    lane = lax.broadcasted_iota(jnp.int32, (tc, LANES), 1)
    return _split3(cum * (-LOG2E), lane)


def _lsplit_kernel(x_ref, o_ref, carry_ref):
    @pl.when(pl.program_id(1) == 0)
    def _():
        carry_ref[...] = jnp.zeros_like(carry_ref)

    o_ref[0] = _cumsum_split(x_ref[0], carry_ref)


def _lsplit(lf128, *, tc):
    b, t, _ = lf128.shape
    spec = pl.BlockSpec((1, tc, LANES), lambda i, j: (i, j, 0))
    return pl.pallas_call(
        _lsplit_kernel,
        out_shape=jax.ShapeDtypeStruct((b, t, LANES), BF16),
        grid=(b, t // tc),
        in_specs=[spec],
        out_specs=spec,
        scratch_shapes=[pltpu.VMEM((1, LANES), F32)],
        compiler_params=pltpu.CompilerParams(dimension_semantics=("arbitrary", "arbitrary")),
        name="lsplit",
    )(lf128)


def _attn_kernel(*refs, tq, tk, q_off, kv_valid, fox):
    if fox:
        q1_ref, k1_ref, k2_ref, v_ref, o_ref, m_ref, l_ref, acc_ref = refs
    else:
        q1_ref, q2_ref, k1_ref, k2_ref, v_ref, o_ref, m_ref, l_ref, acc_ref = refs
    p = pl.program_id(1)
    qi = pl.program_id(2)
    lane = lax.broadcasted_iota(jnp.int32, (tq, LANES), 1)
    q1 = q1_ref[0].astype(F32)
    q2 = jnp.ones_like(q1) if fox else q2_ref[0].astype(F32)

    def head_rows(a):
        main = jnp.where((lane >= a * HEAD_DIM) & (lane < (a + 1) * HEAD_DIM), q1, 0.0)
        if fox:
            lo, width = LOGF_LANE0 + 3 * (2 * p + a), 3
        else:
            lo, width = ROPE_DIM * (2 * (p % 2) + a), ROPE_DIM
        aux = jnp.where((lane >= lo) & (lane < lo + width), q2, 0.0)
        return jnp.concatenate([main, aux], axis=1).astype(BF16)

    q = jnp.concatenate([head_rows(0), head_rows(1)], axis=0)

    m_ref[...] = jnp.full(m_ref.shape, NEG, F32)
    l_ref[...] = jnp.zeros(l_ref.shape, F32)
    acc_ref[...] = jnp.zeros(acc_ref.shape, F32)

    q_start = q_off + qi * tq

    def step(kb, masked):
        ks = pl.multiple_of(kb * tk, tk)
        k = jnp.concatenate([k1_ref[0, pl.ds(ks, tk), :].astype(BF16),
                             k2_ref[0, pl.ds(ks, tk), :].astype(BF16)], axis=1)
        s = lax.dot_general(q, k, (((1,), (1,)), ((), ())), preferred_element_type=F32)
        if masked:
            r = lax.broadcasted_iota(jnp.int32, s.shape, 0)
            t_pos = q_start + jnp.where(r >= tq, r - tq, r)
            s_pos = ks + lax.broadcasted_iota(jnp.int32, s.shape, 1)
            if fox:
                vis = s_pos <= t_pos
            else:
                vis = (s_pos // CHUNK) <= (t_pos // CHUNK)
            s = jnp.where(vis & (s_pos < kv_valid), s, NEG)
        m_prev = m_ref[...]
        m_new = jnp.maximum(m_prev, jnp.max(s, axis=1, keepdims=True))
        alpha = jnp.exp2(m_prev - m_new)
        pexp = jnp.exp2(s - m_new)
        l_ref[...] = alpha * l_ref[...] + jnp.sum(pexp, axis=1, keepdims=True)
        pv = jnp.dot(pexp.astype(BF16), v_ref[0, pl.ds(ks, tk), :].astype(BF16), preferred_element_type=F32)
        acc_ref[...] = alpha * acc_ref[...] + pv
        m_ref[...] = m_new

    if fox:
        hi = q_start + tq
    else:
        hi = ((q_start + tq - 1) // CHUNK + 1) * CHUNK
    hi = jnp.minimum(hi, kv_valid)
    n_blocks = (hi + tk - 1) // tk
    n_full = q_start // tk

    def full_body(kb, carry):
        step(kb, False)
        return carry

    def masked_body(kb, carry):
        step(kb, True)
        return carry

    lax.fori_loop(0, n_full, full_body, 0)
    lax.fori_loop(n_full, n_blocks, masked_body, 0)

    out = acc_ref[...] / l_ref[...]
    o_ref[0] = jnp.where(lane < HEAD_DIM, out[:tq], out[tq:]).astype(o_ref.dtype)


def _attn(q1, q2, k1, k2, v, *, tq, tk, q_off, kv_valid, fox):
    b, t_q, hw = q1.shape
    t_k = k1.shape[1]
    pairs = hw // LANES
    qspec = pl.BlockSpec((1, tq, LANES), lambda bi, p, qi: (bi, qi, p))
    kspec = pl.BlockSpec((1, t_k, LANES), lambda bi, p, qi: (bi, 0, p))
    k2spec = pl.BlockSpec((1, t_k, LANES), lambda bi, p, qi: (bi, 0, 0))
    if fox:
        in_specs = [qspec, kspec, k2spec, kspec]
        args = (q1, k1, k2, v)
    else:
        q2spec = pl.BlockSpec((1, tq, LANES), lambda bi, p, qi: (bi, qi, p // 2))
        in_specs = [qspec, q2spec, kspec, k2spec, kspec]
        args = (q1, q2, k1, k2, v)
    return pl.pallas_call(
        functools.partial(_attn_kernel, tq=tq, tk=tk, q_off=q_off, kv_valid=kv_valid, fox=fox),
        out_shape=jax.ShapeDtypeStruct((b, t_q, hw), BF16),
        grid=(b, pairs, t_q // tq),
        in_specs=in_specs,
        out_specs=qspec,
        scratch_shapes=[pltpu.VMEM((2 * tq, 1), F32), pltpu.VMEM((2 * tq, 1), F32),
                        pltpu.VMEM((2 * tq, LANES), F32)],
        compiler_params=pltpu.CompilerParams(dimension_semantics=("arbitrary",) * 3,
                                             vmem_limit_bytes=VMEM_LIMIT),
        name="attn_fox" if fox else "attn_mla",
    )(*args)


def _attn_t_kernel(*refs, tq, tk, fox):
    n_in = 4 if fox else 5
    q1_ref = refs[0]
    q2_ref = None if fox else refs[1]
    k1_ref, k2_ref, v_ref, o_ref = refs[n_in - 3:n_in + 1]
    (qt_ref, m_ref, acc_ref, s0_ref, s1_ref, p0_ref, p1_ref, a0_ref, a1_ref,
     bm0_ref, bm1_ref) = refs[n_in + 1:]
    s_refs, p_refs, a_refs, bm_refs = (s0_ref, s1_ref), (p0_ref, p1_ref), (a0_ref, a1_ref), (bm0_ref, bm1_ref)
    p = pl.program_id(1)
    qi = pl.program_id(2)
    lane = lax.broadcasted_iota(jnp.int32, (tq, LANES), 1)
    q1 = q1_ref[0].astype(F32)
    q2 = jnp.ones_like(q1) if fox else q2_ref[0].astype(F32)

    def head_rows(a):
        main = jnp.where((lane >= a * HEAD_DIM) & (lane < (a + 1) * HEAD_DIM), q1, 0.0)
        if fox:
            lo, width = LOGF_LANE0 + 3 * (2 * p + a), 3
        else:
            lo, width = ROPE_DIM * (2 * (p % 2) + a), ROPE_DIM
        aux = jnp.where((lane >= lo) & (lane < lo + width), q2, 0.0)
        return jnp.concatenate([main, aux], axis=1)

    qt_ref[...] = jnp.concatenate([head_rows(0), head_rows(1)], axis=0).T.astype(BF16)
    m_ref[...] = jnp.full(m_ref.shape, NEG, F32)
    acc_ref[...] = jnp.zeros(acc_ref.shape, F32)
    p1_ref[...] = jnp.zeros(p1_ref.shape, BF16)
    a1_ref[...] = jnp.zeros(a1_ref.shape, F32)

    q_start = qi * tq

    col_tiles = [slice(j, j + COL_TILE) for j in range(0, 2 * tq, COL_TILE)]

    def key_block(t):
        ks = pl.multiple_of(t * tk, tk)
        return jnp.concatenate([k1_ref[0, pl.ds(ks, tk), :], k2_ref[0, pl.ds(ks, tk), :]], axis=1)

    def value_block_t(kb):
        ks = pl.multiple_of(kb * tk, tk)
        return v_ref[0, pl.ds(ks, tk), :].astype(F32).T.astype(BF16)

    def scores(t, k, par, cols, masked):
        s = jnp.dot(k, qt_ref[:, cols], preferred_element_type=F32)
        if masked:
            c = cols.start + lax.broadcasted_iota(jnp.int32, s.shape, 1)
            t_pos = q_start + jnp.where(c >= tq, c - tq, c)
            s_pos = t * tk + lax.broadcasted_iota(jnp.int32, s.shape, 0)
            vis = (s_pos <= t_pos) if fox else ((s_pos // CHUNK) <= (t_pos // CHUNK))
            s = jnp.where(vis, s, NEG)
        s_refs[par][:, cols] = s
        bm_refs[par][:, cols] = jnp.max(s, axis=0, keepdims=True)

    def softmax(par, cols):
        m_prev = m_ref[:, cols]
        m_new = jnp.maximum(m_prev, bm_refs[par][:, cols])
        a_refs[par][:, cols] = jnp.exp2(m_prev - m_new)
        m_ref[:, cols] = m_new
        p_refs[par][:, cols] = jnp.exp2((s_refs[par][:, cols] - m_new).astype(BF16))

    ones_rows = jnp.ones((ONES_ROWS, tk), BF16)

    def values(v_t, par, cols):
        head = cols.start // tq
        lhs = jnp.concatenate([v_t[head * HEAD_DIM:(head + 1) * HEAD_DIM], ones_rows], axis=0)
        pv = jnp.dot(lhs, p_refs[par][:, cols], preferred_element_type=F32)
        acc_ref[:, cols] = a_refs[par][:, cols] * acc_ref[:, cols] + pv[:ACC_ROWS]

    def tick(t, masked=False, with_scores=True, with_softmax=True):
        par = t % 2 if isinstance(t, int) else t.par
        tv = t if isinstance(t, int) else t.value
        k = key_block(tv) if with_scores else None
        v_t = value_block_t(jnp.maximum(tv - 2, 0))
        for cols in col_tiles:
            if with_softmax:
                softmax(1 - par, cols)
            if with_scores:
                scores(tv, k, par, cols, masked)
            values(v_t, par, cols)

    class Tick:
        def __init__(self, value, par):
            self.value, self.par = value, par

    last = qi

    @pl.when(qi == 0)
    def _():
        k0 = key_block(0)
        for cols in col_tiles:
            scores(0, k0, 0, cols, True)
        tick(1, with_scores=False)
        tick(2, with_scores=False, with_softmax=False)

    @pl.when(qi > 0)
    def _():
        k0 = key_block(0)
        for cols in col_tiles:
            scores(0, k0, 0, cols, False)
        unmasked = last - 1

        def tick_pair(j, carry):
            tick(Tick(2 * j + 1, 1))
            tick(Tick(2 * j + 2, 0))
            return carry

        lax.fori_loop(0, unmasked // 2, tick_pair, 0)

        @pl.when(unmasked % 2 == 1)
        def _():
            tick(Tick(last - 1, 1))
            tick(Tick(last, 0), masked=True)
            tick(Tick(last + 1, 1), with_scores=False)
            tick(Tick(last + 2, 0), with_scores=False, with_softmax=False)

        @pl.when(unmasked % 2 == 0)
        def _():
            tick(Tick(last, 1), masked=True)
            tick(Tick(last + 1, 0), with_scores=False)
            tick(Tick(last + 2, 1), with_scores=False, with_softmax=False)

    acc = acc_ref[...]
    out_t = jnp.concatenate([acc[:HEAD_DIM, :tq] / acc[HEAD_DIM:HEAD_DIM + 1, :tq],
                             acc[:HEAD_DIM, tq:] / acc[HEAD_DIM:HEAD_DIM + 1, tq:]], axis=0)
    o_ref[0] = out_t.T.astype(o_ref.dtype)


def _attn_t(q1, q2, k1, k2, v, *, tq, fox):
    b, t, hw = q1.shape
    tk = tq
    m2 = 2 * tq
    pairs = hw // LANES
    qspec = pl.BlockSpec((1, tq, LANES), lambda bi, p, qi: (bi, qi, p))
    kspec = pl.BlockSpec((1, t, LANES), lambda bi, p, qi: (bi, 0, p))
    k2spec = pl.BlockSpec((1, t, LANES), lambda bi, p, qi: (bi, 0, 0))
    if fox:
        in_specs = [qspec, kspec, k2spec, kspec]
        args = (q1, k1, k2, v)
    else:
        q2spec = pl.BlockSpec((1, tq, LANES), lambda bi, p, qi: (bi, qi, p // 2))
        in_specs = [qspec, q2spec, kspec, k2spec, kspec]
        args = (q1, q2, k1, k2, v)
    return pl.pallas_call(
        functools.partial(_attn_t_kernel, tq=tq, tk=tk, fox=fox),
        out_shape=jax.ShapeDtypeStruct((b, t, hw), BF16),
        grid=(b, pairs, t // tq),
        in_specs=in_specs,
        out_specs=qspec,
        scratch_shapes=[pltpu.VMEM((2 * LANES, m2), BF16),
                        pltpu.VMEM((1, m2), F32),
                        pltpu.VMEM((ACC_ROWS, m2), F32),
                        pltpu.VMEM((tk, m2), F32), pltpu.VMEM((tk, m2), F32),
                        pltpu.VMEM((tk, m2), BF16), pltpu.VMEM((tk, m2), BF16),
                        pltpu.VMEM((1, m2), F32), pltpu.VMEM((1, m2), F32),
                        pltpu.VMEM((1, m2), F32), pltpu.VMEM((1, m2), F32)],
        compiler_params=pltpu.CompilerParams(dimension_semantics=("arbitrary",) * 3,
                                             vmem_limit_bytes=VMEM_LIMIT),
        name="attn_t_fox" if fox else "attn_t_mla",
    )(*args)


class _Item:
    def __init__(self, q, kb, qpar, kpar, first=False, last=False):
        self.q, self.kb, self.qpar, self.kpar, self.first, self.last = q, kb, qpar, kpar, first, last
        self.slot = 2 * qpar + kpar


def _attn_flat_kernel(*refs, tq, nq, fox):
    n_in = 4 if fox else 5
    q1_ref = refs[0]
    q2_ref = None if fox else refs[1]
    k1_ref, k2_ref, vt_ref, o_ref = refs[n_in - 3:n_in + 1]
    scratch = refs[n_in + 1:]
    qt_refs, m_refs, acc_refs = scratch[0:2], scratch[2:4], scratch[4:6]
    s_refs, p_refs, a_refs, bm_refs = scratch[6:10], scratch[10:14], scratch[14:18], scratch[18:22]
    tk = tq
    pair = pl.program_id(1)
    col_tiles = [slice(j, j + COL_TILE) for j in range(0, 2 * tq, COL_TILE)]
    ones_rows = jnp.ones((ONES_ROWS, tk), BF16)

    def rows_of(blk):
        start = blk * tq
        return pl.ds(start if isinstance(start, int) else pl.multiple_of(start, tq), tq)

    def setup(q, qpar):
        lane = lax.broadcasted_iota(jnp.int32, (tq, LANES), 1)
        q1 = q1_ref[rows_of(q), :].astype(F32)
        q2 = jnp.ones_like(q1) if fox else q2_ref[rows_of(q), :].astype(F32)

        def head_rows(a):
            main = jnp.where((lane >= a * HEAD_DIM) & (lane < (a + 1) * HEAD_DIM), q1, 0.0)
            if fox:
                lo, width = LOGF_LANE0 + 3 * (2 * pair + a), 3
            else:
                lo, width = ROPE_DIM * (2 * (pair % 2) + a), ROPE_DIM
            aux = jnp.where((lane >= lo) & (lane < lo + width), q2, 0.0)
            return jnp.concatenate([main, aux], axis=1)

        qt_refs[qpar][...] = jnp.concatenate([head_rows(0), head_rows(1)], axis=0).T.astype(BF16)

    def key_block(kb):
        return jnp.concatenate([k1_ref[rows_of(kb), :], k2_ref[rows_of(kb), :]], axis=1)

    def scores(x, cols):
        s = jnp.dot(key_block(x.kb), qt_refs[x.qpar][:, cols], preferred_element_type=F32)
        if x.last:
            c = (cols.start % tq) + lax.broadcasted_iota(jnp.int32, s.shape, 1)
            r = lax.broadcasted_iota(jnp.int32, s.shape, 0)
            vis = (r <= c) if fox else ((r // CHUNK) <= (c // CHUNK))
            s = jnp.where(vis, s, NEG)
        s_refs[x.slot][cols.start // COL_TILE] = s
        bm_refs[x.slot][:, cols] = jnp.max(s, axis=0, keepdims=True)

    def softmax(x, cols):
        bm = bm_refs[x.slot][:, cols]
        if x.first:
            m_new = bm
            a_refs[x.slot][:, cols] = jnp.zeros_like(bm)
        else:
            m_prev = m_refs[x.qpar][:, cols]
            m_new = jnp.maximum(m_prev, bm)
            a_refs[x.slot][:, cols] = jnp.exp2(m_prev - m_new)
        m_refs[x.qpar][:, cols] = m_new
        ct = cols.start // COL_TILE
        p_refs[x.slot][ct] = jnp.exp2(s_refs[x.slot][ct] - m_new).astype(BF16)

    def values(x, cols):
        head = cols.start // tq
        v_t = vt_ref[x.kb, head * HEAD_DIM:(head + 1) * HEAD_DIM, :]
        lhs = jnp.concatenate([v_t, ones_rows], axis=0)
        pv = jnp.dot(lhs, p_refs[x.slot][cols.start // COL_TILE], preferred_element_type=F32)[:ACC_ROWS]
        acc_ref = acc_refs[x.qpar]
        if x.first:
            acc_ref[:, cols] = pv
        else:
            acc_ref[:, cols] = a_refs[x.slot][:, cols] * acc_ref[:, cols] + pv

    def finalize(x):
        acc = acc_refs[x.qpar][...]
        out_t = jnp.concatenate([acc[:HEAD_DIM, :tq] / acc[HEAD_DIM:HEAD_DIM + 1, :tq],
                                 acc[:HEAD_DIM, tq:] / acc[HEAD_DIM:HEAD_DIM + 1, tq:]], axis=0)
        o_ref[rows_of(x.q), :] = out_t.T.astype(o_ref.dtype)

    def tick(xs, xm, xv, next_q=None):
        for cols in col_tiles:
            if xs is not None:
                scores(xs, cols)
            if xv is not None:
                values(xv, cols)
            if xm is not None:
                softmax(xm, cols)
        if xv is not None and xv.last:
            finalize(xv)
        if next_q is not None:
            setup(*next_q)

    def generic_pairs(q, qpar, count):
        def body(i, carry):
            kb = 3 + 2 * i
            x_a, x_b = _Item(q, kb, qpar, 1), _Item(q, kb + 1, qpar, 0)
            tick(x_a, _Item(q, kb - 1, qpar, 0), _Item(q, kb - 2, qpar, 1))
            tick(x_b, x_a, _Item(q, kb - 1, qpar, 0))
            return carry
        lax.fori_loop(0, count, body, 0)

    def query_block(q, qpar, is_last_q=False):
        p2 = _Item(q - 1, q - 2, 1 - qpar, qpar)
        p1 = _Item(q - 1, q - 1, 1 - qpar, 1 - qpar, last=True)
        x0 = _Item(q, 0, qpar, 0, first=True)
        x1 = _Item(q, 1, qpar, 1)
        x2 = _Item(q, 2, qpar, 0)
        tick(x0, p1, p2)
        tick(x1, x0, p1)
        tick(x2, x1, x0)
        nxt = None if is_last_q else (q + 1, 1 - qpar)
        if qpar == 1:
            generic_pairs(q, qpar, (q - 3) // 2)
            a2, a1 = _Item(q, q - 2, qpar, 1), _Item(q, q - 1, qpar, 0)
        else:
            generic_pairs(q, qpar, (q - 4) // 2)
            a1 = _Item(q, q - 1, qpar, 1)
            a2 = _Item(q, q - 2, qpar, 0)
            tick(a1, a2, _Item(q, q - 3, qpar, 1))
        xl = _Item(q, q, qpar, qpar, last=True)
        tick(xl, a1, a2, next_q=nxt)
        return a1, xl

    setup(0, 0)
    x00 = _Item(0, 0, 0, 0, first=True, last=True)
    x10, x11 = _Item(1, 0, 1, 0, first=True), _Item(1, 1, 1, 1, last=True)
    x20, x21, x22 = _Item(2, 0, 0, 0, first=True), _Item(2, 1, 0, 1), _Item(2, 2, 0, 0, last=True)
    tick(x00, None, None, next_q=(1, 1))
    tick(x10, x00, None)
    tick(x11, x10, x00, next_q=(2, 0))
    tick(x20, x11, x10)
    tick(x21, x20, x11)
    tick(x22, x21, x20, next_q=(3, 1))

    def block_pair(j, carry):
        query_block(3 + 2 * j, 1)
        query_block(4 + 2 * j, 0)
        return carry

    lax.fori_loop(0, (nq - 4) // 2, block_pair, 0)
    a1, xl = query_block(nq - 1, 1, is_last_q=True)
    tick(None, xl, a1)
    tick(None, None, xl)


def _attn_flat(q1, q2, k1, k2, v_t, *, tq, fox):
    pairs, b, t, _ = q1.shape
    hw = pairs * LANES
    nq = t // tq
    assert t % tq == 0 and nq % 2 == 0 and nq >= 4, (t, tq)
    assert v_t.shape == (b, nq, hw, tq), v_t.shape
    m2 = 2 * tq
    spec = pl.BlockSpec((None, None, t, LANES), lambda bi, p: (p, bi, 0, 0))
    k2spec = pl.BlockSpec((None, t, LANES), lambda bi, p: (bi, 0, 0))
    vspec = pl.BlockSpec((None, nq, LANES, tq), lambda bi, p: (bi, 0, p, 0))
    if fox:
        in_specs = [spec, spec, k2spec, vspec]
        args = (q1, k1, k2, v_t)
    else:
        q2spec = pl.BlockSpec((None, None, t, LANES), lambda bi, p: (p // 2, bi, 0, 0))
        in_specs = [spec, q2spec, spec, k2spec, vspec]
        args = (q1, q2, k1, k2, v_t)
    vmem = lambda shape, dt, n: [pltpu.VMEM(shape, dt) for _ in range(n)]
    return pl.pallas_call(
        functools.partial(_attn_flat_kernel, tq=tq, nq=nq, fox=fox),
        out_shape=jax.ShapeDtypeStruct((pairs, b, t, LANES), BF16),
        grid=(b, pairs),
        in_specs=in_specs,
        out_specs=spec,
        scratch_shapes=(vmem((2 * LANES, m2), BF16, 2)
                        + vmem((1, m2), F32, 2)
                        + vmem((ACC_ROWS, m2), F32, 2)
                        + vmem((m2 // COL_TILE, tq, COL_TILE), F32, 4)
                        + vmem((m2 // COL_TILE, tq, COL_TILE), BF16, 4)
                        + vmem((1, m2), F32, 4)
                        + vmem((1, m2), F32, 4)),
        compiler_params=pltpu.CompilerParams(dimension_semantics=("arbitrary",) * 2,
                                             vmem_limit_bytes=VMEM_LIMIT),
        name="attn_flat_fox" if fox else "attn_flat_mla",
    )(*args)


def _rope_tables(pos):
    half = ROPE_DIM // 2
    freqs = ROPE_THETA ** (-jnp.arange(half, dtype=F32) / half)
    ang = pos.astype(F32)[:, None] * freqs[None, :]
    cos, sin = jnp.cos(ang), jnp.sin(ang)
    cos_t = jnp.tile(jnp.concatenate([cos, cos], axis=1), (1, HEADS))
    sin_t = jnp.tile(jnp.concatenate([-sin, sin], axis=1), (1, HEADS))
    return cos_t, sin_t


def _swap_halves(w):
    half = w.shape[-1] // 2
    return jnp.concatenate([w[..., half:], w[..., :half]], axis=-1)


def _prep_weights(w_in, b_forget, w_q_up, w_kv_up, w_out):
    d = w_in.shape[0]
    hw = HEADS * HEAD_DIM
    o = 0
    cq = w_in[:, o:o + Q_LORA]; o += Q_LORA
    ckv = w_in[:, o:o + KV_LORA]; o += KV_LORA
    kr = w_in[:, o:o + ROPE_DIM]; o += ROPE_DIM
    fq = w_in[:, o:o + hw]; o += hw
    fk = w_in[:, o:o + hw]; o += hw
    fv = w_in[:, o:o + hw]; o += hw
    gate = w_in[:, o:o + HEADS]
    reps = LANES // ROPE_DIM
    pad = LANES - 4 * HEADS
    gate128 = jnp.concatenate([gate, jnp.repeat(gate, 3, axis=1), jnp.zeros((d, pad), w_in.dtype)], axis=1)
    w_in_p = jnp.concatenate([cq, ckv, fq, fk, fv, jnp.tile(kr, (1, reps)), jnp.tile(_swap_halves(kr), (1, reps)),
                              gate128], axis=1).astype(BF16)
    bias128 = jnp.concatenate([b_forget, jnp.repeat(b_forget, 3), jnp.zeros((pad,), F32)])[None, :]
    wq = w_q_up.reshape(Q_LORA, HEADS, NOPE_DIM + ROPE_DIM)
    wq_rope = wq[:, :, NOPE_DIM:]
    w_q_p = jnp.concatenate([wq[:, :, :NOPE_DIM].reshape(Q_LORA, -1), wq_rope.reshape(Q_LORA, -1),
                             _swap_halves(wq_rope).reshape(Q_LORA, -1)], axis=1).astype(BF16)
    wkv = w_kv_up.reshape(KV_LORA, HEADS, NOPE_DIM + HEAD_DIM)
    w_kv_p = jnp.concatenate([wkv[:, :, :NOPE_DIM].reshape(KV_LORA, -1), wkv[:, :, NOPE_DIM:].reshape(KV_LORA, -1)],
                             axis=1).astype(BF16)
    w_o1 = w_out[:hw].astype(BF16)
    w_o2 = w_out[hw:].astype(BF16)
    return w_in_p, bias128, w_q_p, w_kv_p, w_o1, w_o2


def _expand_logf(lf):
    pad = LANES - 4 * HEADS
    return jnp.concatenate([lf, jnp.repeat(lf, 3, axis=-1), jnp.zeros(lf.shape[:-1] + (pad,), lf.dtype)], axis=-1)


def kernel(x_prompt, x_sample, cache_mla_ckv, cache_mla_krope, cache_fox_k, cache_fox_v, cache_fox_logf,
           g_ffn1_pre, g_ffn1_post, w_ffn1_gu, w_ffn1_down, g_mix_pre, g_mix_post, w_in, b_forget,
           g_q_latent, w_q_up, g_kv_latent, w_kv_up, w_out, g_ffn2_pre, g_ffn2_post, w_ffn2_gu, w_ffn2_down):
    depth = w_in.shape[0]
    bp, tp, d = x_prompt.shape
    bs, ts, _ = x_sample.shape
    past = cache_mla_ckv.shape[2]
    hw = HEADS * HEAD_DIM
    tq_p = 512
    tc_s = 512
    tk_pad = -(-(past + ts) // tc_s) * tc_s

    cos_p, sin_p = _rope_tables(jnp.arange(tp))
    cos_s, sin_s = _rope_tables(past + jnp.arange(ts))
    cos_s, sin_s = jnp.tile(cos_s, (bs, 1)), jnp.tile(sin_s, (bs, 1))

    xp = x_prompt.reshape(bp * tp, d)
    xs = x_sample.reshape(bs * ts, d)
    tm_p = 512
    tm_s = bs * ts
    rows_p, rows_s = [], []

    def pad_keys(parts):
        n = sum(a.shape[1] for a in parts)
        parts = list(parts) + [jnp.zeros((bs, tk_pad - n, parts[0].shape[2]), parts[0].dtype)]
        return jnp.concatenate(parts, axis=1)

    for l in range(depth):
        d_ff = w_ffn1_down.shape[1]
        w1 = (w_ffn1_gu[l][:, :d_ff].astype(BF16), w_ffn1_gu[l][:, d_ff:].astype(BF16), w_ffn1_down[l].astype(BF16))
        w2 = (w_ffn2_gu[l][:, :d_ff].astype(BF16), w_ffn2_gu[l][:, d_ff:].astype(BF16), w_ffn2_down[l].astype(BF16))
        w_in_p, bias128, w_q_p, w_kv_p, w_o1, w_o2 = _prep_weights(w_in[l], b_forget[l], w_q_up[l], w_kv_up[l], w_out[l])
        g1 = (g_ffn1_pre[l][None, :], g_ffn1_post[l][None, :])
        g2 = (g_ffn2_pre[l][None, :], g_ffn2_post[l][None, :])
        gm_pre, gm_post = g_mix_pre[l][None, :], g_mix_post[l][None, :]
        gq, gkv = g_q_latent[l][None, :], g_kv_latent[l][None, :]

        hp = _ffn(xp, *g1, *w1, tm=tm_p)
        w_vt = (w_kv_p[:, hw:].T, w_in_p[:, _C_FV:_C_KR].T)
        (ckv, krope, fk, fv, logf, qn, qr, kn, vm_t, kr4, fqb, fkb, fv_t, ls) = _proj(
            hp, cos_p, sin_p, gm_pre, w_in_p, bias128, gq, w_q_p, gkv, w_kv_p, w_vt, tm=tq_p)
        sh = lambda a: a.reshape(bp, tp, a.shape[-1])
        sh_g = lambda a: a.reshape(a.shape[0], bp, tp, LANES)
        sh_t = lambda a: a.reshape(bp, tp // tq_p, hw, tq_p)
        o_mla = _attn_flat(sh_g(qn), sh_g(qr), sh_g(kn), sh(kr4), sh_t(vm_t), tq=tq_p, fox=False)
        o_fox = _attn_flat(sh_g(fqb), None, sh_g(fkb), sh(ls), sh_t(fv_t), tq=tq_p, fox=True)
        un_g = lambda o: o.reshape(o.shape[0], bp * tp, LANES)
        xp = _ffn(hp, *g2, *w2, mix=(un_g(o_mla), un_g(o_fox), w_o1, w_o2, gm_post), tm=tm_p)
        rows_p.append((ckv.reshape(bp, tp, KV_LORA), krope.reshape(bp, tp, ROPE_DIM),
                       fk.reshape(bp, tp, HEADS, HEAD_DIM), fv.reshape(bp, tp, HEADS, HEAD_DIM),
                       logf.reshape(bp, tp, HEADS)))

        hs = _ffn(xs, *g1, *w1, tm=tm_s)
        (ckv, krope, fk, fv, logf, qn, qr, kn, vm, kr4, fqb, _, _, lf128) = _proj(
            hs, cos_s, sin_s, gm_pre, w_in_p, bias128, gq, w_q_p, gkv, w_kv_p, tm=tm_s)
        kn_past, vm_past, kr4_past = _pastkv(cache_mla_ckv[l].reshape(bs * past, KV_LORA),
                                             cache_mla_krope[l].reshape(bs * past, ROPE_DIM), w_kv_p, tm=512)
        sh = lambda a: a.reshape(bs, -1, a.shape[-1])
        kn_all = pad_keys([sh(kn_past), sh(kn)])
        vm_all = pad_keys([sh(vm_past), sh(vm)])
        kr4_all = pad_keys([sh(kr4_past), sh(kr4)])
        fk_all = pad_keys([cache_fox_k[l].reshape(bs, past, hw), sh(fk)])
        fv_all = pad_keys([cache_fox_v[l].reshape(bs, past, hw), sh(fv)])
        lf_all = pad_keys([_expand_logf(cache_fox_logf[l].astype(F32)), sh(lf128)])
        ls = _lsplit(lf_all, tc=tc_s)
        kw = dict(tq=ts, tk=tk_pad, q_off=past, kv_valid=past + ts)
        o_mla = _attn(sh(qn), sh(qr), kn_all, kr4_all, vm_all, fox=False, **kw)
        o_fox = _attn(sh(fqb), None, fk_all, ls, fv_all, fox=True, **kw)
        xs = _ffn(hs, *g2, *w2, mix=(o_mla.reshape(-1, hw), o_fox.reshape(-1, hw), w_o1, w_o2, gm_post), tm=tm_s)
        rows_s.append((ckv.reshape(bs, ts, KV_LORA), krope.reshape(bs, ts, ROPE_DIM),
                       fk.reshape(bs, ts, HEADS, HEAD_DIM), fv.reshape(bs, ts, HEADS, HEAD_DIM),
                       logf.reshape(bs, ts, HEADS)))

    outs_p = [jnp.stack([r[i] for r in rows_p]) for i in range(5)]
    outs_s = [jnp.stack([r[i] for r in rows_s]) for i in range(5)]
    return (xp.reshape(bp, tp, d), xs.reshape(bs, ts, d), *outs_p, *outs_s)
```

```python
import functools

import jax
import jax.numpy as jnp
from jax import lax
from jax.experimental import pallas as pl
from jax.experimental.pallas import tpu as pltpu

EPS = 1e-6
CHUNK = 64
ROPE_THETA = 10000.0
HEADS = 8
NOPE_DIM = 64
ROPE_DIM = 32
HEAD_DIM = 64
Q_LORA = 384
KV_LORA = 256
LANES = 128
COL_TILE = 256
ONES_ROWS = 16
ACC_ROWS = HEAD_DIM + 8
LOG2E = 1.4426950408889634
NEG = -1e30
LOGF_LANE0 = 8
VMEM_LIMIT = 56 * 1024 * 1024

BF16 = jnp.bfloat16
F32 = jnp.float32


def _rms(x, g):
    ms = jnp.mean(x * x, axis=-1, keepdims=True)
    return x * lax.rsqrt(ms + EPS) * g


def _const_spec(shape):
    return pl.BlockSpec(shape, lambda *_: (0,) * len(shape), pipeline_mode=pl.Buffered(1))


def _ffn_kernel(*refs, ff_chunk, with_mix):
    if with_mix:
        (h_ref, o1_ref, o2_ref, wo1_ref, wo2_ref, gmix_ref,
         gpre_ref, gpost_ref, wg_ref, wu_ref, wd_ref, out_ref) = refs
        def rows(o_ref):
            if len(o_ref.shape) == 2:
                return o_ref[...]
            return jnp.concatenate([o_ref[g] for g in range(o_ref.shape[0])], axis=1)
        mix = jnp.dot(rows(o1_ref), wo1_ref[...], preferred_element_type=F32)
        mix = mix + jnp.dot(rows(o2_ref), wo2_ref[...], preferred_element_type=F32)
        x = h_ref[...] + _rms(mix, gmix_ref[...])
    else:
        x_ref, gpre_ref, gpost_ref, wg_ref, wu_ref, wd_ref, out_ref = refs
        x = x_ref[...]
    n = _rms(x, gpre_ref[...]).astype(BF16)
    d_ff = wg_ref.shape[1]
    acc = jnp.zeros(x.shape, F32)
    for c in range(d_ff // ff_chunk):
        cols = slice(c * ff_chunk, (c + 1) * ff_chunk)
        gate = jnp.dot(n, wg_ref[:, cols], preferred_element_type=F32)
        up = jnp.dot(n, wu_ref[:, cols], preferred_element_type=F32)
        act = (gate * jax.nn.sigmoid(gate) * up).astype(BF16)
        acc = acc + jnp.dot(act, wd_ref[cols, :], preferred_element_type=F32)
    out_ref[...] = x + 0.5 * _rms(acc, gpost_ref[...])


def _ffn(x, g_pre, g_post, w_g, w_u, w_d, mix=None, *, tm):
    n, d = x.shape
    d_ff = w_g.shape[1]
    row = lambda w: pl.BlockSpec((tm, w), lambda i: (i, 0))
    in_specs = [row(d)]
    args = [x]
    if mix is not None:
        o1, o2, wo1, wo2, g_mix = mix
        grouped = lambda o: pl.BlockSpec((o.shape[0], tm, LANES), lambda i: (0, i, 0))
        in_specs += [row(o.shape[1]) if o.ndim == 2 else grouped(o) for o in (o1, o2)]
        in_specs += [_const_spec(wo1.shape), _const_spec(wo2.shape), _const_spec((1, d))]
        args += [o1, o2, wo1, wo2, g_mix]
    in_specs += [_const_spec((1, d)), _const_spec((1, d)), _const_spec(w_g.shape), _const_spec(w_u.shape),
                 _const_spec(w_d.shape)]
    args += [g_pre, g_post, w_g, w_u, w_d]
    return pl.pallas_call(
        functools.partial(_ffn_kernel, ff_chunk=256, with_mix=mix is not None),
        out_shape=jax.ShapeDtypeStruct((n, d), F32),
        grid=(n // tm,),
        in_specs=in_specs,
        out_specs=row(d),
        compiler_params=pltpu.CompilerParams(dimension_semantics=("arbitrary",), vmem_limit_bytes=VMEM_LIMIT),
        name="ffn_mix" if mix is not None else "ffn",
    )(*args)


_C_CQ = 0
_C_CKV = _C_CQ + Q_LORA
_C_FQ = _C_CKV + KV_LORA
_C_FK = _C_FQ + HEADS * HEAD_DIM
_C_FV = _C_FK + HEADS * HEAD_DIM
_C_KR = _C_FV + HEADS * HEAD_DIM
_C_KRS = _C_KR + LANES
_C_GATE = _C_KRS + LANES
_C_END = _C_GATE + LANES


def _log_sigmoid(x):
    return jnp.minimum(x, 0.0) - jnp.log(1.0 + jnp.exp(-jnp.abs(x)))


def _proj_kernel(*refs, values_t, stream_blocks):
    (h_ref, cos_ref, sin_ref, gpre_ref, win_ref, bias_ref, gq_ref, wq_ref, gkv_ref, wkv_ref) = refs[:10]
    n_in = 12 if values_t else 10
    (ckv_ref, krope_ref, fk_ref, fv_ref, logf_ref,
     qn_ref, qr_ref, kn_ref, vm_ref, kr4_ref, fqb_ref, fkb_ref, fvb_ref, lf128_ref) = refs[n_in:n_in + 14]
    nt_dims = (((1,), (1,)), ((), ()))
    hw = HEADS * HEAD_DIM
    u = _rms(h_ref[...], gpre_ref[...]).astype(BF16)
    proj = jnp.dot(u, win_ref[...], preferred_element_type=F32)
    cos = cos_ref[...]
    sin = sin_ref[...]

    cq = _rms(proj[:, _C_CQ:_C_CKV], gq_ref[...]).astype(BF16)
    q = jnp.dot(cq, wq_ref[...], preferred_element_type=F32)
    q_scale = (NOPE_DIM + ROPE_DIM) ** -0.5 * LOG2E
    rw = HEADS * ROPE_DIM
    def put(ref, x):
        if values_t:
            for g in range(x.shape[1] // LANES):
                ref[g] = x[:, g * LANES:(g + 1) * LANES]
        else:
            ref[...] = x

    put(qn_ref, (q[:, :hw] * q_scale).astype(BF16))
    put(qr_ref, ((q[:, hw:hw + rw] * cos + q[:, hw + rw:] * sin) * q_scale).astype(BF16))

    ckv = _rms(proj[:, _C_CKV:_C_FQ], gkv_ref[...])
    ckv_ref[...] = ckv
    ckv_b = ckv.astype(BF16)
    if values_t:
        put(kn_ref, jnp.dot(ckv_b, wkv_ref[:, :hw], preferred_element_type=F32).astype(BF16))
        vm_ref[0] = lax.dot_general(refs[10][...], ckv_b, nt_dims, preferred_element_type=F32).astype(BF16)
    else:
        kv = jnp.dot(ckv_b, wkv_ref[...], preferred_element_type=F32)
        kn_ref[...] = kv[:, :hw].astype(BF16)
        vm_ref[...] = kv[:, hw:].astype(BF16)
    kr4 = proj[:, _C_KR:_C_KRS] * cos[:, :LANES] + proj[:, _C_KRS:_C_GATE] * sin[:, :LANES]
    krope_ref[...] = kr4[:, :ROPE_DIM]
    kr4_ref[...] = kr4.astype(BF16)

    fq = proj[:, _C_FQ:_C_FK]
    fk = proj[:, _C_FK:_C_FV]
    fv = proj[:, _C_FV:_C_KR]
    put(fqb_ref, (fq * (HEAD_DIM ** -0.5 * LOG2E)).astype(BF16))
    if values_t:
        def by_head(x):
            heads = jnp.stack([x[:, h * HEAD_DIM:(h + 1) * HEAD_DIM] for h in range(HEADS)], axis=0)
            return jnp.swapaxes(heads, 0, 1)
        fk_ref[...] = by_head(fk)
        fv_ref[...] = by_head(fv)
    else:
        fk_ref[...] = fk
        fv_ref[...] = fv
    put(fkb_ref, fk.astype(BF16))
    if values_t:
        fvb_ref[0] = lax.dot_general(refs[11][...], u, nt_dims, preferred_element_type=F32).astype(BF16)
    else:
        fvb_ref[...] = fv.astype(BF16)
    logf = _log_sigmoid(proj[:, _C_GATE:_C_END] + bias_ref[...])
    logf_ref[...] = logf[:, :HEADS]
    if values_t:
        carry_ref = refs[-1]

        @pl.when(pl.program_id(0) % stream_blocks == 0)
        def _():
            carry_ref[...] = jnp.zeros_like(carry_ref)

        lf128_ref[...] = _cumsum_split(logf, carry_ref)
    else:
        lf128_ref[...] = logf


def _proj(h, cos, sin, g_pre, w_in, bias, g_q, w_q, g_kv, w_kv, values_t_weights=None, *, tm):
    n, d = h.shape
    t_blocks = cos.shape[0] // tm
    hw = HEADS * HEAD_DIM
    values_t = values_t_weights is not None
    row = lambda w: pl.BlockSpec((tm, w), lambda i: (i, 0))
    tab = pl.BlockSpec((tm, HEADS * ROPE_DIM), lambda i: (i % t_blocks, 0))
    widths = [(KV_LORA, F32), (ROPE_DIM, F32), (hw, F32), (hw, F32), (HEADS, F32),
              (hw, BF16), (HEADS * ROPE_DIM, BF16), (hw, BF16), (hw, BF16), (LANES, BF16),
              (hw, BF16), (hw, BF16), (hw, BF16), (LANES, F32)]
    out_shape = [jax.ShapeDtypeStruct((n, w), dt) for w, dt in widths]
    out_specs = [row(w) for w, _ in widths]
    in_specs = [row(d), tab, tab, _const_spec((1, d)), _const_spec(w_in.shape), _const_spec((1, LANES)),
                _const_spec((1, Q_LORA)), _const_spec(w_q.shape), _const_spec((1, KV_LORA)),
                _const_spec(w_kv.shape)]
    args = [h, cos, sin, g_pre, w_in, bias, g_q, w_q, g_kv, w_kv]
    if values_t:
        for i in (8, 12):
            out_shape[i] = jax.ShapeDtypeStruct((n // tm, hw, tm), BF16)
            out_specs[i] = pl.BlockSpec((1, hw, tm), lambda i: (i, 0, 0))
        out_shape[13] = jax.ShapeDtypeStruct((n, LANES), BF16)
        for i in (5, 6, 7, 10, 11):
            groups = widths[i][0] // LANES
            out_shape[i] = jax.ShapeDtypeStruct((groups, n, LANES), BF16)
            out_specs[i] = pl.BlockSpec((groups, tm, LANES), lambda i: (0, i, 0))
        for i in (2, 3):
            out_shape[i] = jax.ShapeDtypeStruct((n, HEADS, HEAD_DIM), F32)
            out_specs[i] = pl.BlockSpec((tm, HEADS, HEAD_DIM), lambda i: (i, 0, 0))
        in_specs += [_const_spec(w.shape) for w in values_t_weights]
        args += list(values_t_weights)
    return pl.pallas_call(
        functools.partial(_proj_kernel, values_t=values_t, stream_blocks=t_blocks),
        out_shape=out_shape,
        grid=(n // tm,),
        in_specs=in_specs,
        out_specs=out_specs,
        scratch_shapes=[pltpu.VMEM((1, LANES), F32)] if values_t else [],
        compiler_params=pltpu.CompilerParams(dimension_semantics=("arbitrary",), vmem_limit_bytes=VMEM_LIMIT),
        name="proj",
    )(*args)


def _pastkv_kernel(ckv_ref, kr_ref, wkv_ref, kn_ref, vm_ref, kr4_ref):
    hw = HEADS * HEAD_DIM
    kv = jnp.dot(ckv_ref[...].astype(BF16), wkv_ref[...], preferred_element_type=F32)
    kn_ref[...] = kv[:, :hw].astype(BF16)
    vm_ref[...] = kv[:, hw:].astype(BF16)
    src = lax.broadcasted_iota(jnp.int32, (ROPE_DIM, LANES), 0)
    dst = lax.broadcasted_iota(jnp.int32, (ROPE_DIM, LANES), 1)
    rep = (dst % ROPE_DIM == src).astype(BF16)
    kr4_ref[...] = jnp.dot(kr_ref[...].astype(BF16), rep, preferred_element_type=F32).astype(BF16)


def _pastkv(ckv, krope, w_kv, *, tm):
    n = ckv.shape[0]
    hw = HEADS * HEAD_DIM
    row = lambda w: pl.BlockSpec((tm, w), lambda i: (i, 0))
    return pl.pallas_call(
        _pastkv_kernel,
        out_shape=[jax.ShapeDtypeStruct((n, hw), BF16), jax.ShapeDtypeStruct((n, hw), BF16),
                   jax.ShapeDtypeStruct((n, LANES), BF16)],
        grid=(n // tm,),
        in_specs=[row(KV_LORA), row(ROPE_DIM), _const_spec(w_kv.shape)],
        out_specs=[row(hw), row(hw), row(LANES)],
        compiler_params=pltpu.CompilerParams(dimension_semantics=("arbitrary",), vmem_limit_bytes=VMEM_LIMIT),
        name="pastkv",
    )(ckv, krope, w_kv)


def _split3(y, lane):
    hi = y.astype(BF16).astype(F32)
    r1 = y - hi
    mid = r1.astype(BF16).astype(F32)
    lo = r1 - mid
    j = (lane - LOGF_LANE0) % 3
    sel = jnp.where(j == 0, hi, jnp.where(j == 1, mid, lo))
    used = (lane >= LOGF_LANE0) & (lane < LOGF_LANE0 + 3 * HEADS)
    return jnp.where(used, sel, 0.0).astype(BF16)


def _cumsum_split(x, carry_ref):
    tc = x.shape[0]
    hi = x.astype(BF16)
    r1 = x - hi.astype(F32)
    mid = r1.astype(BF16)
    lo = (r1 - mid.astype(F32)).astype(BF16)
    r = lax.broadcasted_iota(jnp.int32, (tc, tc), 0)
    c = lax.broadcasted_iota(jnp.int32, (tc, tc), 1)
    tri = (c <= r).astype(BF16)
    cum = (jnp.dot(tri, hi, preferred_element_type=F32) + jnp.dot(tri, mid, preferred_element_type=F32)
           + jnp.dot(tri, lo, preferred_element_type=F32)) + carry_ref[...]
    carry_ref[...] = cum[tc - 1:tc, :]
    lane = lax.broadcasted_iota(jnp.int32, (tc, LANES), 1)
    return _split3(cum * (-LOG2E), lane)


def _lsplit_kernel(x_ref, o_ref, carry_ref):
    @pl.when(pl.program_id(1) == 0)
    def _():
        carry_ref[...] = jnp.zeros_like(carry_ref)

    o_ref[0] = _cumsum_split(x_ref[0], carry_ref)


def _lsplit(lf128, *, tc):
    b, t, _ = lf128.shape
    spec = pl.BlockSpec((1, tc, LANES), lambda i, j: (i, j, 0))
    return pl.pallas_call(
        _lsplit_kernel,
        out_shape=jax.ShapeDtypeStruct((b, t, LANES), BF16),
        grid=(b, t // tc),
        in_specs=[spec],
        out_specs=spec,
        scratch_shapes=[pltpu.VMEM((1, LANES), F32)],
        compiler_params=pltpu.CompilerParams(dimension_semantics=("arbitrary", "arbitrary")),
        name="lsplit",
    )(lf128)


def _attn_sample_kernel(*refs, past, fox):
    n_q = 1 if fox else 2
    q1_ref = refs[0]
    q2_ref = None if fox else refs[1]
    k1p_ref, k2p_ref, vp_ref, k1n_ref, k2n_ref, vn_ref, o_ref = refs[n_q:]
    ts = q1_ref.shape[1]
    p = pl.program_id(1)
    lane = lax.broadcasted_iota(jnp.int32, (ts, LANES), 1)
    q1 = q1_ref[0].astype(F32)
    q2 = jnp.ones_like(q1) if fox else q2_ref[0].astype(F32)

    def head_rows(a):
        main = jnp.where((lane >= a * HEAD_DIM) & (lane < (a + 1) * HEAD_DIM), q1, 0.0)
        if fox:
            lo, width = LOGF_LANE0 + 3 * (2 * p + a), 3
        else:
            lo, width = ROPE_DIM * (2 * (p % 2) + a), ROPE_DIM
        aux = jnp.where((lane >= lo) & (lane < lo + width), q2, 0.0)
        return jnp.concatenate([main, aux], axis=1).astype(BF16)

    q = jnp.concatenate([head_rows(0), head_rows(1)], axis=0)
    nt_dims = (((1,), (1,)), ((), ()))

    def logits(k1_ref, k2_ref):
        k = jnp.concatenate([k1_ref[0].astype(BF16), k2_ref[0].astype(BF16)], axis=1)
        return lax.dot_general(q, k, nt_dims, preferred_element_type=F32)

    s_p = logits(k1p_ref, k2p_ref)
    s_n = logits(k1n_ref, k2n_ref)
    r = lax.broadcasted_iota(jnp.int32, s_n.shape, 0)
    t_pos = past + jnp.where(r >= ts, r - ts, r)
    s_pos = past + lax.broadcasted_iota(jnp.int32, s_n.shape, 1)
    vis = (s_pos <= t_pos) if fox else ((s_pos // CHUNK) <= (t_pos // CHUNK))
    s_n = jnp.where(vis, s_n, NEG)
    m = jnp.maximum(jnp.max(s_p, axis=1, keepdims=True), jnp.max(s_n, axis=1, keepdims=True))
    p_p = jnp.exp2(s_p - m)
    p_n = jnp.exp2(s_n - m)
    denom = jnp.sum(p_p, axis=1, keepdims=True) + jnp.sum(p_n, axis=1, keepdims=True)
    out = (jnp.dot(p_p.astype(BF16), vp_ref[0].astype(BF16), preferred_element_type=F32)
           + jnp.dot(p_n.astype(BF16), vn_ref[0].astype(BF16), preferred_element_type=F32)) / denom
    o_ref[0] = jnp.where(lane < HEAD_DIM, out[:ts], out[ts:]).astype(o_ref.dtype)


def _attn_sample(q1, q2, k1p, k2p, vp, k1n, k2n, vn, *, k2n_row0=0, fox):
    b, ts, hw = q1.shape
    past = k1p.shape[1]
    assert k2n_row0 % ts == 0
    pairs = hw // LANES
    qspec = pl.BlockSpec((1, ts, LANES), lambda bi, p: (bi, 0, p))
    pspec = pl.BlockSpec((1, past, LANES), lambda bi, p: (bi, 0, p))
    p2spec = pl.BlockSpec((1, past, LANES), lambda bi, p: (bi, 0, 0))
    n2spec = pl.BlockSpec((1, ts, LANES), lambda bi, p: (bi, k2n_row0 // ts, 0))
    in_specs = [pspec, p2spec, pspec, qspec, n2spec, qspec]
    args = (k1p, k2p, vp, k1n, k2n, vn)
    if fox:
        in_specs, args = [qspec] + in_specs, (q1,) + args
    else:
        q2spec = pl.BlockSpec((1, ts, LANES), lambda bi, p: (bi, 0, p // 2))
        in_specs, args = [qspec, q2spec] + in_specs, (q1, q2) + args
    return pl.pallas_call(
        functools.partial(_attn_sample_kernel, past=past, fox=fox),
        out_shape=jax.ShapeDtypeStruct((b, ts, hw), BF16),
        grid=(b, pairs),
        in_specs=in_specs,
        out_specs=qspec,
        compiler_params=pltpu.CompilerParams(dimension_semantics=("arbitrary",) * 2,
                                             vmem_limit_bytes=VMEM_LIMIT),
        name="attn_sample_fox" if fox else "attn_sample_mla",
    )(*args)


def _attn_t_kernel(*refs, tq, tk, fox):
    n_in = 4 if fox else 5
    q1_ref = refs[0]
    q2_ref = None if fox else refs[1]
    k1_ref, k2_ref, v_ref, o_ref = refs[n_in - 3:n_in + 1]
    (qt_ref, m_ref, acc_ref, s0_ref, s1_ref, p0_ref, p1_ref, a0_ref, a1_ref,
     bm0_ref, bm1_ref) = refs[n_in + 1:]
    s_refs, p_refs, a_refs, bm_refs = (s0_ref, s1_ref), (p0_ref, p1_ref), (a0_ref, a1_ref), (bm0_ref, bm1_ref)
    p = pl.program_id(1)
    qi = pl.program_id(2)
    lane = lax.broadcasted_iota(jnp.int32, (tq, LANES), 1)
    q1 = q1_ref[0].astype(F32)
    q2 = jnp.ones_like(q1) if fox else q2_ref[0].astype(F32)

    def head_rows(a):
        main = jnp.where((lane >= a * HEAD_DIM) & (lane < (a + 1) * HEAD_DIM), q1, 0.0)
        if fox:
            lo, width = LOGF_LANE0 + 3 * (2 * p + a), 3
        else:
            lo, width = ROPE_DIM * (2 * (p % 2) + a), ROPE_DIM
        aux = jnp.where((lane >= lo) & (lane < lo + width), q2, 0.0)
        return jnp.concatenate([main, aux], axis=1)

    qt_ref[...] = jnp.concatenate([head_rows(0), head_rows(1)], axis=0).T.astype(BF16)
    m_ref[...] = jnp.full(m_ref.shape, NEG, F32)
    acc_ref[...] = jnp.zeros(acc_ref.shape, F32)
    p1_ref[...] = jnp.zeros(p1_ref.shape, BF16)
    a1_ref[...] = jnp.zeros(a1_ref.shape, F32)

    q_start = qi * tq

    col_tiles = [slice(j, j + COL_TILE) for j in range(0, 2 * tq, COL_TILE)]

    def key_block(t):
        ks = pl.multiple_of(t * tk, tk)
        return jnp.concatenate([k1_ref[0, pl.ds(ks, tk), :], k2_ref[0, pl.ds(ks, tk), :]], axis=1)

    def value_block_t(kb):
        ks = pl.multiple_of(kb * tk, tk)
        return v_ref[0, pl.ds(ks, tk), :].astype(F32).T.astype(BF16)

    def scores(t, k, par, cols, masked):
        s = jnp.dot(k, qt_ref[:, cols], preferred_element_type=F32)
        if masked:
            c = cols.start + lax.broadcasted_iota(jnp.int32, s.shape, 1)
            t_pos = q_start + jnp.where(c >= tq, c - tq, c)
            s_pos = t * tk + lax.broadcasted_iota(jnp.int32, s.shape, 0)
            vis = (s_pos <= t_pos) if fox else ((s_pos // CHUNK) <= (t_pos // CHUNK))
            s = jnp.where(vis, s, NEG)
        s_refs[par][:, cols] = s
        bm_refs[par][:, cols] = jnp.max(s, axis=0, keepdims=True)

    def softmax(par, cols):
        m_prev = m_ref[:, cols]
        m_new = jnp.maximum(m_prev, bm_refs[par][:, cols])
        a_refs[par][:, cols] = jnp.exp2(m_prev - m_new)
        m_ref[:, cols] = m_new
        p_refs[par][:, cols] = jnp.exp2((s_refs[par][:, cols] - m_new).astype(BF16))

    ones_rows = jnp.ones((ONES_ROWS, tk), BF16)

    def values(v_t, par, cols):
        head = cols.start // tq
        lhs = jnp.concatenate([v_t[head * HEAD_DIM:(head + 1) * HEAD_DIM], ones_rows], axis=0)
        pv = jnp.dot(lhs, p_refs[par][:, cols], preferred_element_type=F32)
        acc_ref[:, cols] = a_refs[par][:, cols] * acc_ref[:, cols] + pv[:ACC_ROWS]

    def tick(t, masked=False, with_scores=True, with_softmax=True):
        par = t % 2 if isinstance(t, int) else t.par
        tv = t if isinstance(t, int) else t.value
        k = key_block(tv) if with_scores else None
        v_t = value_block_t(jnp.maximum(tv - 2, 0))
        for cols in col_tiles:
            if with_softmax:
                softmax(1 - par, cols)
            if with_scores:
                scores(tv, k, par, cols, masked)
            values(v_t, par, cols)

    class Tick:
        def __init__(self, value, par):
            self.value, self.par = value, par

    last = qi

    @pl.when(qi == 0)
    def _():
        k0 = key_block(0)
        for cols in col_tiles:
            scores(0, k0, 0, cols, True)
        tick(1, with_scores=False)
        tick(2, with_scores=False, with_softmax=False)

    @pl.when(qi > 0)
    def _():
        k0 = key_block(0)
        for cols in col_tiles:
            scores(0, k0, 0, cols, False)
        unmasked = last - 1

        def tick_pair(j, carry):
            tick(Tick(2 * j + 1, 1))
            tick(Tick(2 * j + 2, 0))
            return carry

        lax.fori_loop(0, unmasked // 2, tick_pair, 0)

        @pl.when(unmasked % 2 == 1)
        def _():
            tick(Tick(last - 1, 1))
            tick(Tick(last, 0), masked=True)
            tick(Tick(last + 1, 1), with_scores=False)
            tick(Tick(last + 2, 0), with_scores=False, with_softmax=False)

        @pl.when(unmasked % 2 == 0)
        def _():
            tick(Tick(last, 1), masked=True)
            tick(Tick(last + 1, 0), with_scores=False)
            tick(Tick(last + 2, 1), with_scores=False, with_softmax=False)

    acc = acc_ref[...]
    out_t = jnp.concatenate([acc[:HEAD_DIM, :tq] / acc[HEAD_DIM:HEAD_DIM + 1, :tq],
                             acc[:HEAD_DIM, tq:] / acc[HEAD_DIM:HEAD_DIM + 1, tq:]], axis=0)
    o_ref[0] = out_t.T.astype(o_ref.dtype)


def _attn_t(q1, q2, k1, k2, v, *, tq, fox):
    b, t, hw = q1.shape
    tk = tq
    m2 = 2 * tq
    pairs = hw // LANES
    qspec = pl.BlockSpec((1, tq, LANES), lambda bi, p, qi: (bi, qi, p))
    kspec = pl.BlockSpec((1, t, LANES), lambda bi, p, qi: (bi, 0, p))
    k2spec = pl.BlockSpec((1, t, LANES), lambda bi, p, qi: (bi, 0, 0))
    if fox:
        in_specs = [qspec, kspec, k2spec, kspec]
        args = (q1, k1, k2, v)
    else:
        q2spec = pl.BlockSpec((1, tq, LANES), lambda bi, p, qi: (bi, qi, p // 2))
        in_specs = [qspec, q2spec, kspec, k2spec, kspec]
        args = (q1, q2, k1, k2, v)
    return pl.pallas_call(
        functools.partial(_attn_t_kernel, tq=tq, tk=tk, fox=fox),
        out_shape=jax.ShapeDtypeStruct((b, t, hw), BF16),
        grid=(b, pairs, t // tq),
        in_specs=in_specs,
        out_specs=qspec,
        scratch_shapes=[pltpu.VMEM((2 * LANES, m2), BF16),
                        pltpu.VMEM((1, m2), F32),
                        pltpu.VMEM((ACC_ROWS, m2), F32),
                        pltpu.VMEM((tk, m2), F32), pltpu.VMEM((tk, m2), F32),
                        pltpu.VMEM((tk, m2), BF16), pltpu.VMEM((tk, m2), BF16),
                        pltpu.VMEM((1, m2), F32), pltpu.VMEM((1, m2), F32),
                        pltpu.VMEM((1, m2), F32), pltpu.VMEM((1, m2), F32)],
        compiler_params=pltpu.CompilerParams(dimension_semantics=("arbitrary",) * 3,
                                             vmem_limit_bytes=VMEM_LIMIT),
        name="attn_t_fox" if fox else "attn_t_mla",
    )(*args)


class _Item:
    def __init__(self, q, kb, qpar, kpar, first=False, last=False):
        self.q, self.kb, self.qpar, self.kpar, self.first, self.last = q, kb, qpar, kpar, first, last
        self.slot = 2 * qpar + kpar


def _attn_flat_kernel(*refs, tq, nq, fox):
    n_in = 4 if fox else 5
    q1_ref = refs[0]
    q2_ref = None if fox else refs[1]
    k1_ref, k2_ref, vt_ref, o_ref = refs[n_in - 3:n_in + 1]
    scratch = refs[n_in + 1:]
    qt_refs, m_refs, acc_refs = scratch[0:2], scratch[2:4], scratch[4:6]
    s_refs, p_refs, a_refs, bm_refs = scratch[6:10], scratch[10:14], scratch[14:18], scratch[18:22]
    tk = tq
    pair = pl.program_id(1)
    col_tiles = [slice(j, j + COL_TILE) for j in range(0, 2 * tq, COL_TILE)]
    ones_rows = jnp.ones((ONES_ROWS, tk), BF16)

    def rows_of(blk):
        start = blk * tq
        return pl.ds(start if isinstance(start, int) else pl.multiple_of(start, tq), tq)

    def setup(q, qpar):
        lane = lax.broadcasted_iota(jnp.int32, (tq, LANES), 1)
        q1 = q1_ref[rows_of(q), :].astype(F32)
        q2 = jnp.ones_like(q1) if fox else q2_ref[rows_of(q), :].astype(F32)

        def head_rows(a):
            main = jnp.where((lane >= a * HEAD_DIM) & (lane < (a + 1) * HEAD_DIM), q1, 0.0)
            if fox:
                lo, width = LOGF_LANE0 + 3 * (2 * pair + a), 3
            else:
                lo, width = ROPE_DIM * (2 * (pair % 2) + a), ROPE_DIM
            aux = jnp.where((lane >= lo) & (lane < lo + width), q2, 0.0)
            return jnp.concatenate([main, aux], axis=1)

        qt_refs[qpar][...] = jnp.concatenate([head_rows(0), head_rows(1)], axis=0).T.astype(BF16)

    def key_block(kb):
        return jnp.concatenate([k1_ref[rows_of(kb), :], k2_ref[rows_of(kb), :]], axis=1)

    def scores(x, cols):
        s = jnp.dot(key_block(x.kb), qt_refs[x.qpar][:, cols], preferred_element_type=F32)
        if x.last:
            c = (cols.start % tq) + lax.broadcasted_iota(jnp.int32, s.shape, 1)
            r = lax.broadcasted_iota(jnp.int32, s.shape, 0)
            vis = (r <= c) if fox else ((r // CHUNK) <= (c // CHUNK))
            s = jnp.where(vis, s, NEG)
        s_refs[x.slot][cols.start // COL_TILE] = s
        bm_refs[x.slot][:, cols] = jnp.max(s, axis=0, keepdims=True)

    def softmax(x, cols):
        bm = bm_refs[x.slot][:, cols]
        if x.first:
            m_new = bm
            a_refs[x.slot][:, cols] = jnp.zeros_like(bm)
        else:
            m_prev = m_refs[x.qpar][:, cols]
            m_new = jnp.maximum(m_prev, bm)
            a_refs[x.slot][:, cols] = jnp.exp2(m_prev - m_new)
        m_refs[x.qpar][:, cols] = m_new
        ct = cols.start // COL_TILE
        p_refs[x.slot][ct] = jnp.exp2(s_refs[x.slot][ct] - m_new).astype(BF16)

    def values(x, cols):
        head = cols.start // tq
        v_t = vt_ref[x.kb, head * HEAD_DIM:(head + 1) * HEAD_DIM, :]
        lhs = jnp.concatenate([v_t, ones_rows], axis=0)
        pv = jnp.dot(lhs, p_refs[x.slot][cols.start // COL_TILE], preferred_element_type=F32)[:ACC_ROWS]
        acc_ref = acc_refs[x.qpar]
        if x.first:
            acc_ref[:, cols] = pv
        else:
            acc_ref[:, cols] = a_refs[x.slot][:, cols] * acc_ref[:, cols] + pv

    def finalize(x):
        acc = acc_refs[x.qpar][...]
        out_t = jnp.concatenate([acc[:HEAD_DIM, :tq] / acc[HEAD_DIM:HEAD_DIM + 1, :tq],
                                 acc[:HEAD_DIM, tq:] / acc[HEAD_DIM:HEAD_DIM + 1, tq:]], axis=0)
        o_ref[rows_of(x.q), :] = out_t.T.astype(o_ref.dtype)

    def tick(xs, xm, xv, next_q=None):
        for cols in col_tiles:
            if xs is not None:
                scores(xs, cols)
            if xv is not None:
                values(xv, cols)
            if xm is not None:
                softmax(xm, cols)
        if xv is not None and xv.last:
            finalize(xv)
        if next_q is not None:
            setup(*next_q)

    def generic_pairs(q, qpar, count):
        def body(i, carry):
            kb = 3 + 2 * i
            x_a, x_b = _Item(q, kb, qpar, 1), _Item(q, kb + 1, qpar, 0)
            tick(x_a, _Item(q, kb - 1, qpar, 0), _Item(q, kb - 2, qpar, 1))
            tick(x_b, x_a, _Item(q, kb - 1, qpar, 0))
            return carry
        lax.fori_loop(0, count, body, 0)

    def query_block(q, qpar, is_last_q=False):
        p2 = _Item(q - 1, q - 2, 1 - qpar, qpar)
        p1 = _Item(q - 1, q - 1, 1 - qpar, 1 - qpar, last=True)
        x0 = _Item(q, 0, qpar, 0, first=True)
        x1 = _Item(q, 1, qpar, 1)
        x2 = _Item(q, 2, qpar, 0)
        tick(x0, p1, p2)
        tick(x1, x0, p1)
        tick(x2, x1, x0)
        nxt = None if is_last_q else (q + 1, 1 - qpar)
        if qpar == 1:
            generic_pairs(q, qpar, (q - 3) // 2)
            a2, a1 = _Item(q, q - 2, qpar, 1), _Item(q, q - 1, qpar, 0)
        else:
            generic_pairs(q, qpar, (q - 4) // 2)
            a1 = _Item(q, q - 1, qpar, 1)
            a2 = _Item(q, q - 2, qpar, 0)
            tick(a1, a2, _Item(q, q - 3, qpar, 1))
        xl = _Item(q, q, qpar, qpar, last=True)
        tick(xl, a1, a2, next_q=nxt)
        return a1, xl

    setup(0, 0)
    x00 = _Item(0, 0, 0, 0, first=True, last=True)
    x10, x11 = _Item(1, 0, 1, 0, first=True), _Item(1, 1, 1, 1, last=True)
    x20, x21, x22 = _Item(2, 0, 0, 0, first=True), _Item(2, 1, 0, 1), _Item(2, 2, 0, 0, last=True)
    tick(x00, None, None, next_q=(1, 1))
    tick(x10, x00, None)
    tick(x11, x10, x00, next_q=(2, 0))
    tick(x20, x11, x10)
    tick(x21, x20, x11)
    tick(x22, x21, x20, next_q=(3, 1))

    def block_pair(j, carry):
        query_block(3 + 2 * j, 1)
        query_block(4 + 2 * j, 0)
        return carry

    lax.fori_loop(0, (nq - 4) // 2, block_pair, 0)
    a1, xl = query_block(nq - 1, 1, is_last_q=True)
    tick(None, xl, a1)
    tick(None, None, xl)


def _attn_flat(q1, q2, k1, k2, v_t, *, tq, fox):
    pairs, b, t, _ = q1.shape
    hw = pairs * LANES
    nq = t // tq
    assert t % tq == 0 and nq % 2 == 0 and nq >= 4, (t, tq)
    assert v_t.shape == (b, nq, hw, tq), v_t.shape
    m2 = 2 * tq
    spec = pl.BlockSpec((None, None, t, LANES), lambda bi, p: (p, bi, 0, 0))
    k2spec = pl.BlockSpec((None, t, LANES), lambda bi, p: (bi, 0, 0))
    vspec = pl.BlockSpec((None, nq, LANES, tq), lambda bi, p: (bi, 0, p, 0))
    if fox:
        in_specs = [spec, spec, k2spec, vspec]
        args = (q1, k1, k2, v_t)
    else:
        q2spec = pl.BlockSpec((None, None, t, LANES), lambda bi, p: (p // 2, bi, 0, 0))
        in_specs = [spec, q2spec, spec, k2spec, vspec]
        args = (q1, q2, k1, k2, v_t)
    vmem = lambda shape, dt, n: [pltpu.VMEM(shape, dt) for _ in range(n)]
    return pl.pallas_call(
        functools.partial(_attn_flat_kernel, tq=tq, nq=nq, fox=fox),
        out_shape=jax.ShapeDtypeStruct((pairs, b, t, LANES), BF16),
        grid=(b, pairs),
        in_specs=in_specs,
        out_specs=spec,
        scratch_shapes=(vmem((2 * LANES, m2), BF16, 2)
                        + vmem((1, m2), F32, 2)
                        + vmem((ACC_ROWS, m2), F32, 2)
                        + vmem((m2 // COL_TILE, tq, COL_TILE), F32, 4)
                        + vmem((m2 // COL_TILE, tq, COL_TILE), BF16, 4)
                        + vmem((1, m2), F32, 4)
                        + vmem((1, m2), F32, 4)),
        compiler_params=pltpu.CompilerParams(dimension_semantics=("arbitrary",) * 2,
                                             vmem_limit_bytes=VMEM_LIMIT),
        name="attn_flat_fox" if fox else "attn_flat_mla",
    )(*args)


def _rope_tables(pos):
    half = ROPE_DIM // 2
    freqs = ROPE_THETA ** (-jnp.arange(half, dtype=F32) / half)
    ang = pos.astype(F32)[:, None] * freqs[None, :]
    cos, sin = jnp.cos(ang), jnp.sin(ang)
    cos_t = jnp.tile(jnp.concatenate([cos, cos], axis=1), (1, HEADS))
    sin_t = jnp.tile(jnp.concatenate([-sin, sin], axis=1), (1, HEADS))
    return cos_t, sin_t


def _swap_halves(w):
    half = w.shape[-1] // 2
    return jnp.concatenate([w[..., half:], w[..., :half]], axis=-1)


def _prep_weights(w_in, b_forget, w_q_up, w_kv_up, w_out):
    d = w_in.shape[0]
    hw = HEADS * HEAD_DIM
    o = 0
    cq = w_in[:, o:o + Q_LORA]; o += Q_LORA
    ckv = w_in[:, o:o + KV_LORA]; o += KV_LORA
    kr = w_in[:, o:o + ROPE_DIM]; o += ROPE_DIM
    fq = w_in[:, o:o + hw]; o += hw
    fk = w_in[:, o:o + hw]; o += hw
    fv = w_in[:, o:o + hw]; o += hw
    gate = w_in[:, o:o + HEADS]
    reps = LANES // ROPE_DIM
    pad = LANES - 4 * HEADS
    gate128 = jnp.concatenate([gate, jnp.repeat(gate, 3, axis=1), jnp.zeros((d, pad), w_in.dtype)], axis=1)
    w_in_p = jnp.concatenate([cq, ckv, fq, fk, fv, jnp.tile(kr, (1, reps)), jnp.tile(_swap_halves(kr), (1, reps)),
                              gate128], axis=1).astype(BF16)
    bias128 = jnp.concatenate([b_forget, jnp.repeat(b_forget, 3), jnp.zeros((pad,), F32)])[None, :]
    wq = w_q_up.reshape(Q_LORA, HEADS, NOPE_DIM + ROPE_DIM)
    wq_rope = wq[:, :, NOPE_DIM:]
    w_q_p = jnp.concatenate([wq[:, :, :NOPE_DIM].reshape(Q_LORA, -1), wq_rope.reshape(Q_LORA, -1),
                             _swap_halves(wq_rope).reshape(Q_LORA, -1)], axis=1).astype(BF16)
    wkv = w_kv_up.reshape(KV_LORA, HEADS, NOPE_DIM + HEAD_DIM)
    w_kv_p = jnp.concatenate([wkv[:, :, :NOPE_DIM].reshape(KV_LORA, -1), wkv[:, :, NOPE_DIM:].reshape(KV_LORA, -1)],
                             axis=1).astype(BF16)
    w_o1 = w_out[:hw].astype(BF16)
    w_o2 = w_out[hw:].astype(BF16)
    return w_in_p, bias128, w_q_p, w_kv_p, w_o1, w_o2


def _expand_logf(lf):
    pad = LANES - 4 * HEADS
    return jnp.concatenate([lf, jnp.repeat(lf, 3, axis=-1), jnp.zeros(lf.shape[:-1] + (pad,), lf.dtype)], axis=-1)


def kernel(x_prompt, x_sample, cache_mla_ckv, cache_mla_krope, cache_fox_k, cache_fox_v, cache_fox_logf,
           g_ffn1_pre, g_ffn1_post, w_ffn1_gu, w_ffn1_down, g_mix_pre, g_mix_post, w_in, b_forget,
           g_q_latent, w_q_up, g_kv_latent, w_kv_up, w_out, g_ffn2_pre, g_ffn2_post, w_ffn2_gu, w_ffn2_down):
    depth = w_in.shape[0]
    bp, tp, d = x_prompt.shape
    bs, ts, _ = x_sample.shape
    past = cache_mla_ckv.shape[2]
    hw = HEADS * HEAD_DIM
    tq_p = 512
    tc_s = 512
    tk_pad = -(-(past + ts) // tc_s) * tc_s

    cos_p, sin_p = _rope_tables(jnp.arange(tp))
    cos_s, sin_s = _rope_tables(past + jnp.arange(ts))
    cos_s, sin_s = jnp.tile(cos_s, (bs, 1)), jnp.tile(sin_s, (bs, 1))

    xp = x_prompt.reshape(bp * tp, d)
    xs = x_sample.reshape(bs * ts, d)
    tm_p = 512
    tm_s = bs * ts
    rows_p, rows_s = [], []

    def pad_keys(parts):
        n = sum(a.shape[1] for a in parts)
        parts = list(parts) + [jnp.zeros((bs, tk_pad - n, parts[0].shape[2]), parts[0].dtype)]
        return jnp.concatenate(parts, axis=1)

    for l in range(depth):
        d_ff = w_ffn1_down.shape[1]
        w1 = (w_ffn1_gu[l][:, :d_ff].astype(BF16), w_ffn1_gu[l][:, d_ff:].astype(BF16), w_ffn1_down[l].astype(BF16))
        w2 = (w_ffn2_gu[l][:, :d_ff].astype(BF16), w_ffn2_gu[l][:, d_ff:].astype(BF16), w_ffn2_down[l].astype(BF16))
        w_in_p, bias128, w_q_p, w_kv_p, w_o1, w_o2 = _prep_weights(w_in[l], b_forget[l], w_q_up[l], w_kv_up[l], w_out[l])
        g1 = (g_ffn1_pre[l][None, :], g_ffn1_post[l][None, :])
        g2 = (g_ffn2_pre[l][None, :], g_ffn2_post[l][None, :])
        gm_pre, gm_post = g_mix_pre[l][None, :], g_mix_post[l][None, :]
        gq, gkv = g_q_latent[l][None, :], g_kv_latent[l][None, :]

        hp = _ffn(xp, *g1, *w1, tm=tm_p)
        w_vt = (w_kv_p[:, hw:].T, w_in_p[:, _C_FV:_C_KR].T)
        (ckv, krope, fk, fv, logf, qn, qr, kn, vm_t, kr4, fqb, fkb, fv_t, ls) = _proj(
            hp, cos_p, sin_p, gm_pre, w_in_p, bias128, gq, w_q_p, gkv, w_kv_p, w_vt, tm=tq_p)
        sh = lambda a: a.reshape(bp, tp, a.shape[-1])
        sh_g = lambda a: a.reshape(a.shape[0], bp, tp, LANES)
        sh_t = lambda a: a.reshape(bp, tp // tq_p, hw, tq_p)
        o_mla = _attn_flat(sh_g(qn), sh_g(qr), sh_g(kn), sh(kr4), sh_t(vm_t), tq=tq_p, fox=False)
        o_fox = _attn_flat(sh_g(fqb), None, sh_g(fkb), sh(ls), sh_t(fv_t), tq=tq_p, fox=True)
        un_g = lambda o: o.reshape(o.shape[0], bp * tp, LANES)
        xp = _ffn(hp, *g2, *w2, mix=(un_g(o_mla), un_g(o_fox), w_o1, w_o2, gm_post), tm=tm_p)
        rows_p.append((ckv.reshape(bp, tp, KV_LORA), krope.reshape(bp, tp, ROPE_DIM),
                       fk.reshape(bp, tp, HEADS, HEAD_DIM), fv.reshape(bp, tp, HEADS, HEAD_DIM),
                       logf.reshape(bp, tp, HEADS)))

        hs = _ffn(xs, *g1, *w1, tm=tm_s)
        (ckv, krope, fk, fv, logf, qn, qr, kn, vm, kr4, fqb, _, _, lf128) = _proj(
            hs, cos_s, sin_s, gm_pre, w_in_p, bias128, gq, w_q_p, gkv, w_kv_p, tm=tm_s)
        kn_past, vm_past, kr4_past = _pastkv(cache_mla_ckv[l].reshape(bs * past, KV_LORA),
                                             cache_mla_krope[l].reshape(bs * past, ROPE_DIM), w_kv_p, tm=512)
        sh = lambda a: a.reshape(bs, -1, a.shape[-1])
        lf_all = pad_keys([_expand_logf(cache_fox_logf[l].astype(F32)), sh(lf128)])
        ls = _lsplit(lf_all, tc=tc_s)
        o_mla = _attn_sample(sh(qn), sh(qr), sh(kn_past), sh(kr4_past), sh(vm_past),
                             sh(kn), sh(kr4), sh(vm), fox=False)
        o_fox = _attn_sample(sh(fqb), None, cache_fox_k[l].reshape(bs, past, hw), ls,
                             cache_fox_v[l].reshape(bs, past, hw), sh(fk), ls, sh(fv),
                             k2n_row0=past, fox=True)
        xs = _ffn(hs, *g2, *w2, mix=(o_mla.reshape(-1, hw), o_fox.reshape(-1, hw), w_o1, w_o2, gm_post), tm=tm_s)
        rows_s.append((ckv.reshape(bs, ts, KV_LORA), krope.reshape(bs, ts, ROPE_DIM),
                       fk.reshape(bs, ts, HEADS, HEAD_DIM), fv.reshape(bs, ts, HEADS, HEAD_DIM),
                       logf.reshape(bs, ts, HEADS)))

    outs_p = [jnp.stack([r[i] for r in rows_p]) for i in range(5)]
    outs_s = [jnp.stack([r[i] for r in rows_s]) for i in range(5)]
    return (xp.reshape(bp, tp, d), xs.reshape(bs, ts, d), *outs_p, *outs_s)
```

```python
import functools

import jax
import jax.numpy as jnp
from jax import lax
from jax.experimental import pallas as pl
from jax.experimental.pallas import tpu as pltpu

EPS = 1e-6
CHUNK = 64
ROPE_THETA = 10000.0
HEADS = 8
NOPE_DIM = 64
ROPE_DIM = 32
HEAD_DIM = 64
Q_LORA = 384
KV_LORA = 256
LANES = 128
COL_TILE = 256
ONES_ROWS = 16
ACC_ROWS = HEAD_DIM + 8
LOG2E = 1.4426950408889634
NEG = -1e30
LOGF_LANE0 = 8
VMEM_LIMIT = 56 * 1024 * 1024

BF16 = jnp.bfloat16
F32 = jnp.float32


def _rms(x, g):
    ms = jnp.mean(x * x, axis=-1, keepdims=True)
    return x * lax.rsqrt(ms + EPS) * g


def _const_spec(shape):
    return pl.BlockSpec(shape, lambda *_: (0,) * len(shape), pipeline_mode=pl.Buffered(1))


def _ffn_kernel(*refs, ff_chunk, with_mix):
    if with_mix:
        (h_ref, o1_ref, o2_ref, wo1_ref, wo2_ref, gmix_ref,
         gpre_ref, gpost_ref, wg_ref, wu_ref, wd_ref, out_ref) = refs
        def rows(o_ref):
            if len(o_ref.shape) == 2:
                return o_ref[...]
            return jnp.concatenate([o_ref[g] for g in range(o_ref.shape[0])], axis=1)
        mix = jnp.dot(rows(o1_ref), wo1_ref[...], preferred_element_type=F32)
        mix = mix + jnp.dot(rows(o2_ref), wo2_ref[...], preferred_element_type=F32)
        x = h_ref[...] + _rms(mix, gmix_ref[...])
    else:
        x_ref, gpre_ref, gpost_ref, wg_ref, wu_ref, wd_ref, out_ref = refs
        x = x_ref[...]
    n = _rms(x, gpre_ref[...]).astype(BF16)
    d_ff = wg_ref.shape[1]
    acc = jnp.zeros(x.shape, F32)
    for c in range(d_ff // ff_chunk):
        cols = slice(c * ff_chunk, (c + 1) * ff_chunk)
        gate = jnp.dot(n, wg_ref[:, cols], preferred_element_type=F32)
        up = jnp.dot(n, wu_ref[:, cols], preferred_element_type=F32)
        act = (gate * jax.nn.sigmoid(gate) * up).astype(BF16)
        acc = acc + jnp.dot(act, wd_ref[cols, :], preferred_element_type=F32)
    out_ref[...] = x + 0.5 * _rms(acc, gpost_ref[...])


def _ffn(x, g_pre, g_post, w_g, w_u, w_d, mix=None, *, tm):
    n, d = x.shape
    d_ff = w_g.shape[1]
    row = lambda w: pl.BlockSpec((tm, w), lambda i: (i, 0))
    in_specs = [row(d)]
    args = [x]
    if mix is not None:
        o1, o2, wo1, wo2, g_mix = mix
        grouped = lambda o: pl.BlockSpec((o.shape[0], tm, LANES), lambda i: (0, i, 0))
        in_specs += [row(o.shape[1]) if o.ndim == 2 else grouped(o) for o in (o1, o2)]
        in_specs += [_const_spec(wo1.shape), _const_spec(wo2.shape), _const_spec((1, d))]
        args += [o1, o2, wo1, wo2, g_mix]
    in_specs += [_const_spec((1, d)), _const_spec((1, d)), _const_spec(w_g.shape), _const_spec(w_u.shape),
                 _const_spec(w_d.shape)]
    args += [g_pre, g_post, w_g, w_u, w_d]
    return pl.pallas_call(
        functools.partial(_ffn_kernel, ff_chunk=256, with_mix=mix is not None),
        out_shape=jax.ShapeDtypeStruct((n, d), F32),
        grid=(n // tm,),
        in_specs=in_specs,
        out_specs=row(d),
        compiler_params=pltpu.CompilerParams(dimension_semantics=("arbitrary",), vmem_limit_bytes=VMEM_LIMIT),
        name="ffn_mix" if mix is not None else "ffn",
    )(*args)


_C_CQ = 0
_C_CKV = _C_CQ + Q_LORA
_C_FQ = _C_CKV + KV_LORA
_C_FK = _C_FQ + HEADS * HEAD_DIM
_C_FV = _C_FK + HEADS * HEAD_DIM
_C_KR = _C_FV + HEADS * HEAD_DIM
_C_KRS = _C_KR + LANES
_C_GATE = _C_KRS + LANES
_C_END = _C_GATE + LANES


def _log_sigmoid(x):
    return jnp.minimum(x, 0.0) - jnp.log(1.0 + jnp.exp(-jnp.abs(x)))


def _proj_kernel(*refs, values_t, stream_blocks):
    (h_ref, cos_ref, sin_ref, gpre_ref, win_ref, bias_ref, gq_ref, wq_ref, gkv_ref, wkv_ref) = refs[:10]
    n_in = 11 if values_t else 10
    (ckv_ref, krope_ref, fk_ref, fv_ref, logf_ref,
     qn_ref, qr_ref, kn_ref, vm_ref, kr4_ref, fqb_ref, fkb_ref, fvb_ref, lf128_ref) = refs[n_in:n_in + 14]
    nt_dims = (((1,), (1,)), ((), ()))
    hw = HEADS * HEAD_DIM
    u = _rms(h_ref[...], gpre_ref[...]).astype(BF16)
    proj = jnp.dot(u, win_ref[...], preferred_element_type=F32)
    cos = cos_ref[...]
    sin = sin_ref[...]

    cq = _rms(proj[:, _C_CQ:_C_CKV], gq_ref[...]).astype(BF16)
    q = jnp.dot(cq, wq_ref[...], preferred_element_type=F32)
    q_scale = (NOPE_DIM + ROPE_DIM) ** -0.5 * LOG2E
    rw = HEADS * ROPE_DIM
    def put(ref, x):
        if values_t:
            for g in range(x.shape[1] // LANES):
                ref[g] = x[:, g * LANES:(g + 1) * LANES]
        else:
            ref[...] = x

    put(qn_ref, (q[:, :hw] * q_scale).astype(BF16))
    put(qr_ref, ((q[:, hw:hw + rw] * cos + q[:, hw + rw:] * sin) * q_scale).astype(BF16))

    ckv = _rms(proj[:, _C_CKV:_C_FQ], gkv_ref[...])
    ckv_ref[...] = ckv
    ckv_b = ckv.astype(BF16)
    if values_t:
        put(kn_ref, jnp.dot(ckv_b, wkv_ref[:, :hw], preferred_element_type=F32).astype(BF16))
        vm_ref[0] = lax.dot_general(refs[10][...], ckv_b, nt_dims, preferred_element_type=F32).astype(BF16)
    else:
        kv = jnp.dot(ckv_b, wkv_ref[...], preferred_element_type=F32)
        kn_ref[...] = kv[:, :hw].astype(BF16)
        vm_ref[...] = kv[:, hw:].astype(BF16)
    kr4 = proj[:, _C_KR:_C_KRS] * cos[:, :LANES] + proj[:, _C_KRS:_C_GATE] * sin[:, :LANES]
    krope_ref[...] = kr4[:, :ROPE_DIM]
    kr4_ref[...] = kr4.astype(BF16)

    fq = proj[:, _C_FQ:_C_FK]
    fk = proj[:, _C_FK:_C_FV]
    fv = proj[:, _C_FV:_C_KR]
    put(fqb_ref, (fq * (HEAD_DIM ** -0.5 * LOG2E)).astype(BF16))
    if values_t:
        def by_head(x):
            heads = jnp.stack([x[:, h * HEAD_DIM:(h + 1) * HEAD_DIM] for h in range(HEADS)], axis=0)
            return jnp.swapaxes(heads, 0, 1)
        fk_ref[...] = by_head(fk)
        fv_ref[...] = by_head(fv)
    else:
        fk_ref[...] = fk
        fv_ref[...] = fv
    put(fkb_ref, fk.astype(BF16))
    if values_t:
        fvb_ref[0] = fv.T.astype(BF16)
    else:
        fvb_ref[...] = fv.astype(BF16)
    logf = _log_sigmoid(proj[:, _C_GATE:_C_END] + bias_ref[...])
    logf_ref[...] = logf[:, :HEADS]
    if values_t:
        carry_ref = refs[-1]

        @pl.when(pl.program_id(0) % stream_blocks == 0)
        def _():
            carry_ref[...] = jnp.zeros_like(carry_ref)

        lf128_ref[...] = _cumsum_split(logf, carry_ref)
    else:
        lf128_ref[...] = logf


def _proj(h, cos, sin, g_pre, w_in, bias, g_q, w_q, g_kv, w_kv, w_v_t=None, *, tm):
    n, d = h.shape
    t_blocks = cos.shape[0] // tm
    hw = HEADS * HEAD_DIM
    values_t = w_v_t is not None
    row = lambda w: pl.BlockSpec((tm, w), lambda i: (i, 0))
    tab = pl.BlockSpec((tm, HEADS * ROPE_DIM), lambda i: (i % t_blocks, 0))
    widths = [(KV_LORA, F32), (ROPE_DIM, F32), (hw, F32), (hw, F32), (HEADS, F32),
              (hw, BF16), (HEADS * ROPE_DIM, BF16), (hw, BF16), (hw, BF16), (LANES, BF16),
              (hw, BF16), (hw, BF16), (hw, BF16), (LANES, F32)]
    out_shape = [jax.ShapeDtypeStruct((n, w), dt) for w, dt in widths]
    out_specs = [row(w) for w, _ in widths]
    in_specs = [row(d), tab, tab, _const_spec((1, d)), _const_spec(w_in.shape), _const_spec((1, LANES)),
                _const_spec((1, Q_LORA)), _const_spec(w_q.shape), _const_spec((1, KV_LORA)),
                _const_spec(w_kv.shape)]
    args = [h, cos, sin, g_pre, w_in, bias, g_q, w_q, g_kv, w_kv]
    if values_t:
        for i in (8, 12):
            out_shape[i] = jax.ShapeDtypeStruct((n // tm, hw, tm), BF16)
            out_specs[i] = pl.BlockSpec((1, hw, tm), lambda i: (i, 0, 0))
        out_shape[13] = jax.ShapeDtypeStruct((n, LANES), BF16)
        for i in (5, 6, 7, 10, 11):
            groups = widths[i][0] // LANES
            out_shape[i] = jax.ShapeDtypeStruct((groups, n, LANES), BF16)
            out_specs[i] = pl.BlockSpec((groups, tm, LANES), lambda i: (0, i, 0))
        for i in (2, 3):
            out_shape[i] = jax.ShapeDtypeStruct((n, HEADS, HEAD_DIM), F32)
            out_specs[i] = pl.BlockSpec((tm, HEADS, HEAD_DIM), lambda i: (i, 0, 0))
        in_specs.append(_const_spec(w_v_t.shape))
        args.append(w_v_t)
    return pl.pallas_call(
        functools.partial(_proj_kernel, values_t=values_t, stream_blocks=t_blocks),
        out_shape=out_shape,
        grid=(n // tm,),
        in_specs=in_specs,
        out_specs=out_specs,
        scratch_shapes=[pltpu.VMEM((1, LANES), F32)] if values_t else [],
        compiler_params=pltpu.CompilerParams(dimension_semantics=("arbitrary",), vmem_limit_bytes=VMEM_LIMIT),
        name="proj",
    )(*args)


def _pastkv_kernel(ckv_ref, kr_ref, wkv_ref, kn_ref, vm_ref, kr4_ref):
    hw = HEADS * HEAD_DIM
    kv = jnp.dot(ckv_ref[...].astype(BF16), wkv_ref[...], preferred_element_type=F32)
    kn_ref[...] = kv[:, :hw].astype(BF16)
    vm_ref[...] = kv[:, hw:].astype(BF16)
    src = lax.broadcasted_iota(jnp.int32, (ROPE_DIM, LANES), 0)
    dst = lax.broadcasted_iota(jnp.int32, (ROPE_DIM, LANES), 1)
    rep = (dst % ROPE_DIM == src).astype(BF16)
    kr4_ref[...] = jnp.dot(kr_ref[...].astype(BF16), rep, preferred_element_type=F32).astype(BF16)


def _pastkv(ckv, krope, w_kv, *, tm):
    n = ckv.shape[0]
    hw = HEADS * HEAD_DIM
    row = lambda w: pl.BlockSpec((tm, w), lambda i: (i, 0))
    return pl.pallas_call(
        _pastkv_kernel,
        out_shape=[jax.ShapeDtypeStruct((n, hw), BF16), jax.ShapeDtypeStruct((n, hw), BF16),
                   jax.ShapeDtypeStruct((n, LANES), BF16)],
        grid=(n // tm,),
        in_specs=[row(KV_LORA), row(ROPE_DIM), _const_spec(w_kv.shape)],
        out_specs=[row(hw), row(hw), row(LANES)],
        compiler_params=pltpu.CompilerParams(dimension_semantics=("arbitrary",), vmem_limit_bytes=VMEM_LIMIT),
        name="pastkv",
    )(ckv, krope, w_kv)


def _split3(y, lane):
    hi = y.astype(BF16).astype(F32)
    r1 = y - hi
    mid = r1.astype(BF16).astype(F32)
    lo = r1 - mid
    j = (lane - LOGF_LANE0) % 3
    sel = jnp.where(j == 0, hi, jnp.where(j == 1, mid, lo))
    used = (lane >= LOGF_LANE0) & (lane < LOGF_LANE0 + 3 * HEADS)
    return jnp.where(used, sel, 0.0).astype(BF16)


def _cumsum_split(x, carry_ref):
    tc = x.shape[0]
    hi = x.astype(BF16)
    r1 = x - hi.astype(F32)
    mid = r1.astype(BF16)
    lo = (r1 - mid.astype(F32)).astype(BF16)
    r = lax.broadcasted_iota(jnp.int32, (tc, tc), 0)
    c = lax.broadcasted_iota(jnp.int32, (tc, tc), 1)
    tri = (c <= r).astype(BF16)
    cum = (jnp.dot(tri, hi, preferred_element_type=F32) + jnp.dot(tri, mid, preferred_element_type=F32)
           + jnp.dot(tri, lo, preferred_element_type=F32)) + carry_ref[...]
    carry_ref[...] = cum[tc - 1:tc, :]
    lane = lax.broadcasted_iota(jnp.int32, (tc, LANES), 1)
    return _split3(cum * (-LOG2E), lane)


def _lsplit_kernel(x_ref, o_ref, carry_ref):
    @pl.when(pl.program_id(1) == 0)
    def _():
        carry_ref[...] = jnp.zeros_like(carry_ref)

    o_ref[0] = _cumsum_split(x_ref[0], carry_ref)


def _lsplit(lf128, *, tc):
    b, t, _ = lf128.shape
    spec = pl.BlockSpec((1, tc, LANES), lambda i, j: (i, j, 0))
    return pl.pallas_call(
        _lsplit_kernel,
        out_shape=jax.ShapeDtypeStruct((b, t, LANES), BF16),
        grid=(b, t // tc),
        in_specs=[spec],
        out_specs=spec,
        scratch_shapes=[pltpu.VMEM((1, LANES), F32)],
        compiler_params=pltpu.CompilerParams(dimension_semantics=("arbitrary", "arbitrary")),
        name="lsplit",
    )(lf128)


def _attn_sample_kernel(*refs, past, fox):
    n_q = 1 if fox else 2
    q1_ref = refs[0]
    q2_ref = None if fox else refs[1]
    k1p_ref, k2p_ref, vp_ref, k1n_ref, k2n_ref, vn_ref, o_ref = refs[n_q:]
    ts = q1_ref.shape[1]
    p = pl.program_id(1)
    lane = lax.broadcasted_iota(jnp.int32, (ts, LANES), 1)
    q1 = q1_ref[0].astype(F32)
    q2 = jnp.ones_like(q1) if fox else q2_ref[0].astype(F32)

    def head_rows(a):
        main = jnp.where((lane >= a * HEAD_DIM) & (lane < (a + 1) * HEAD_DIM), q1, 0.0)
        if fox:
            lo, width = LOGF_LANE0 + 3 * (2 * p + a), 3
        else:
            lo, width = ROPE_DIM * (2 * (p % 2) + a), ROPE_DIM
        aux = jnp.where((lane >= lo) & (lane < lo + width), q2, 0.0)
        return jnp.concatenate([main, aux], axis=1).astype(BF16)

    q = jnp.concatenate([head_rows(0), head_rows(1)], axis=0)
    nt_dims = (((1,), (1,)), ((), ()))

    def logits(k1_ref, k2_ref):
        k = jnp.concatenate([k1_ref[0].astype(BF16), k2_ref[0].astype(BF16)], axis=1)
        return lax.dot_general(q, k, nt_dims, preferred_element_type=F32)

    s_p = logits(k1p_ref, k2p_ref)
    s_n = logits(k1n_ref, k2n_ref)
    r = lax.broadcasted_iota(jnp.int32, s_n.shape, 0)
    t_pos = past + jnp.where(r >= ts, r - ts, r)
    s_pos = past + lax.broadcasted_iota(jnp.int32, s_n.shape, 1)
    vis = (s_pos <= t_pos) if fox else ((s_pos // CHUNK) <= (t_pos // CHUNK))
    s_n = jnp.where(vis, s_n, NEG)
    m = jnp.maximum(jnp.max(s_p, axis=1, keepdims=True), jnp.max(s_n, axis=1, keepdims=True))
    p_p = jnp.exp2(s_p - m)
    p_n = jnp.exp2(s_n - m)
    denom = jnp.sum(p_p, axis=1, keepdims=True) + jnp.sum(p_n, axis=1, keepdims=True)
    out = (jnp.dot(p_p.astype(BF16), vp_ref[0].astype(BF16), preferred_element_type=F32)
           + jnp.dot(p_n.astype(BF16), vn_ref[0].astype(BF16), preferred_element_type=F32)) / denom
    o_ref[0] = jnp.where(lane < HEAD_DIM, out[:ts], out[ts:]).astype(o_ref.dtype)


def _attn_sample(q1, q2, k1p, k2p, vp, k1n, k2n, vn, *, k2n_row0=0, fox):
    b, ts, hw = q1.shape
    past = k1p.shape[1]
    assert k2n_row0 % ts == 0
    pairs = hw // LANES
    qspec = pl.BlockSpec((1, ts, LANES), lambda bi, p: (bi, 0, p))
    pspec = pl.BlockSpec((1, past, LANES), lambda bi, p: (bi, 0, p))
    p2spec = pl.BlockSpec((1, past, LANES), lambda bi, p: (bi, 0, 0))
    n2spec = pl.BlockSpec((1, ts, LANES), lambda bi, p: (bi, k2n_row0 // ts, 0))
    in_specs = [pspec, p2spec, pspec, qspec, n2spec, qspec]
    args = (k1p, k2p, vp, k1n, k2n, vn)
    if fox:
        in_specs, args = [qspec] + in_specs, (q1,) + args
    else:
        q2spec = pl.BlockSpec((1, ts, LANES), lambda bi, p: (bi, 0, p // 2))
        in_specs, args = [qspec, q2spec] + in_specs, (q1, q2) + args
    return pl.pallas_call(
        functools.partial(_attn_sample_kernel, past=past, fox=fox),
        out_shape=jax.ShapeDtypeStruct((b, ts, hw), BF16),
        grid=(b, pairs),
        in_specs=in_specs,
        out_specs=qspec,
        compiler_params=pltpu.CompilerParams(dimension_semantics=("arbitrary",) * 2,
                                             vmem_limit_bytes=VMEM_LIMIT),
        name="attn_sample_fox" if fox else "attn_sample_mla",
    )(*args)


def _attn_t_kernel(*refs, tq, tk, fox):
    n_in = 4 if fox else 5
    q1_ref = refs[0]
    q2_ref = None if fox else refs[1]
    k1_ref, k2_ref, v_ref, o_ref = refs[n_in - 3:n_in + 1]
    (qt_ref, m_ref, acc_ref, s0_ref, s1_ref, p0_ref, p1_ref, a0_ref, a1_ref,
     bm0_ref, bm1_ref) = refs[n_in + 1:]
    s_refs, p_refs, a_refs, bm_refs = (s0_ref, s1_ref), (p0_ref, p1_ref), (a0_ref, a1_ref), (bm0_ref, bm1_ref)
    p = pl.program_id(1)
    qi = pl.program_id(2)
    lane = lax.broadcasted_iota(jnp.int32, (tq, LANES), 1)
    q1 = q1_ref[0].astype(F32)
    q2 = jnp.ones_like(q1) if fox else q2_ref[0].astype(F32)

    def head_rows(a):
        main = jnp.where((lane >= a * HEAD_DIM) & (lane < (a + 1) * HEAD_DIM), q1, 0.0)
        if fox:
            lo, width = LOGF_LANE0 + 3 * (2 * p + a), 3
        else:
            lo, width = ROPE_DIM * (2 * (p % 2) + a), ROPE_DIM
        aux = jnp.where((lane >= lo) & (lane < lo + width), q2, 0.0)
        return jnp.concatenate([main, aux], axis=1)

    qt_ref[...] = jnp.concatenate([head_rows(0), head_rows(1)], axis=0).T.astype(BF16)
    m_ref[...] = jnp.full(m_ref.shape, NEG, F32)
    acc_ref[...] = jnp.zeros(acc_ref.shape, F32)
    p1_ref[...] = jnp.zeros(p1_ref.shape, BF16)
    a1_ref[...] = jnp.zeros(a1_ref.shape, F32)

    q_start = qi * tq

    col_tiles = [slice(j, j + COL_TILE) for j in range(0, 2 * tq, COL_TILE)]

    def key_block(t):
        ks = pl.multiple_of(t * tk, tk)
        return jnp.concatenate([k1_ref[0, pl.ds(ks, tk), :], k2_ref[0, pl.ds(ks, tk), :]], axis=1)

    def value_block_t(kb):
        ks = pl.multiple_of(kb * tk, tk)
        return v_ref[0, pl.ds(ks, tk), :].astype(F32).T.astype(BF16)

    def scores(t, k, par, cols, masked):
        s = jnp.dot(k, qt_ref[:, cols], preferred_element_type=F32)
        if masked:
            c = cols.start + lax.broadcasted_iota(jnp.int32, s.shape, 1)
            t_pos = q_start + jnp.where(c >= tq, c - tq, c)
            s_pos = t * tk + lax.broadcasted_iota(jnp.int32, s.shape, 0)
            vis = (s_pos <= t_pos) if fox else ((s_pos // CHUNK) <= (t_pos // CHUNK))
            s = jnp.where(vis, s, NEG)
        s_refs[par][:, cols] = s
        bm_refs[par][:, cols] = jnp.max(s, axis=0, keepdims=True)

    def softmax(par, cols):
        m_prev = m_ref[:, cols]
        m_new = jnp.maximum(m_prev, bm_refs[par][:, cols])
        a_refs[par][:, cols] = jnp.exp2(m_prev - m_new)
        m_ref[:, cols] = m_new
        p_refs[par][:, cols] = jnp.exp2((s_refs[par][:, cols] - m_new).astype(BF16))

    ones_rows = jnp.ones((ONES_ROWS, tk), BF16)

    def values(v_t, par, cols):
        head = cols.start // tq
        lhs = jnp.concatenate([v_t[head * HEAD_DIM:(head + 1) * HEAD_DIM], ones_rows], axis=0)
        pv = jnp.dot(lhs, p_refs[par][:, cols], preferred_element_type=F32)
        acc_ref[:, cols] = a_refs[par][:, cols] * acc_ref[:, cols] + pv[:ACC_ROWS]

    def tick(t, masked=False, with_scores=True, with_softmax=True):
        par = t % 2 if isinstance(t, int) else t.par
        tv = t if isinstance(t, int) else t.value
        k = key_block(tv) if with_scores else None
        v_t = value_block_t(jnp.maximum(tv - 2, 0))
        for cols in col_tiles:
            if with_softmax:
                softmax(1 - par, cols)
            if with_scores:
                scores(tv, k, par, cols, masked)
            values(v_t, par, cols)

    class Tick:
        def __init__(self, value, par):
            self.value, self.par = value, par

    last = qi

    @pl.when(qi == 0)
    def _():
        k0 = key_block(0)
        for cols in col_tiles:
            scores(0, k0, 0, cols, True)
        tick(1, with_scores=False)
        tick(2, with_scores=False, with_softmax=False)

    @pl.when(qi > 0)
    def _():
        k0 = key_block(0)
        for cols in col_tiles:
            scores(0, k0, 0, cols, False)
        unmasked = last - 1

        def tick_pair(j, carry):
            tick(Tick(2 * j + 1, 1))
            tick(Tick(2 * j + 2, 0))
            return carry

        lax.fori_loop(0, unmasked // 2, tick_pair, 0)

        @pl.when(unmasked % 2 == 1)
        def _():
            tick(Tick(last - 1, 1))
            tick(Tick(last, 0), masked=True)
            tick(Tick(last + 1, 1), with_scores=False)
            tick(Tick(last + 2, 0), with_scores=False, with_softmax=False)

        @pl.when(unmasked % 2 == 0)
        def _():
            tick(Tick(last, 1), masked=True)
            tick(Tick(last + 1, 0), with_scores=False)
            tick(Tick(last + 2, 1), with_scores=False, with_softmax=False)

    acc = acc_ref[...]
    out_t = jnp.concatenate([acc[:HEAD_DIM, :tq] / acc[HEAD_DIM:HEAD_DIM + 1, :tq],
                             acc[:HEAD_DIM, tq:] / acc[HEAD_DIM:HEAD_DIM + 1, tq:]], axis=0)
    o_ref[0] = out_t.T.astype(o_ref.dtype)


def _attn_t(q1, q2, k1, k2, v, *, tq, fox):
    b, t, hw = q1.shape
    tk = tq
    m2 = 2 * tq
    pairs = hw // LANES
    qspec = pl.BlockSpec((1, tq, LANES), lambda bi, p, qi: (bi, qi, p))
    kspec = pl.BlockSpec((1, t, LANES), lambda bi, p, qi: (bi, 0, p))
    k2spec = pl.BlockSpec((1, t, LANES), lambda bi, p, qi: (bi, 0, 0))
    if fox:
        in_specs = [qspec, kspec, k2spec, kspec]
        args = (q1, k1, k2, v)
    else:
        q2spec = pl.BlockSpec((1, tq, LANES), lambda bi, p, qi: (bi, qi, p // 2))
        in_specs = [qspec, q2spec, kspec, k2spec, kspec]
        args = (q1, q2, k1, k2, v)
    return pl.pallas_call(
        functools.partial(_attn_t_kernel, tq=tq, tk=tk, fox=fox),
        out_shape=jax.ShapeDtypeStruct((b, t, hw), BF16),
        grid=(b, pairs, t // tq),
        in_specs=in_specs,
        out_specs=qspec,
        scratch_shapes=[pltpu.VMEM((2 * LANES, m2), BF16),
                        pltpu.VMEM((1, m2), F32),
                        pltpu.VMEM((ACC_ROWS, m2), F32),
                        pltpu.VMEM((tk, m2), F32), pltpu.VMEM((tk, m2), F32),
                        pltpu.VMEM((tk, m2), BF16), pltpu.VMEM((tk, m2), BF16),
                        pltpu.VMEM((1, m2), F32), pltpu.VMEM((1, m2), F32),
                        pltpu.VMEM((1, m2), F32), pltpu.VMEM((1, m2), F32)],
        compiler_params=pltpu.CompilerParams(dimension_semantics=("arbitrary",) * 3,
                                             vmem_limit_bytes=VMEM_LIMIT),
        name="attn_t_fox" if fox else "attn_t_mla",
    )(*args)


class _Item:
    def __init__(self, q, kb, qpar, kpar, first=False, last=False):
        self.q, self.kb, self.qpar, self.kpar, self.first, self.last = q, kb, qpar, kpar, first, last
        self.slot = 2 * qpar + kpar


def _attn_flat_kernel(*refs, tq, nq, fox):
    n_in = 4 if fox else 5
    q1_ref = refs[0]
    q2_ref = None if fox else refs[1]
    k1_ref, k2_ref, vt_ref, o_ref = refs[n_in - 3:n_in + 1]
    scratch = refs[n_in + 1:]
    qt_refs, m_refs, acc_refs = scratch[0:2], scratch[2:4], scratch[4:6]
    s_refs, p_refs, a_refs, bm_refs = scratch[6:10], scratch[10:14], scratch[14:18], scratch[18:22]
    tk = tq
    pair = pl.program_id(1)
    col_tiles = [slice(j, j + COL_TILE) for j in range(0, 2 * tq, COL_TILE)]
    ones_rows = jnp.ones((ONES_ROWS, tk), BF16)

    def rows_of(blk):
        start = blk * tq
        return pl.ds(start if isinstance(start, int) else pl.multiple_of(start, tq), tq)

    def setup(q, qpar):
        lane = lax.broadcasted_iota(jnp.int32, (tq, LANES), 1)
        q1 = q1_ref[rows_of(q), :].astype(F32)
        q2 = jnp.ones_like(q1) if fox else q2_ref[rows_of(q), :].astype(F32)

        def head_rows(a):
            main = jnp.where((lane >= a * HEAD_DIM) & (lane < (a + 1) * HEAD_DIM), q1, 0.0)
            if fox:
                lo, width = LOGF_LANE0 + 3 * (2 * pair + a), 3
            else:
                lo, width = ROPE_DIM * (2 * (pair % 2) + a), ROPE_DIM
            aux = jnp.where((lane >= lo) & (lane < lo + width), q2, 0.0)
            return jnp.concatenate([main, aux], axis=1)

        qt_refs[qpar][...] = jnp.concatenate([head_rows(0), head_rows(1)], axis=0).T.astype(BF16)

    def key_block(kb):
        return jnp.concatenate([k1_ref[rows_of(kb), :], k2_ref[rows_of(kb), :]], axis=1)

    def scores(x, cols):
        s = jnp.dot(key_block(x.kb), qt_refs[x.qpar][:, cols], preferred_element_type=F32)
        if x.last:
            c = (cols.start % tq) + lax.broadcasted_iota(jnp.int32, s.shape, 1)
            r = lax.broadcasted_iota(jnp.int32, s.shape, 0)
            vis = (r <= c) if fox else ((r // CHUNK) <= (c // CHUNK))
            s = jnp.where(vis, s, NEG)
        s_refs[x.slot][cols.start // COL_TILE] = s
        bm_refs[x.slot][:, cols] = jnp.max(s, axis=0, keepdims=True)

    def softmax(x, cols):
        bm = bm_refs[x.slot][:, cols]
        if x.first:
            m_new = bm
            a_refs[x.slot][:, cols] = jnp.zeros_like(bm)
        else:
            m_prev = m_refs[x.qpar][:, cols]
            m_new = jnp.maximum(m_prev, bm)
            a_refs[x.slot][:, cols] = jnp.exp2(m_prev - m_new)
        m_refs[x.qpar][:, cols] = m_new
        ct = cols.start // COL_TILE
        p_refs[x.slot][ct] = jnp.exp2(s_refs[x.slot][ct] - m_new).astype(BF16)

    def values(x, cols):
        head = cols.start // tq
        v_t = vt_ref[x.kb, head * HEAD_DIM:(head + 1) * HEAD_DIM, :]
        lhs = jnp.concatenate([v_t, ones_rows], axis=0)
        pv = jnp.dot(lhs, p_refs[x.slot][cols.start // COL_TILE], preferred_element_type=F32)[:ACC_ROWS]
        acc_ref = acc_refs[x.qpar]
        if x.first:
            acc_ref[:, cols] = pv
        else:
            acc_ref[:, cols] = a_refs[x.slot][:, cols] * acc_ref[:, cols] + pv

    def finalize(x):
        acc = acc_refs[x.qpar][...]
        out_t = jnp.concatenate([acc[:HEAD_DIM, :tq] / acc[HEAD_DIM:HEAD_DIM + 1, :tq],
                                 acc[:HEAD_DIM, tq:] / acc[HEAD_DIM:HEAD_DIM + 1, tq:]], axis=0)
        o_ref[rows_of(x.q), :] = out_t.T.astype(o_ref.dtype)

    def tick(xs, xm, xv, next_q=None):
        for cols in col_tiles:
            if xs is not None:
                scores(xs, cols)
            if xv is not None:
                values(xv, cols)
            if xm is not None:
                softmax(xm, cols)
        if xv is not None and xv.last:
            finalize(xv)
        if next_q is not None:
            setup(*next_q)

    def generic_pairs(q, qpar, count):
        def body(i, carry):
            kb = 3 + 2 * i
            x_a, x_b = _Item(q, kb, qpar, 1), _Item(q, kb + 1, qpar, 0)
            tick(x_a, _Item(q, kb - 1, qpar, 0), _Item(q, kb - 2, qpar, 1))
            tick(x_b, x_a, _Item(q, kb - 1, qpar, 0))
            return carry
        lax.fori_loop(0, count, body, 0)

    def query_block(q, qpar, is_last_q=False):
        p2 = _Item(q - 1, q - 2, 1 - qpar, qpar)
        p1 = _Item(q - 1, q - 1, 1 - qpar, 1 - qpar, last=True)
        x0 = _Item(q, 0, qpar, 0, first=True)
        x1 = _Item(q, 1, qpar, 1)
        x2 = _Item(q, 2, qpar, 0)
        tick(x0, p1, p2)
        tick(x1, x0, p1)
        tick(x2, x1, x0)
        nxt = None if is_last_q else (q + 1, 1 - qpar)
        if qpar == 1:
            generic_pairs(q, qpar, (q - 3) // 2)
            a2, a1 = _Item(q, q - 2, qpar, 1), _Item(q, q - 1, qpar, 0)
        else:
            generic_pairs(q, qpar, (q - 4) // 2)
            a1 = _Item(q, q - 1, qpar, 1)
            a2 = _Item(q, q - 2, qpar, 0)
            tick(a1, a2, _Item(q, q - 3, qpar, 1))
        xl = _Item(q, q, qpar, qpar, last=True)
        tick(xl, a1, a2, next_q=nxt)
        return a1, xl

    setup(0, 0)
    x00 = _Item(0, 0, 0, 0, first=True, last=True)
    x10, x11 = _Item(1, 0, 1, 0, first=True), _Item(1, 1, 1, 1, last=True)
    x20, x21, x22 = _Item(2, 0, 0, 0, first=True), _Item(2, 1, 0, 1), _Item(2, 2, 0, 0, last=True)
    tick(x00, None, None, next_q=(1, 1))
    tick(x10, x00, None)
    tick(x11, x10, x00, next_q=(2, 0))
    tick(x20, x11, x10)
    tick(x21, x20, x11)
    tick(x22, x21, x20, next_q=(3, 1))

    def block_pair(j, carry):
        query_block(3 + 2 * j, 1)
        query_block(4 + 2 * j, 0)
        return carry

    lax.fori_loop(0, (nq - 4) // 2, block_pair, 0)
    a1, xl = query_block(nq - 1, 1, is_last_q=True)
    tick(None, xl, a1)
    tick(None, None, xl)


def _attn_flat(q1, q2, k1, k2, v_t, *, tq, fox):
    pairs, b, t, _ = q1.shape
    hw = pairs * LANES
    nq = t // tq
    assert t % tq == 0 and nq % 2 == 0 and nq >= 4, (t, tq)
    assert v_t.shape == (b, nq, hw, tq), v_t.shape
    m2 = 2 * tq
    spec = pl.BlockSpec((None, None, t, LANES), lambda bi, p: (p, bi, 0, 0))
    k2spec = pl.BlockSpec((None, t, LANES), lambda bi, p: (bi, 0, 0))
    vspec = pl.BlockSpec((None, nq, LANES, tq), lambda bi, p: (bi, 0, p, 0))
    if fox:
        in_specs = [spec, spec, k2spec, vspec]
        args = (q1, k1, k2, v_t)
    else:
        q2spec = pl.BlockSpec((None, None, t, LANES), lambda bi, p: (p // 2, bi, 0, 0))
        in_specs = [spec, q2spec, spec, k2spec, vspec]
        args = (q1, q2, k1, k2, v_t)
    vmem = lambda shape, dt, n: [pltpu.VMEM(shape, dt) for _ in range(n)]
    return pl.pallas_call(
        functools.partial(_attn_flat_kernel, tq=tq, nq=nq, fox=fox),
        out_shape=jax.ShapeDtypeStruct((pairs, b, t, LANES), BF16),
        grid=(b, pairs),
        in_specs=in_specs,
        out_specs=spec,
        scratch_shapes=(vmem((2 * LANES, m2), BF16, 2)
                        + vmem((1, m2), F32, 2)
                        + vmem((ACC_ROWS, m2), F32, 2)
                        + vmem((m2 // COL_TILE, tq, COL_TILE), F32, 4)
                        + vmem((m2 // COL_TILE, tq, COL_TILE), BF16, 4)
                        + vmem((1, m2), F32, 4)
                        + vmem((1, m2), F32, 4)),
        compiler_params=pltpu.CompilerParams(dimension_semantics=("arbitrary",) * 2,
                                             vmem_limit_bytes=VMEM_LIMIT),
        name="attn_flat_fox" if fox else "attn_flat_mla",
    )(*args)


def _rope_tables(pos):
    half = ROPE_DIM // 2
    freqs = ROPE_THETA ** (-jnp.arange(half, dtype=F32) / half)
    ang = pos.astype(F32)[:, None] * freqs[None, :]
    cos, sin = jnp.cos(ang), jnp.sin(ang)
    cos_t = jnp.tile(jnp.concatenate([cos, cos], axis=1), (1, HEADS))
    sin_t = jnp.tile(jnp.concatenate([-sin, sin], axis=1), (1, HEADS))
    return cos_t, sin_t


def _swap_halves(w):
    half = w.shape[-1] // 2
    return jnp.concatenate([w[..., half:], w[..., :half]], axis=-1)


def _prep_weights(w_in, b_forget, w_q_up, w_kv_up, w_out):
    d = w_in.shape[0]
    hw = HEADS * HEAD_DIM
    o = 0
    cq = w_in[:, o:o + Q_LORA]; o += Q_LORA
    ckv = w_in[:, o:o + KV_LORA]; o += KV_LORA
    kr = w_in[:, o:o + ROPE_DIM]; o += ROPE_DIM
    fq = w_in[:, o:o + hw]; o += hw
    fk = w_in[:, o:o + hw]; o += hw
    fv = w_in[:, o:o + hw]; o += hw
    gate = w_in[:, o:o + HEADS]
    reps = LANES // ROPE_DIM
    pad = LANES - 4 * HEADS
    gate128 = jnp.concatenate([gate, jnp.repeat(gate, 3, axis=1), jnp.zeros((d, pad), w_in.dtype)], axis=1)
    w_in_p = jnp.concatenate([cq, ckv, fq, fk, fv, jnp.tile(kr, (1, reps)), jnp.tile(_swap_halves(kr), (1, reps)),
                              gate128], axis=1).astype(BF16)
    bias128 = jnp.concatenate([b_forget, jnp.repeat(b_forget, 3), jnp.zeros((pad,), F32)])[None, :]
    wq = w_q_up.reshape(Q_LORA, HEADS, NOPE_DIM + ROPE_DIM)
    wq_rope = wq[:, :, NOPE_DIM:]
    w_q_p = jnp.concatenate([wq[:, :, :NOPE_DIM].reshape(Q_LORA, -1), wq_rope.reshape(Q_LORA, -1),
                             _swap_halves(wq_rope).reshape(Q_LORA, -1)], axis=1).astype(BF16)
    wkv = w_kv_up.reshape(KV_LORA, HEADS, NOPE_DIM + HEAD_DIM)
    w_kv_p = jnp.concatenate([wkv[:, :, :NOPE_DIM].reshape(KV_LORA, -1), wkv[:, :, NOPE_DIM:].reshape(KV_LORA, -1)],
                             axis=1).astype(BF16)
    w_o1 = w_out[:hw].astype(BF16)
    w_o2 = w_out[hw:].astype(BF16)
    return w_in_p, bias128, w_q_p, w_kv_p, w_o1, w_o2


def _expand_logf(lf):
    pad = LANES - 4 * HEADS
    return jnp.concatenate([lf, jnp.repeat(lf, 3, axis=-1), jnp.zeros(lf.shape[:-1] + (pad,), lf.dtype)], axis=-1)


def kernel(x_prompt, x_sample, cache_mla_ckv, cache_mla_krope, cache_fox_k, cache_fox_v, cache_fox_logf,
           g_ffn1_pre, g_ffn1_post, w_ffn1_gu, w_ffn1_down, g_mix_pre, g_mix_post, w_in, b_forget,
           g_q_latent, w_q_up, g_kv_latent, w_kv_up, w_out, g_ffn2_pre, g_ffn2_post, w_ffn2_gu, w_ffn2_down):
    depth = w_in.shape[0]
    bp, tp, d = x_prompt.shape
    bs, ts, _ = x_sample.shape
    past = cache_mla_ckv.shape[2]
    hw = HEADS * HEAD_DIM
    tq_p = 512
    tc_s = 512
    tk_pad = -(-(past + ts) // tc_s) * tc_s

    cos_p, sin_p = _rope_tables(jnp.arange(tp))
    cos_s, sin_s = _rope_tables(past + jnp.arange(ts))
    cos_s, sin_s = jnp.tile(cos_s, (bs, 1)), jnp.tile(sin_s, (bs, 1))

    xp = x_prompt.reshape(bp * tp, d)
    xs = x_sample.reshape(bs * ts, d)
    tm_p = 512
    tm_s = bs * ts
    rows_p, rows_s = [], []

    def pad_keys(parts):
        n = sum(a.shape[1] for a in parts)
        parts = list(parts) + [jnp.zeros((bs, tk_pad - n, parts[0].shape[2]), parts[0].dtype)]
        return jnp.concatenate(parts, axis=1)

    for l in range(depth):
        d_ff = w_ffn1_down.shape[1]
        w1 = (w_ffn1_gu[l][:, :d_ff].astype(BF16), w_ffn1_gu[l][:, d_ff:].astype(BF16), w_ffn1_down[l].astype(BF16))
        w2 = (w_ffn2_gu[l][:, :d_ff].astype(BF16), w_ffn2_gu[l][:, d_ff:].astype(BF16), w_ffn2_down[l].astype(BF16))
        w_in_p, bias128, w_q_p, w_kv_p, w_o1, w_o2 = _prep_weights(w_in[l], b_forget[l], w_q_up[l], w_kv_up[l], w_out[l])
        g1 = (g_ffn1_pre[l][None, :], g_ffn1_post[l][None, :])
        g2 = (g_ffn2_pre[l][None, :], g_ffn2_post[l][None, :])
        gm_pre, gm_post = g_mix_pre[l][None, :], g_mix_post[l][None, :]
        gq, gkv = g_q_latent[l][None, :], g_kv_latent[l][None, :]

        hp = _ffn(xp, *g1, *w1, tm=tm_p)
        (ckv, krope, fk, fv, logf, qn, qr, kn, vm_t, kr4, fqb, fkb, fv_t, ls) = _proj(
            hp, cos_p, sin_p, gm_pre, w_in_p, bias128, gq, w_q_p, gkv, w_kv_p, w_kv_p[:, hw:].T, tm=tq_p)
        sh = lambda a: a.reshape(bp, tp, a.shape[-1])
        sh_g = lambda a: a.reshape(a.shape[0], bp, tp, LANES)
        sh_t = lambda a: a.reshape(bp, tp // tq_p, hw, tq_p)
        o_mla = _attn_flat(sh_g(qn), sh_g(qr), sh_g(kn), sh(kr4), sh_t(vm_t), tq=tq_p, fox=False)
        o_fox = _attn_flat(sh_g(fqb), None, sh_g(fkb), sh(ls), sh_t(fv_t), tq=tq_p, fox=True)
        un_g = lambda o: o.reshape(o.shape[0], bp * tp, LANES)
        xp = _ffn(hp, *g2, *w2, mix=(un_g(o_mla), un_g(o_fox), w_o1, w_o2, gm_post), tm=tm_p)
        rows_p.append((ckv.reshape(bp, tp, KV_LORA), krope.reshape(bp, tp, ROPE_DIM),
                       fk.reshape(bp, tp, HEADS, HEAD_DIM), fv.reshape(bp, tp, HEADS, HEAD_DIM),
                       logf.reshape(bp, tp, HEADS)))

        hs = _ffn(xs, *g1, *w1, tm=tm_s)
        (ckv, krope, fk, fv, logf, qn, qr, kn, vm, kr4, fqb, _, _, lf128) = _proj(
            hs, cos_s, sin_s, gm_pre, w_in_p, bias128, gq, w_q_p, gkv, w_kv_p, tm=tm_s)
        kn_past, vm_past, kr4_past = _pastkv(cache_mla_ckv[l].reshape(bs * past, KV_LORA),
                                             cache_mla_krope[l].reshape(bs * past, ROPE_DIM), w_kv_p, tm=512)
        sh = lambda a: a.reshape(bs, -1, a.shape[-1])
        lf_all = pad_keys([_expand_logf(cache_fox_logf[l].astype(F32)), sh(lf128)])
        ls = _lsplit(lf_all, tc=tc_s)
        o_mla = _attn_sample(sh(qn), sh(qr), sh(kn_past), sh(kr4_past), sh(vm_past),
                             sh(kn), sh(kr4), sh(vm), fox=False)
        o_fox = _attn_sample(sh(fqb), None, cache_fox_k[l].reshape(bs, past, hw), ls,
                             cache_fox_v[l].reshape(bs, past, hw), sh(fk), ls, sh(fv),
                             k2n_row0=past, fox=True)
        xs = _ffn(hs, *g2, *w2, mix=(o_mla.reshape(-1, hw), o_fox.reshape(-1, hw), w_o1, w_o2, gm_post), tm=tm_s)
        rows_s.append((ckv.reshape(bs, ts, KV_LORA), krope.reshape(bs, ts, ROPE_DIM),
                       fk.reshape(bs, ts, HEADS, HEAD_DIM), fv.reshape(bs, ts, HEADS, HEAD_DIM),
                       logf.reshape(bs, ts, HEADS)))

    outs_p = [jnp.stack([r[i] for r in rows_p]) for i in range(5)]
    outs_s = [jnp.stack([r[i] for r in rows_s]) for i in range(5)]
    return (xp.reshape(bp, tp, d), xs.reshape(bs, ts, d), *outs_p, *outs_s)
```

```python
import functools

import jax
import jax.numpy as jnp
from jax import lax
from jax.experimental import pallas as pl
from jax.experimental.pallas import tpu as pltpu

EPS = 1e-6
CHUNK = 64
ROPE_THETA = 10000.0
HEADS = 8
NOPE_DIM = 64
ROPE_DIM = 32
HEAD_DIM = 64
Q_LORA = 384
KV_LORA = 256
LANES = 128
COL_TILE = 256
ROW_TILE = 512
ONES_ROWS = 16
ACC_ROWS = HEAD_DIM + 8
LOG2E = 1.4426950408889634
NEG = -1e30
LOGF_LANE0 = 8
VMEM_LIMIT = 56 * 1024 * 1024

BF16 = jnp.bfloat16
F32 = jnp.float32


def _rms(x, g):
    ms = jnp.mean(x * x, axis=-1, keepdims=True)
    return x * lax.rsqrt(ms + EPS) * g


def _const_spec(shape):
    return pl.BlockSpec(shape, lambda *_: (0,) * len(shape), pipeline_mode=pl.Buffered(1))


def _ffn_kernel(*refs, ff_chunk, with_mix):
    if with_mix:
        (h_ref, o1_ref, o2_ref, wo1_ref, wo2_ref, gmix_ref,
         gpre_ref, gpost_ref, wg_ref, wu_ref, wd_ref, out_ref) = refs
        def rows(o_ref):
            if len(o_ref.shape) == 2:
                return o_ref[...]
            return jnp.concatenate([o_ref[g] for g in range(o_ref.shape[0])], axis=1)
        mix = jnp.dot(rows(o1_ref), wo1_ref[...], preferred_element_type=F32)
        mix = mix + jnp.dot(rows(o2_ref), wo2_ref[...], preferred_element_type=F32)
        x = h_ref[...] + _rms(mix, gmix_ref[...])
    else:
        x_ref, gpre_ref, gpost_ref, wg_ref, wu_ref, wd_ref, out_ref = refs
        x = x_ref[...]
    n = _rms(x, gpre_ref[...]).astype(BF16)
    d_ff = wg_ref.shape[1]
    acc = jnp.zeros(x.shape, F32)
    for c in range(d_ff // ff_chunk):
        cols = slice(c * ff_chunk, (c + 1) * ff_chunk)
        gate = jnp.dot(n, wg_ref[:, cols], preferred_element_type=F32)
        up = jnp.dot(n, wu_ref[:, cols], preferred_element_type=F32)
        act = (gate * jax.nn.sigmoid(gate) * up).astype(BF16)
        acc = acc + jnp.dot(act, wd_ref[cols, :], preferred_element_type=F32)
    out_ref[...] = x + 0.5 * _rms(acc, gpost_ref[...])


def _ffn(x, g_pre, g_post, w_g, w_u, w_d, mix=None, *, tm):
    n, d = x.shape
    d_ff = w_g.shape[1]
    row = lambda w: pl.BlockSpec((tm, w), lambda i: (i, 0))
    in_specs = [row(d)]
    args = [x]
    if mix is not None:
        o1, o2, wo1, wo2, g_mix = mix
        grouped = lambda o: pl.BlockSpec((o.shape[0], tm, LANES), lambda i: (0, i, 0))
        in_specs += [row(o.shape[1]) if o.ndim == 2 else grouped(o) for o in (o1, o2)]
        in_specs += [_const_spec(wo1.shape), _const_spec(wo2.shape), _const_spec((1, d))]
        args += [o1, o2, wo1, wo2, g_mix]
    in_specs += [_const_spec((1, d)), _const_spec((1, d)), _const_spec(w_g.shape), _const_spec(w_u.shape),
                 _const_spec(w_d.shape)]
    args += [g_pre, g_post, w_g, w_u, w_d]
    return pl.pallas_call(
        functools.partial(_ffn_kernel, ff_chunk=COL_TILE, with_mix=mix is not None),
        out_shape=jax.ShapeDtypeStruct((n, d), F32),
        grid=(n // tm,),
        in_specs=in_specs,
        out_specs=row(d),
        compiler_params=pltpu.CompilerParams(dimension_semantics=("arbitrary",), vmem_limit_bytes=VMEM_LIMIT),
        name="ffn_mix" if mix is not None else "ffn",
    )(*args)


_C_CQ = 0
_C_CKV = _C_CQ + Q_LORA
_C_FQ = _C_CKV + KV_LORA
_C_FK = _C_FQ + HEADS * HEAD_DIM
_C_FV = _C_FK + HEADS * HEAD_DIM
_C_KR = _C_FV + HEADS * HEAD_DIM
_C_KRS = _C_KR + LANES
_C_GATE = _C_KRS + LANES
_C_END = _C_GATE + LANES


def _log_sigmoid(x):
    return jnp.minimum(x, 0.0) - jnp.log(1.0 + jnp.exp(-jnp.abs(x)))


def _proj_kernel(*refs, values_t, stream_blocks):
    (h_ref, cos_ref, sin_ref, gpre_ref, win_ref, bias_ref, gq_ref, wq_ref, gkv_ref, wkv_ref) = refs[:10]
    n_in = 11 if values_t else 10
    (ckv_ref, krope_ref, fk_ref, fv_ref, logf_ref,
     qn_ref, qr_ref, kn_ref, vm_ref, kr4_ref, fqb_ref, fkb_ref, fvb_ref, lf128_ref) = refs[n_in:n_in + 14]
    nt_dims = (((1,), (1,)), ((), ()))
    hw = HEADS * HEAD_DIM
    u = _rms(h_ref[...], gpre_ref[...]).astype(BF16)
    proj = jnp.dot(u, win_ref[...], preferred_element_type=F32)
    cos = cos_ref[...]
    sin = sin_ref[...]

    cq = _rms(proj[:, _C_CQ:_C_CKV], gq_ref[...]).astype(BF16)
    q = jnp.dot(cq, wq_ref[...], preferred_element_type=F32)
    q_scale = (NOPE_DIM + ROPE_DIM) ** -0.5 * LOG2E
    rw = HEADS * ROPE_DIM
    def put(ref, x):
        if values_t:
            for g in range(x.shape[1] // LANES):
                ref[g] = x[:, g * LANES:(g + 1) * LANES]
        else:
            ref[...] = x

    put(qn_ref, (q[:, :hw] * q_scale).astype(BF16))
    put(qr_ref, ((q[:, hw:hw + rw] * cos + q[:, hw + rw:] * sin) * q_scale).astype(BF16))

    ckv = _rms(proj[:, _C_CKV:_C_FQ], gkv_ref[...])
    ckv_ref[...] = ckv
    ckv_b = ckv.astype(BF16)
    if values_t:
        put(kn_ref, jnp.dot(ckv_b, wkv_ref[:, :hw], preferred_element_type=F32).astype(BF16))
        vm_ref[0] = lax.dot_general(refs[10][...], ckv_b, nt_dims, preferred_element_type=F32).astype(BF16)
    else:
        kv = jnp.dot(ckv_b, wkv_ref[...], preferred_element_type=F32)
        kn_ref[...] = kv[:, :hw].astype(BF16)
        vm_ref[...] = kv[:, hw:].astype(BF16)
    kr4 = proj[:, _C_KR:_C_KRS] * cos[:, :LANES] + proj[:, _C_KRS:_C_GATE] * sin[:, :LANES]
    krope_ref[...] = kr4[:, :ROPE_DIM]
    kr4_ref[...] = kr4.astype(BF16)

    fq = proj[:, _C_FQ:_C_FK]
    fk = proj[:, _C_FK:_C_FV]
    fv = proj[:, _C_FV:_C_KR]
    put(fqb_ref, (fq * (HEAD_DIM ** -0.5 * LOG2E)).astype(BF16))
    if values_t:
        def by_head(x):
            heads = jnp.stack([x[:, h * HEAD_DIM:(h + 1) * HEAD_DIM] for h in range(HEADS)], axis=0)
            return jnp.swapaxes(heads, 0, 1)
        fk_ref[...] = by_head(fk)
        fv_ref[...] = by_head(fv)
    else:
        fk_ref[...] = fk
        fv_ref[...] = fv
    put(fkb_ref, fk.astype(BF16))
    if values_t:
        fvb_ref[0] = fv.T.astype(BF16)
    else:
        fvb_ref[...] = fv.astype(BF16)
    logf = _log_sigmoid(proj[:, _C_GATE:_C_END] + bias_ref[...])
    logf_ref[...] = logf[:, :HEADS]
    if values_t:
        carry_ref = refs[-1]

        @pl.when(pl.program_id(0) % stream_blocks == 0)
        def _():
            carry_ref[...] = jnp.zeros_like(carry_ref)

        lf128_ref[...] = _cumsum_split(logf, carry_ref)
    else:
        lf128_ref[...] = logf


def _proj(h, cos, sin, g_pre, w_in, bias, g_q, w_q, g_kv, w_kv, w_v_t=None, *, tm):
    n, d = h.shape
    t_blocks = cos.shape[0] // tm
    hw = HEADS * HEAD_DIM
    values_t = w_v_t is not None
    row = lambda w: pl.BlockSpec((tm, w), lambda i: (i, 0))
    tab = pl.BlockSpec((tm, HEADS * ROPE_DIM), lambda i: (i % t_blocks, 0))
    widths = [(KV_LORA, F32), (ROPE_DIM, F32), (hw, F32), (hw, F32), (HEADS, F32),
              (hw, BF16), (HEADS * ROPE_DIM, BF16), (hw, BF16), (hw, BF16), (LANES, BF16),
              (hw, BF16), (hw, BF16), (hw, BF16), (LANES, F32)]
    out_shape = [jax.ShapeDtypeStruct((n, w), dt) for w, dt in widths]
    out_specs = [row(w) for w, _ in widths]
    in_specs = [row(d), tab, tab, _const_spec((1, d)), _const_spec(w_in.shape), _const_spec((1, LANES)),
                _const_spec((1, Q_LORA)), _const_spec(w_q.shape), _const_spec((1, KV_LORA)),
                _const_spec(w_kv.shape)]
    args = [h, cos, sin, g_pre, w_in, bias, g_q, w_q, g_kv, w_kv]
    if values_t:
        for i in (8, 12):
            out_shape[i] = jax.ShapeDtypeStruct((n // tm, hw, tm), BF16)
            out_specs[i] = pl.BlockSpec((1, hw, tm), lambda i: (i, 0, 0))
        out_shape[13] = jax.ShapeDtypeStruct((n, LANES), BF16)
        for i in (5, 6, 7, 10, 11):
            groups = widths[i][0] // LANES
            out_shape[i] = jax.ShapeDtypeStruct((groups, n, LANES), BF16)
            out_specs[i] = pl.BlockSpec((groups, tm, LANES), lambda i: (0, i, 0))
        for i in (2, 3):
            out_shape[i] = jax.ShapeDtypeStruct((n, HEADS, HEAD_DIM), F32)
            out_specs[i] = pl.BlockSpec((tm, HEADS, HEAD_DIM), lambda i: (i, 0, 0))
        in_specs.append(_const_spec(w_v_t.shape))
        args.append(w_v_t)
    return pl.pallas_call(
        functools.partial(_proj_kernel, values_t=values_t, stream_blocks=t_blocks),
        out_shape=out_shape,
        grid=(n // tm,),
        in_specs=in_specs,
        out_specs=out_specs,
        scratch_shapes=[pltpu.VMEM((1, LANES), F32)] if values_t else [],
        compiler_params=pltpu.CompilerParams(dimension_semantics=("arbitrary",), vmem_limit_bytes=VMEM_LIMIT),
        name="proj",
    )(*args)


def _pastkv_kernel(ckv_ref, kr_ref, wkv_ref, kn_ref, vm_ref, kr4_ref):
    hw = HEADS * HEAD_DIM
    kv = jnp.dot(ckv_ref[...].astype(BF16), wkv_ref[...], preferred_element_type=F32)
    kn_ref[...] = kv[:, :hw].astype(BF16)
    vm_ref[...] = kv[:, hw:].astype(BF16)
    src = lax.broadcasted_iota(jnp.int32, (ROPE_DIM, LANES), 0)
    dst = lax.broadcasted_iota(jnp.int32, (ROPE_DIM, LANES), 1)
    rep = (dst % ROPE_DIM == src).astype(BF16)
    kr4_ref[...] = jnp.dot(kr_ref[...].astype(BF16), rep, preferred_element_type=F32).astype(BF16)


def _pastkv(ckv, krope, w_kv, *, tm):
    n = ckv.shape[0]
    hw = HEADS * HEAD_DIM
    row = lambda w: pl.BlockSpec((tm, w), lambda i: (i, 0))
    return pl.pallas_call(
        _pastkv_kernel,
        out_shape=[jax.ShapeDtypeStruct((n, hw), BF16), jax.ShapeDtypeStruct((n, hw), BF16),
                   jax.ShapeDtypeStruct((n, LANES), BF16)],
        grid=(n // tm,),
        in_specs=[row(KV_LORA), row(ROPE_DIM), _const_spec(w_kv.shape)],
        out_specs=[row(hw), row(hw), row(LANES)],
        compiler_params=pltpu.CompilerParams(dimension_semantics=("arbitrary",), vmem_limit_bytes=VMEM_LIMIT),
        name="pastkv",
    )(ckv, krope, w_kv)


def _split3(y, lane):
    hi = y.astype(BF16).astype(F32)
    r1 = y - hi
    mid = r1.astype(BF16).astype(F32)
    lo = r1 - mid
    j = (lane - LOGF_LANE0) % 3
    sel = jnp.where(j == 0, hi, jnp.where(j == 1, mid, lo))
    used = (lane >= LOGF_LANE0) & (lane < LOGF_LANE0 + 3 * HEADS)
    return jnp.where(used, sel, 0.0).astype(BF16)


def _cumsum_split(x, carry_ref):
    tc = x.shape[0]
    hi = x.astype(BF16)
    r1 = x - hi.astype(F32)
    mid = r1.astype(BF16)
    lo = (r1 - mid.astype(F32)).astype(BF16)
    r = lax.broadcasted_iota(jnp.int32, (tc, tc), 0)
    c = lax.broadcasted_iota(jnp.int32, (tc, tc), 1)
    tri = (c <= r).astype(BF16)
    cum = (jnp.dot(tri, hi, preferred_element_type=F32) + jnp.dot(tri, mid, preferred_element_type=F32)
           + jnp.dot(tri, lo, preferred_element_type=F32)) + carry_ref[...]
    carry_ref[...] = cum[tc - 1:tc, :]
    lane = lax.broadcasted_iota(jnp.int32, (tc, LANES), 1)
    return _split3(cum * (-LOG2E), lane)


def _lsplit_kernel(x_ref, o_ref, carry_ref):
    @pl.when(pl.program_id(1) == 0)
    def _():
        carry_ref[...] = jnp.zeros_like(carry_ref)

    o_ref[0] = _cumsum_split(x_ref[0], carry_ref)


def _lsplit(lf128, *, tc):
    b, t, _ = lf128.shape
    spec = pl.BlockSpec((1, tc, LANES), lambda i, j: (i, j, 0))
    return pl.pallas_call(
        _lsplit_kernel,
        out_shape=jax.ShapeDtypeStruct((b, t, LANES), BF16),
        grid=(b, t // tc),
        in_specs=[spec],
        out_specs=spec,
        scratch_shapes=[pltpu.VMEM((1, LANES), F32)],
        compiler_params=pltpu.CompilerParams(dimension_semantics=("arbitrary", "arbitrary")),
        name="lsplit",
    )(lf128)


def _attn_sample_kernel(*refs, past, fox):
    n_q = 1 if fox else 2
    q1_ref = refs[0]
    q2_ref = None if fox else refs[1]
    k1p_ref, k2p_ref, vp_ref, k1n_ref, k2n_ref, vn_ref, o_ref = refs[n_q:]
    ts = q1_ref.shape[1]
    p = pl.program_id(1)
    lane = lax.broadcasted_iota(jnp.int32, (ts, LANES), 1)
    q1 = q1_ref[0].astype(F32)
    q2 = jnp.ones_like(q1) if fox else q2_ref[0].astype(F32)

    def head_rows(a):
        main = jnp.where((lane >= a * HEAD_DIM) & (lane < (a + 1) * HEAD_DIM), q1, 0.0)
        if fox:
            lo, width = LOGF_LANE0 + 3 * (2 * p + a), 3
        else:
            lo, width = ROPE_DIM * (2 * (p % 2) + a), ROPE_DIM
        aux = jnp.where((lane >= lo) & (lane < lo + width), q2, 0.0)
        return jnp.concatenate([main, aux], axis=1).astype(BF16)

    q = jnp.concatenate([head_rows(0), head_rows(1)], axis=0)
    nt_dims = (((1,), (1,)), ((), ()))

    def logits(k1_ref, k2_ref):
        k = jnp.concatenate([k1_ref[0].astype(BF16), k2_ref[0].astype(BF16)], axis=1)
        return lax.dot_general(q, k, nt_dims, preferred_element_type=F32)

    s_p = logits(k1p_ref, k2p_ref)
    s_n = logits(k1n_ref, k2n_ref)
    r = lax.broadcasted_iota(jnp.int32, s_n.shape, 0)
    t_pos = past + jnp.where(r >= ts, r - ts, r)
    s_pos = past + lax.broadcasted_iota(jnp.int32, s_n.shape, 1)
    vis = (s_pos <= t_pos) if fox else ((s_pos // CHUNK) <= (t_pos // CHUNK))
    s_n = jnp.where(vis, s_n, NEG)
    m = jnp.maximum(jnp.max(s_p, axis=1, keepdims=True), jnp.max(s_n, axis=1, keepdims=True))
    p_p = jnp.exp2(s_p - m)
    p_n = jnp.exp2(s_n - m)
    denom = jnp.sum(p_p, axis=1, keepdims=True) + jnp.sum(p_n, axis=1, keepdims=True)
    out = (jnp.dot(p_p.astype(BF16), vp_ref[0].astype(BF16), preferred_element_type=F32)
           + jnp.dot(p_n.astype(BF16), vn_ref[0].astype(BF16), preferred_element_type=F32)) / denom
    o_ref[0] = jnp.where(lane < HEAD_DIM, out[:ts], out[ts:]).astype(o_ref.dtype)


def _attn_sample(q1, q2, k1p, k2p, vp, k1n, k2n, vn, *, k2n_row0=0, fox):
    b, ts, hw = q1.shape
    past = k1p.shape[1]
    assert k2n_row0 % ts == 0
    pairs = hw // LANES
    qspec = pl.BlockSpec((1, ts, LANES), lambda bi, p: (bi, 0, p))
    pspec = pl.BlockSpec((1, past, LANES), lambda bi, p: (bi, 0, p))
    p2spec = pl.BlockSpec((1, past, LANES), lambda bi, p: (bi, 0, 0))
    n2spec = pl.BlockSpec((1, ts, LANES), lambda bi, p: (bi, k2n_row0 // ts, 0))
    in_specs = [pspec, p2spec, pspec, qspec, n2spec, qspec]
    args = (k1p, k2p, vp, k1n, k2n, vn)
    if fox:
        in_specs, args = [qspec] + in_specs, (q1,) + args
    else:
        q2spec = pl.BlockSpec((1, ts, LANES), lambda bi, p: (bi, 0, p // 2))
        in_specs, args = [qspec, q2spec] + in_specs, (q1, q2) + args
    return pl.pallas_call(
        functools.partial(_attn_sample_kernel, past=past, fox=fox),
        out_shape=jax.ShapeDtypeStruct((b, ts, hw), BF16),
        grid=(b, pairs),
        in_specs=in_specs,
        out_specs=qspec,
        compiler_params=pltpu.CompilerParams(dimension_semantics=("arbitrary",) * 2,
                                             vmem_limit_bytes=VMEM_LIMIT),
        name="attn_sample_fox" if fox else "attn_sample_mla",
    )(*args)


class _Item:
    def __init__(self, q, kb, qpar, kpar, first=False, last=False):
        self.q, self.kb, self.qpar, self.kpar, self.first, self.last = q, kb, qpar, kpar, first, last
        self.slot = 2 * qpar + kpar


def _attn_flat_kernel(*refs, tq, nq, fox):
    n_in = 4 if fox else 5
    q1_ref = refs[0]
    q2_ref = None if fox else refs[1]
    k1_ref, k2_ref, vt_ref, o_ref = refs[n_in - 3:n_in + 1]
    scratch = refs[n_in + 1:]
    qt_refs, m_refs, acc_refs = scratch[0:2], scratch[2:4], scratch[4:6]
    s_refs, p_refs, a_refs, bm_refs = scratch[6:10], scratch[10:14], scratch[14:18], scratch[18:22]
    tk = tq
    pair = pl.program_id(1)
    col_tiles = [slice(j, j + COL_TILE) for j in range(0, 2 * tq, COL_TILE)]
    ones_rows = jnp.ones((ONES_ROWS, tk), BF16)

    def rows_of(blk):
        start = blk * tq
        return pl.ds(start if isinstance(start, int) else pl.multiple_of(start, tq), tq)

    def setup(q, qpar):
        lane = lax.broadcasted_iota(jnp.int32, (tq, LANES), 1)
        q1 = q1_ref[rows_of(q), :].astype(F32)
        q2 = jnp.ones_like(q1) if fox else q2_ref[rows_of(q), :].astype(F32)

        def head_rows(a):
            main = jnp.where((lane >= a * HEAD_DIM) & (lane < (a + 1) * HEAD_DIM), q1, 0.0)
            if fox:
                lo, width = LOGF_LANE0 + 3 * (2 * pair + a), 3
            else:
                lo, width = ROPE_DIM * (2 * (pair % 2) + a), ROPE_DIM
            aux = jnp.where((lane >= lo) & (lane < lo + width), q2, 0.0)
            return jnp.concatenate([main, aux], axis=1)

        qt_refs[qpar][...] = jnp.concatenate([head_rows(0), head_rows(1)], axis=0).T.astype(BF16)

    def key_block(kb):
        return jnp.concatenate([k1_ref[rows_of(kb), :], k2_ref[rows_of(kb), :]], axis=1)

    def scores(x, cols):
        s = jnp.dot(key_block(x.kb), qt_refs[x.qpar][:, cols], preferred_element_type=F32)
        if x.last:
            c = (cols.start % tq) + lax.broadcasted_iota(jnp.int32, s.shape, 1)
            r = lax.broadcasted_iota(jnp.int32, s.shape, 0)
            vis = (r <= c) if fox else ((r // CHUNK) <= (c // CHUNK))
            s = jnp.where(vis, s, NEG)
        s_refs[x.slot][cols.start // COL_TILE] = s
        bm_refs[x.slot][:, cols] = jnp.max(s, axis=0, keepdims=True)

    def softmax(x, cols):
        bm = bm_refs[x.slot][:, cols]
        if x.first:
            m_new = bm
            a_refs[x.slot][:, cols] = jnp.zeros_like(bm)
        else:
            m_prev = m_refs[x.qpar][:, cols]
            m_new = jnp.maximum(m_prev, bm)
            a_refs[x.slot][:, cols] = jnp.exp2(m_prev - m_new)
        m_refs[x.qpar][:, cols] = m_new
        ct = cols.start // COL_TILE
        p_refs[x.slot][ct] = jnp.exp2(s_refs[x.slot][ct] - m_new).astype(BF16)

    def values(x, cols):
        head = cols.start // tq
        v_t = vt_ref[x.kb, head * HEAD_DIM:(head + 1) * HEAD_DIM, :]
        lhs = jnp.concatenate([v_t, ones_rows], axis=0)
        pv = jnp.dot(lhs, p_refs[x.slot][cols.start // COL_TILE], preferred_element_type=F32)[:ACC_ROWS]
        acc_ref = acc_refs[x.qpar]
        if x.first:
            acc_ref[:, cols] = pv
        else:
            acc_ref[:, cols] = a_refs[x.slot][:, cols] * acc_ref[:, cols] + pv

    def finalize(x):
        acc = acc_refs[x.qpar][...]
        out_t = jnp.concatenate([acc[:HEAD_DIM, :tq] / acc[HEAD_DIM:HEAD_DIM + 1, :tq],
                                 acc[:HEAD_DIM, tq:] / acc[HEAD_DIM:HEAD_DIM + 1, tq:]], axis=0)
        o_ref[rows_of(x.q), :] = out_t.T.astype(o_ref.dtype)

    def tick(xs, xm, xv, next_q=None):
        for cols in col_tiles:
            if xs is not None:
                scores(xs, cols)
            if xv is not None:
                values(xv, cols)
            if xm is not None:
                softmax(xm, cols)
        if xv is not None and xv.last:
            finalize(xv)
        if next_q is not None:
            setup(*next_q)

    def generic_pairs(q, qpar, count):
        def body(i, carry):
            kb = 3 + 2 * i
            x_a, x_b = _Item(q, kb, qpar, 1), _Item(q, kb + 1, qpar, 0)
            tick(x_a, _Item(q, kb - 1, qpar, 0), _Item(q, kb - 2, qpar, 1))
            tick(x_b, x_a, _Item(q, kb - 1, qpar, 0))
            return carry
        lax.fori_loop(0, count, body, 0)

    def query_block(q, qpar, is_last_q=False):
        p2 = _Item(q - 1, q - 2, 1 - qpar, qpar)
        p1 = _Item(q - 1, q - 1, 1 - qpar, 1 - qpar, last=True)
        x0 = _Item(q, 0, qpar, 0, first=True)
        x1 = _Item(q, 1, qpar, 1)
        x2 = _Item(q, 2, qpar, 0)
        tick(x0, p1, p2)
        tick(x1, x0, p1)
        tick(x2, x1, x0)
        nxt = None if is_last_q else (q + 1, 1 - qpar)
        if qpar == 1:
            generic_pairs(q, qpar, (q - 3) // 2)
            a2, a1 = _Item(q, q - 2, qpar, 1), _Item(q, q - 1, qpar, 0)
        else:
            generic_pairs(q, qpar, (q - 4) // 2)
            a1 = _Item(q, q - 1, qpar, 1)
            a2 = _Item(q, q - 2, qpar, 0)
            tick(a1, a2, _Item(q, q - 3, qpar, 1))
        xl = _Item(q, q, qpar, qpar, last=True)
        tick(xl, a1, a2, next_q=nxt)
        return a1, xl

    setup(0, 0)
    x00 = _Item(0, 0, 0, 0, first=True, last=True)
    x10, x11 = _Item(1, 0, 1, 0, first=True), _Item(1, 1, 1, 1, last=True)
    x20, x21, x22 = _Item(2, 0, 0, 0, first=True), _Item(2, 1, 0, 1), _Item(2, 2, 0, 0, last=True)
    tick(x00, None, None, next_q=(1, 1))
    tick(x10, x00, None)
    tick(x11, x10, x00, next_q=(2, 0))
    tick(x20, x11, x10)
    tick(x21, x20, x11)
    tick(x22, x21, x20, next_q=(3, 1))

    def block_pair(j, carry):
        query_block(3 + 2 * j, 1)
        query_block(4 + 2 * j, 0)
        return carry

    lax.fori_loop(0, (nq - 4) // 2, block_pair, 0)
    a1, xl = query_block(nq - 1, 1, is_last_q=True)
    tick(None, xl, a1)
    tick(None, None, xl)


def _attn_flat(q1, q2, k1, k2, v_t, *, tq, fox):
    pairs, b, t, _ = q1.shape
    hw = pairs * LANES
    nq = t // tq
    assert t % tq == 0 and nq % 2 == 0 and nq >= 4, (t, tq)
    assert v_t.shape == (b, nq, hw, tq), v_t.shape
    m2 = 2 * tq
    spec = pl.BlockSpec((None, None, t, LANES), lambda bi, p: (p, bi, 0, 0))
    k2spec = pl.BlockSpec((None, t, LANES), lambda bi, p: (bi, 0, 0))
    vspec = pl.BlockSpec((None, nq, LANES, tq), lambda bi, p: (bi, 0, p, 0))
    if fox:
        in_specs = [spec, spec, k2spec, vspec]
        args = (q1, k1, k2, v_t)
    else:
        q2spec = pl.BlockSpec((None, None, t, LANES), lambda bi, p: (p // 2, bi, 0, 0))
        in_specs = [spec, q2spec, spec, k2spec, vspec]
        args = (q1, q2, k1, k2, v_t)
    vmem = lambda shape, dt, n: [pltpu.VMEM(shape, dt) for _ in range(n)]
    return pl.pallas_call(
        functools.partial(_attn_flat_kernel, tq=tq, nq=nq, fox=fox),
        out_shape=jax.ShapeDtypeStruct((pairs, b, t, LANES), BF16),
        grid=(b, pairs),
        in_specs=in_specs,
        out_specs=spec,
        scratch_shapes=(vmem((2 * LANES, m2), BF16, 2)
                        + vmem((1, m2), F32, 2)
                        + vmem((ACC_ROWS, m2), F32, 2)
                        + vmem((m2 // COL_TILE, tq, COL_TILE), F32, 4)
                        + vmem((m2 // COL_TILE, tq, COL_TILE), BF16, 4)
                        + vmem((1, m2), F32, 4)
                        + vmem((1, m2), F32, 4)),
        compiler_params=pltpu.CompilerParams(dimension_semantics=("arbitrary",) * 2,
                                             vmem_limit_bytes=VMEM_LIMIT),
        name="attn_flat_fox" if fox else "attn_flat_mla",
    )(*args)


def _rope_tables(pos):
    half = ROPE_DIM // 2
    freqs = ROPE_THETA ** (-jnp.arange(half, dtype=F32) / half)
    ang = pos.astype(F32)[:, None] * freqs[None, :]
    cos, sin = jnp.cos(ang), jnp.sin(ang)
    cos_t = jnp.tile(jnp.concatenate([cos, cos], axis=1), (1, HEADS))
    sin_t = jnp.tile(jnp.concatenate([-sin, sin], axis=1), (1, HEADS))
    return cos_t, sin_t


def _swap_halves(w):
    half = w.shape[-1] // 2
    return jnp.concatenate([w[..., half:], w[..., :half]], axis=-1)


def _prep_weights(w_in, b_forget, w_q_up, w_kv_up, w_out):
    d = w_in.shape[0]
    hw = HEADS * HEAD_DIM
    o = 0
    cq = w_in[:, o:o + Q_LORA]; o += Q_LORA
    ckv = w_in[:, o:o + KV_LORA]; o += KV_LORA
    kr = w_in[:, o:o + ROPE_DIM]; o += ROPE_DIM
    fq = w_in[:, o:o + hw]; o += hw
    fk = w_in[:, o:o + hw]; o += hw
    fv = w_in[:, o:o + hw]; o += hw
    gate = w_in[:, o:o + HEADS]
    reps = LANES // ROPE_DIM
    pad = LANES - 4 * HEADS
    gate128 = jnp.concatenate([gate, jnp.repeat(gate, 3, axis=1), jnp.zeros((d, pad), w_in.dtype)], axis=1)
    w_in_p = jnp.concatenate([cq, ckv, fq, fk, fv, jnp.tile(kr, (1, reps)), jnp.tile(_swap_halves(kr), (1, reps)),
                              gate128], axis=1).astype(BF16)
    bias128 = jnp.concatenate([b_forget, jnp.repeat(b_forget, 3), jnp.zeros((pad,), F32)])[None, :]
    wq = w_q_up.reshape(Q_LORA, HEADS, NOPE_DIM + ROPE_DIM)
    wq_rope = wq[:, :, NOPE_DIM:]
    w_q_p = jnp.concatenate([wq[:, :, :NOPE_DIM].reshape(Q_LORA, -1), wq_rope.reshape(Q_LORA, -1),
                             _swap_halves(wq_rope).reshape(Q_LORA, -1)], axis=1).astype(BF16)
    wkv = w_kv_up.reshape(KV_LORA, HEADS, NOPE_DIM + HEAD_DIM)
    w_kv_p = jnp.concatenate([wkv[:, :, :NOPE_DIM].reshape(KV_LORA, -1), wkv[:, :, NOPE_DIM:].reshape(KV_LORA, -1)],
                             axis=1).astype(BF16)
    w_o1 = w_out[:hw].astype(BF16)
    w_o2 = w_out[hw:].astype(BF16)
    return w_in_p, bias128, w_q_p, w_kv_p, w_o1, w_o2


def _expand_logf(lf):
    pad = LANES - 4 * HEADS
    return jnp.concatenate([lf, jnp.repeat(lf, 3, axis=-1), jnp.zeros(lf.shape[:-1] + (pad,), lf.dtype)], axis=-1)


def kernel(x_prompt, x_sample, cache_mla_ckv, cache_mla_krope, cache_fox_k, cache_fox_v, cache_fox_logf,
           g_ffn1_pre, g_ffn1_post, w_ffn1_gu, w_ffn1_down, g_mix_pre, g_mix_post, w_in, b_forget,
           g_q_latent, w_q_up, g_kv_latent, w_kv_up, w_out, g_ffn2_pre, g_ffn2_post, w_ffn2_gu, w_ffn2_down):
    depth = w_in.shape[0]
    bp, tp, d = x_prompt.shape
    bs, ts, _ = x_sample.shape
    past = cache_mla_ckv.shape[2]
    hw = HEADS * HEAD_DIM
    tq_p = tm_p = tc_s = ROW_TILE
    tm_s = bs * ts
    tk_pad = -(-(past + ts) // tc_s) * tc_s

    cos_p, sin_p = _rope_tables(jnp.arange(tp))
    cos_s, sin_s = _rope_tables(past + jnp.arange(ts))
    cos_s, sin_s = jnp.tile(cos_s, (bs, 1)), jnp.tile(sin_s, (bs, 1))

    xp = x_prompt.reshape(bp * tp, d)
    xs = x_sample.reshape(bs * ts, d)
    rows_p, rows_s = [], []

    def pad_keys(parts):
        n = sum(a.shape[1] for a in parts)
        parts = list(parts) + [jnp.zeros((bs, tk_pad - n, parts[0].shape[2]), parts[0].dtype)]
        return jnp.concatenate(parts, axis=1)

    for l in range(depth):
        d_ff = w_ffn1_down.shape[1]
        w1 = (w_ffn1_gu[l][:, :d_ff].astype(BF16), w_ffn1_gu[l][:, d_ff:].astype(BF16), w_ffn1_down[l].astype(BF16))
        w2 = (w_ffn2_gu[l][:, :d_ff].astype(BF16), w_ffn2_gu[l][:, d_ff:].astype(BF16), w_ffn2_down[l].astype(BF16))
        w_in_p, bias128, w_q_p, w_kv_p, w_o1, w_o2 = _prep_weights(w_in[l], b_forget[l], w_q_up[l], w_kv_up[l], w_out[l])
        g1 = (g_ffn1_pre[l][None, :], g_ffn1_post[l][None, :])
        g2 = (g_ffn2_pre[l][None, :], g_ffn2_post[l][None, :])
        gm_pre, gm_post = g_mix_pre[l][None, :], g_mix_post[l][None, :]
        gq, gkv = g_q_latent[l][None, :], g_kv_latent[l][None, :]

        hp = _ffn(xp, *g1, *w1, tm=tm_p)
        (ckv, krope, fk, fv, logf, qn, qr, kn, vm_t, kr4, fqb, fkb, fv_t, ls) = _proj(
            hp, cos_p, sin_p, gm_pre, w_in_p, bias128, gq, w_q_p, gkv, w_kv_p, w_kv_p[:, hw:].T, tm=tq_p)
        sh = lambda a: a.reshape(bp, tp, a.shape[-1])
        sh_g = lambda a: a.reshape(a.shape[0], bp, tp, LANES)
        sh_t = lambda a: a.reshape(bp, tp // tq_p, hw, tq_p)
        o_mla = _attn_flat(sh_g(qn), sh_g(qr), sh_g(kn), sh(kr4), sh_t(vm_t), tq=tq_p, fox=False)
        o_fox = _attn_flat(sh_g(fqb), None, sh_g(fkb), sh(ls), sh_t(fv_t), tq=tq_p, fox=True)
        un_g = lambda o: o.reshape(o.shape[0], bp * tp, LANES)
        xp = _ffn(hp, *g2, *w2, mix=(un_g(o_mla), un_g(o_fox), w_o1, w_o2, gm_post), tm=tm_p)
        rows_p.append((ckv.reshape(bp, tp, KV_LORA), krope.reshape(bp, tp, ROPE_DIM),
                       fk.reshape(bp, tp, HEADS, HEAD_DIM), fv.reshape(bp, tp, HEADS, HEAD_DIM),
                       logf.reshape(bp, tp, HEADS)))

        hs = _ffn(xs, *g1, *w1, tm=tm_s)
        (ckv, krope, fk, fv, logf, qn, qr, kn, vm, kr4, fqb, _, _, lf128) = _proj(
            hs, cos_s, sin_s, gm_pre, w_in_p, bias128, gq, w_q_p, gkv, w_kv_p, tm=tm_s)
        kn_past, vm_past, kr4_past = _pastkv(cache_mla_ckv[l].reshape(bs * past, KV_LORA),
                                             cache_mla_krope[l].reshape(bs * past, ROPE_DIM), w_kv_p, tm=ROW_TILE)
        sh = lambda a: a.reshape(bs, -1, a.shape[-1])
        lf_all = pad_keys([_expand_logf(cache_fox_logf[l].astype(F32)), sh(lf128)])
        ls = _lsplit(lf_all, tc=tc_s)
        o_mla = _attn_sample(sh(qn), sh(qr), sh(kn_past), sh(kr4_past), sh(vm_past),
                             sh(kn), sh(kr4), sh(vm), fox=False)
        o_fox = _attn_sample(sh(fqb), None, cache_fox_k[l].reshape(bs, past, hw), ls,
                             cache_fox_v[l].reshape(bs, past, hw), sh(fk), ls, sh(fv),
                             k2n_row0=past, fox=True)
        xs = _ffn(hs, *g2, *w2, mix=(o_mla.reshape(-1, hw), o_fox.reshape(-1, hw), w_o1, w_o2, gm_post), tm=tm_s)
        rows_s.append((ckv.reshape(bs, ts, KV_LORA), krope.reshape(bs, ts, ROPE_DIM),
                       fk.reshape(bs, ts, HEADS, HEAD_DIM), fv.reshape(bs, ts, HEADS, HEAD_DIM),
                       logf.reshape(bs, ts, HEADS)))

    outs_p = [jnp.stack([r[i] for r in rows_p]) for i in range(5)]
    outs_s = [jnp.stack([r[i] for r in rows_s]) for i in range(5)]
    return (xp.reshape(bp, tp, d), xs.reshape(bs, ts, d), *outs_p, *outs_s)
```

```python
import functools

import jax
import jax.numpy as jnp
from jax import lax
from jax.experimental import pallas as pl
from jax.experimental.pallas import tpu as pltpu

EPS = 1e-6
CHUNK = 64
ROPE_THETA = 10000.0
HEADS = 8
NOPE_DIM = 64
ROPE_DIM = 32
HEAD_DIM = 64
Q_LORA = 384
KV_LORA = 256
LANES = 128
COL_TILE = 256
ROW_TILE = 512
ONES_ROWS = 16
ACC_ROWS = HEAD_DIM + 8
LOG2E = 1.4426950408889634
NEG = -1e30
LOGF_LANE0 = 8
VMEM_LIMIT = 56 * 1024 * 1024

BF16 = jnp.bfloat16
F32 = jnp.float32


def _rms(x, g):
    ms = jnp.mean(x * x, axis=-1, keepdims=True)
    return x * lax.rsqrt(ms + EPS) * g


def _const_spec(shape):
    return pl.BlockSpec(shape, lambda *_: (0,) * len(shape), pipeline_mode=pl.Buffered(1))


def _ffn_kernel(*refs, ff_chunk, with_mix):
    if with_mix:
        (h_ref, o1_ref, o2_ref, wo1_ref, wo2_ref, gmix_ref,
         gpre_ref, gpost_ref, wg_ref, wu_ref, wd_ref, out_ref) = refs
        def rows(o_ref):
            if len(o_ref.shape) == 2:
                return o_ref[...]
            return jnp.concatenate([o_ref[g] for g in range(o_ref.shape[0])], axis=1)
        mix = jnp.dot(rows(o1_ref), wo1_ref[...], preferred_element_type=F32)
        mix = mix + jnp.dot(rows(o2_ref), wo2_ref[...], preferred_element_type=F32)
        x = h_ref[...] + _rms(mix, gmix_ref[...])
    else:
        x_ref, gpre_ref, gpost_ref, wg_ref, wu_ref, wd_ref, out_ref = refs
        x = x_ref[...]
    n = _rms(x, gpre_ref[...]).astype(BF16)
    d_ff = wg_ref.shape[1]
    acc = jnp.zeros(x.shape, F32)
    for c in range(d_ff // ff_chunk):
        cols = slice(c * ff_chunk, (c + 1) * ff_chunk)
        gate = jnp.dot(n, wg_ref[:, cols], preferred_element_type=F32)
        up = jnp.dot(n, wu_ref[:, cols], preferred_element_type=F32)
        act = (gate * jax.nn.sigmoid(gate) * up).astype(BF16)
        acc = acc + jnp.dot(act, wd_ref[cols, :], preferred_element_type=F32)
    out_ref[...] = x + 0.5 * _rms(acc, gpost_ref[...])


def _ffn(x, g_pre, g_post, w_g, w_u, w_d, mix=None, *, tm):
    n, d = x.shape
    d_ff = w_g.shape[1]
    row = lambda w: pl.BlockSpec((tm, w), lambda i: (i, 0))
    in_specs = [row(d)]
    args = [x]
    if mix is not None:
        o1, o2, wo1, wo2, g_mix = mix
        grouped = lambda o: pl.BlockSpec((o.shape[0], tm, LANES), lambda i: (0, i, 0))
        in_specs += [row(o.shape[1]) if o.ndim == 2 else grouped(o) for o in (o1, o2)]
        in_specs += [_const_spec(wo1.shape), _const_spec(wo2.shape), _const_spec((1, d))]
        args += [o1, o2, wo1, wo2, g_mix]
    in_specs += [_const_spec((1, d)), _const_spec((1, d)), _const_spec(w_g.shape), _const_spec(w_u.shape),
                 _const_spec(w_d.shape)]
    args += [g_pre, g_post, w_g, w_u, w_d]
    return pl.pallas_call(
        functools.partial(_ffn_kernel, ff_chunk=COL_TILE, with_mix=mix is not None),
        out_shape=jax.ShapeDtypeStruct((n, d), F32),
        grid=(n // tm,),
        in_specs=in_specs,
        out_specs=row(d),
        compiler_params=pltpu.CompilerParams(dimension_semantics=("arbitrary",), vmem_limit_bytes=VMEM_LIMIT),
        name="ffn_mix" if mix is not None else "ffn",
    )(*args)


_C_CQ = 0
_C_CKV = _C_CQ + Q_LORA
_C_FQ = _C_CKV + KV_LORA
_C_FK = _C_FQ + HEADS * HEAD_DIM
_C_FV = _C_FK + HEADS * HEAD_DIM
_C_KR = _C_FV + HEADS * HEAD_DIM
_C_KRS = _C_KR + LANES
_C_GATE = _C_KRS + LANES
_C_END = _C_GATE + LANES


def _log_sigmoid(x):
    return jnp.minimum(x, 0.0) - jnp.log1p(jnp.exp(-jnp.abs(x)))


def _proj_kernel(*refs, values_t, stream_blocks):
    (h_ref, cos_ref, sin_ref, gpre_ref, win_ref, bias_ref, gq_ref, wq_ref, gkv_ref, wkv_ref) = refs[:10]
    n_in = 11 if values_t else 10
    (ckv_ref, krope_ref, fk_ref, fv_ref, logf_ref,
     qn_ref, qr_ref, kn_ref, vm_ref, kr4_ref, fqb_ref, fkb_ref, fvb_ref, lf128_ref) = refs[n_in:n_in + 14]
    nt_dims = (((1,), (1,)), ((), ()))
    hw = HEADS * HEAD_DIM
    u = _rms(h_ref[...], gpre_ref[...]).astype(BF16)
    proj = jnp.dot(u, win_ref[...], preferred_element_type=F32)
    cos = cos_ref[...]
    sin = sin_ref[...]

    cq = _rms(proj[:, _C_CQ:_C_CKV], gq_ref[...]).astype(BF16)
    q = jnp.dot(cq, wq_ref[...], preferred_element_type=F32)
    q_scale = (NOPE_DIM + ROPE_DIM) ** -0.5 * LOG2E
    rw = HEADS * ROPE_DIM
    def put(ref, x):
        if values_t:
            for g in range(x.shape[1] // LANES):
                ref[g] = x[:, g * LANES:(g + 1) * LANES]
        else:
            ref[...] = x

    put(qn_ref, (q[:, :hw] * q_scale).astype(BF16))
    put(qr_ref, ((q[:, hw:hw + rw] * cos + q[:, hw + rw:] * sin) * q_scale).astype(BF16))

    ckv = _rms(proj[:, _C_CKV:_C_FQ], gkv_ref[...])
    ckv_ref[...] = ckv
    ckv_b = ckv.astype(BF16)
    if values_t:
        put(kn_ref, jnp.dot(ckv_b, wkv_ref[:, :hw], preferred_element_type=F32).astype(BF16))
        vm_ref[0] = lax.dot_general(refs[10][...], ckv_b, nt_dims, preferred_element_type=F32).astype(BF16)
    else:
        kv = jnp.dot(ckv_b, wkv_ref[...], preferred_element_type=F32)
        kn_ref[...] = kv[:, :hw].astype(BF16)
        vm_ref[...] = kv[:, hw:].astype(BF16)
    kr4 = proj[:, _C_KR:_C_KRS] * cos[:, :LANES] + proj[:, _C_KRS:_C_GATE] * sin[:, :LANES]
    krope_ref[...] = kr4[:, :ROPE_DIM]
    kr4_ref[...] = kr4.astype(BF16)

    fq = proj[:, _C_FQ:_C_FK]
    fk = proj[:, _C_FK:_C_FV]
    fv = proj[:, _C_FV:_C_KR]
    put(fqb_ref, (fq * (HEAD_DIM ** -0.5 * LOG2E)).astype(BF16))
    if values_t:
        def by_head(x):
            heads = jnp.stack([x[:, h * HEAD_DIM:(h + 1) * HEAD_DIM] for h in range(HEADS)], axis=0)
            return jnp.swapaxes(heads, 0, 1)
        fk_ref[...] = by_head(fk)
        fv_ref[...] = by_head(fv)
    else:
        fk_ref[...] = fk
        fv_ref[...] = fv
    put(fkb_ref, fk.astype(BF16))
    if values_t:
        fvb_ref[0] = fv.T.astype(BF16)
    else:
        fvb_ref[...] = fv.astype(BF16)
    logf = _log_sigmoid(proj[:, _C_GATE:_C_END] + bias_ref[...])
    logf_ref[...] = logf[:, :HEADS]
    if values_t:
        carry_ref = refs[-1]

        @pl.when(pl.program_id(0) % stream_blocks == 0)
        def _():
            carry_ref[...] = jnp.zeros_like(carry_ref)

        lf128_ref[...] = _cumsum_split(logf, carry_ref)
    else:
        lf128_ref[...] = logf


def _proj(h, cos, sin, g_pre, w_in, bias, g_q, w_q, g_kv, w_kv, w_v_t=None, *, tm):
    n, d = h.shape
    t_blocks = cos.shape[0] // tm
    hw = HEADS * HEAD_DIM
    values_t = w_v_t is not None
    row = lambda w: pl.BlockSpec((tm, w), lambda i: (i, 0))
    tab = pl.BlockSpec((tm, HEADS * ROPE_DIM), lambda i: (i % t_blocks, 0))
    widths = [(KV_LORA, F32), (ROPE_DIM, F32), (hw, F32), (hw, F32), (HEADS, F32),
              (hw, BF16), (HEADS * ROPE_DIM, BF16), (hw, BF16), (hw, BF16), (LANES, BF16),
              (hw, BF16), (hw, BF16), (hw, BF16), (LANES, F32)]
    out_shape = [jax.ShapeDtypeStruct((n, w), dt) for w, dt in widths]
    out_specs = [row(w) for w, _ in widths]
    in_specs = [row(d), tab, tab, _const_spec((1, d)), _const_spec(w_in.shape), _const_spec((1, LANES)),
                _const_spec((1, Q_LORA)), _const_spec(w_q.shape), _const_spec((1, KV_LORA)),
                _const_spec(w_kv.shape)]
    args = [h, cos, sin, g_pre, w_in, bias, g_q, w_q, g_kv, w_kv]
    if values_t:
        for i in (8, 12):
            out_shape[i] = jax.ShapeDtypeStruct((n // tm, hw, tm), BF16)
            out_specs[i] = pl.BlockSpec((1, hw, tm), lambda i: (i, 0, 0))
        out_shape[13] = jax.ShapeDtypeStruct((n, LANES), BF16)
        for i in (5, 6, 7, 10, 11):
            groups = widths[i][0] // LANES
            out_shape[i] = jax.ShapeDtypeStruct((groups, n, LANES), BF16)
            out_specs[i] = pl.BlockSpec((groups, tm, LANES), lambda i: (0, i, 0))
        for i in (2, 3):
            out_shape[i] = jax.ShapeDtypeStruct((n, HEADS, HEAD_DIM), F32)
            out_specs[i] = pl.BlockSpec((tm, HEADS, HEAD_DIM), lambda i: (i, 0, 0))
        in_specs.append(_const_spec(w_v_t.shape))
        args.append(w_v_t)
    return pl.pallas_call(
        functools.partial(_proj_kernel, values_t=values_t, stream_blocks=t_blocks),
        out_shape=out_shape,
        grid=(n // tm,),
        in_specs=in_specs,
        out_specs=out_specs,
        scratch_shapes=[pltpu.VMEM((1, LANES), F32)] if values_t else [],
        compiler_params=pltpu.CompilerParams(dimension_semantics=("arbitrary",), vmem_limit_bytes=VMEM_LIMIT),
        name="proj",
    )(*args)


def _pastkv_kernel(ckv_ref, kr_ref, wkv_ref, kn_ref, vm_ref, kr4_ref):
    hw = HEADS * HEAD_DIM
    kv = jnp.dot(ckv_ref[...].astype(BF16), wkv_ref[...], preferred_element_type=F32)
    kn_ref[...] = kv[:, :hw].astype(BF16)
    vm_ref[...] = kv[:, hw:].astype(BF16)
    src = lax.broadcasted_iota(jnp.int32, (ROPE_DIM, LANES), 0)
    dst = lax.broadcasted_iota(jnp.int32, (ROPE_DIM, LANES), 1)
    rep = (dst % ROPE_DIM == src).astype(BF16)
    kr4_ref[...] = jnp.dot(kr_ref[...].astype(BF16), rep, preferred_element_type=F32).astype(BF16)


def _pastkv(ckv, krope, w_kv, *, tm):
    n = ckv.shape[0]
    hw = HEADS * HEAD_DIM
    row = lambda w: pl.BlockSpec((tm, w), lambda i: (i, 0))
    return pl.pallas_call(
        _pastkv_kernel,
        out_shape=[jax.ShapeDtypeStruct((n, hw), BF16), jax.ShapeDtypeStruct((n, hw), BF16),
                   jax.ShapeDtypeStruct((n, LANES), BF16)],
        grid=(n // tm,),
        in_specs=[row(KV_LORA), row(ROPE_DIM), _const_spec(w_kv.shape)],
        out_specs=[row(hw), row(hw), row(LANES)],
        compiler_params=pltpu.CompilerParams(dimension_semantics=("arbitrary",), vmem_limit_bytes=VMEM_LIMIT),
        name="pastkv",
    )(ckv, krope, w_kv)


def _split3(y, lane):
    hi = y.astype(BF16).astype(F32)
    r1 = y - hi
    mid = r1.astype(BF16).astype(F32)
    lo = r1 - mid
    j = (lane - LOGF_LANE0) % 3
    sel = jnp.where(j == 0, hi, jnp.where(j == 1, mid, lo))
    used = (lane >= LOGF_LANE0) & (lane < LOGF_LANE0 + 3 * HEADS)
    return jnp.where(used, sel, 0.0).astype(BF16)


def _cumsum_split(x, carry_ref):
    tc = x.shape[0]
    hi = x.astype(BF16)
    r1 = x - hi.astype(F32)
    mid = r1.astype(BF16)
    lo = (r1 - mid.astype(F32)).astype(BF16)
    r = lax.broadcasted_iota(jnp.int32, (tc, tc), 0)
    c = lax.broadcasted_iota(jnp.int32, (tc, tc), 1)
    tri = (c <= r).astype(BF16)
    cum = (jnp.dot(tri, hi, preferred_element_type=F32) + jnp.dot(tri, mid, preferred_element_type=F32)
           + jnp.dot(tri, lo, preferred_element_type=F32)) + carry_ref[...]
    carry_ref[...] = cum[tc - 1:tc, :]
    lane = lax.broadcasted_iota(jnp.int32, (tc, LANES), 1)
    return _split3(cum * (-LOG2E), lane)


def _lsplit_kernel(x_ref, o_ref, carry_ref):
    @pl.when(pl.program_id(1) == 0)
    def _():
        carry_ref[...] = jnp.zeros_like(carry_ref)

    o_ref[0] = _cumsum_split(x_ref[0], carry_ref)


def _lsplit(lf128, *, tc):
    b, t, _ = lf128.shape
    spec = pl.BlockSpec((1, tc, LANES), lambda i, j: (i, j, 0))
    return pl.pallas_call(
        _lsplit_kernel,
        out_shape=jax.ShapeDtypeStruct((b, t, LANES), BF16),
        grid=(b, t // tc),
        in_specs=[spec],
        out_specs=spec,
        scratch_shapes=[pltpu.VMEM((1, LANES), F32)],
        compiler_params=pltpu.CompilerParams(dimension_semantics=("arbitrary", "arbitrary")),
        name="lsplit",
    )(lf128)


def _attn_sample_kernel(*refs, past, fox):
    n_q = 1 if fox else 2
    q1_ref = refs[0]
    q2_ref = None if fox else refs[1]
    k1p_ref, k2p_ref, vp_ref, k1n_ref, k2n_ref, vn_ref, o_ref = refs[n_q:]
    ts = q1_ref.shape[1]
    p = pl.program_id(1)
    lane = lax.broadcasted_iota(jnp.int32, (ts, LANES), 1)
    q1 = q1_ref[0].astype(F32)
    q2 = jnp.ones_like(q1) if fox else q2_ref[0].astype(F32)

    def head_rows(a):
        main = jnp.where((lane >= a * HEAD_DIM) & (lane < (a + 1) * HEAD_DIM), q1, 0.0)
        if fox:
            lo, width = LOGF_LANE0 + 3 * (2 * p + a), 3
        else:
            lo, width = ROPE_DIM * (2 * (p % 2) + a), ROPE_DIM
        aux = jnp.where((lane >= lo) & (lane < lo + width), q2, 0.0)
        return jnp.concatenate([main, aux], axis=1).astype(BF16)

    q = jnp.concatenate([head_rows(0), head_rows(1)], axis=0)
    nt_dims = (((1,), (1,)), ((), ()))

    def logits(k1_ref, k2_ref):
        k = jnp.concatenate([k1_ref[0].astype(BF16), k2_ref[0].astype(BF16)], axis=1)
        return lax.dot_general(q, k, nt_dims, preferred_element_type=F32)

    s_p = logits(k1p_ref, k2p_ref)
    s_n = logits(k1n_ref, k2n_ref)
    r = lax.broadcasted_iota(jnp.int32, s_n.shape, 0)
    t_pos = past + jnp.where(r >= ts, r - ts, r)
    s_pos = past + lax.broadcasted_iota(jnp.int32, s_n.shape, 1)
    vis = (s_pos <= t_pos) if fox else ((s_pos // CHUNK) <= (t_pos // CHUNK))
    s_n = jnp.where(vis, s_n, NEG)
    m = jnp.maximum(jnp.max(s_p, axis=1, keepdims=True), jnp.max(s_n, axis=1, keepdims=True))
    p_p = jnp.exp2(s_p - m)
    p_n = jnp.exp2(s_n - m)
    denom = jnp.sum(p_p, axis=1, keepdims=True) + jnp.sum(p_n, axis=1, keepdims=True)
    out = (jnp.dot(p_p.astype(BF16), vp_ref[0].astype(BF16), preferred_element_type=F32)
           + jnp.dot(p_n.astype(BF16), vn_ref[0].astype(BF16), preferred_element_type=F32)) / denom
    o_ref[0] = jnp.where(lane < HEAD_DIM, out[:ts], out[ts:]).astype(o_ref.dtype)


def _attn_sample(q1, q2, k1p, k2p, vp, k1n, k2n, vn, *, k2n_row0=0, fox):
    b, ts, hw = q1.shape
    past = k1p.shape[1]
    assert k2n_row0 % ts == 0
    pairs = hw // LANES
    qspec = pl.BlockSpec((1, ts, LANES), lambda bi, p: (bi, 0, p))
    pspec = pl.BlockSpec((1, past, LANES), lambda bi, p: (bi, 0, p))
    p2spec = pl.BlockSpec((1, past, LANES), lambda bi, p: (bi, 0, 0))
    n2spec = pl.BlockSpec((1, ts, LANES), lambda bi, p: (bi, k2n_row0 // ts, 0))
    in_specs = [pspec, p2spec, pspec, qspec, n2spec, qspec]
    args = (k1p, k2p, vp, k1n, k2n, vn)
    if fox:
        in_specs, args = [qspec] + in_specs, (q1,) + args
    else:
        q2spec = pl.BlockSpec((1, ts, LANES), lambda bi, p: (bi, 0, p // 2))
        in_specs, args = [qspec, q2spec] + in_specs, (q1, q2) + args
    return pl.pallas_call(
        functools.partial(_attn_sample_kernel, past=past, fox=fox),
        out_shape=jax.ShapeDtypeStruct((b, ts, hw), BF16),
        grid=(b, pairs),
        in_specs=in_specs,
        out_specs=qspec,
        compiler_params=pltpu.CompilerParams(dimension_semantics=("arbitrary",) * 2,
                                             vmem_limit_bytes=VMEM_LIMIT),
        name="attn_sample_fox" if fox else "attn_sample_mla",
    )(*args)


class _Item:
    def __init__(self, q, kb, qpar, kpar, first=False, last=False):
        self.q, self.kb, self.qpar, self.kpar, self.first, self.last = q, kb, qpar, kpar, first, last
        self.slot = 2 * qpar + kpar


def _attn_flat_kernel(*refs, tq, nq, fox):
    n_in = 4 if fox else 5
    q1_ref = refs[0]
    q2_ref = None if fox else refs[1]
    k1_ref, k2_ref, vt_ref, o_ref = refs[n_in - 3:n_in + 1]
    scratch = refs[n_in + 1:]
    qt_refs, m_refs, acc_refs = scratch[0:2], scratch[2:4], scratch[4:6]
    s_refs, p_refs, a_refs, bm_refs = scratch[6:10], scratch[10:14], scratch[14:18], scratch[18:22]
    tk = tq
    pair = pl.program_id(1)
    col_tiles = [slice(j, j + COL_TILE) for j in range(0, 2 * tq, COL_TILE)]
    ones_rows = jnp.ones((ONES_ROWS, tk), BF16)

    def rows_of(blk):
        start = blk * tq
        return pl.ds(start if isinstance(start, int) else pl.multiple_of(start, tq), tq)

    def setup(q, qpar):
        lane = lax.broadcasted_iota(jnp.int32, (tq, LANES), 1)
        q1 = q1_ref[rows_of(q), :].astype(F32)
        q2 = jnp.ones_like(q1) if fox else q2_ref[rows_of(q), :].astype(F32)

        def head_rows(a):
            main = jnp.where((lane >= a * HEAD_DIM) & (lane < (a + 1) * HEAD_DIM), q1, 0.0)
            if fox:
                lo, width = LOGF_LANE0 + 3 * (2 * pair + a), 3
            else:
                lo, width = ROPE_DIM * (2 * (pair % 2) + a), ROPE_DIM
            aux = jnp.where((lane >= lo) & (lane < lo + width), q2, 0.0)
            return jnp.concatenate([main, aux], axis=1)

        qt_refs[qpar][...] = jnp.concatenate([head_rows(0), head_rows(1)], axis=0).T.astype(BF16)

    def key_block(kb):
        return jnp.concatenate([k1_ref[rows_of(kb), :], k2_ref[rows_of(kb), :]], axis=1)

    def scores(x, cols):
        s = jnp.dot(key_block(x.kb), qt_refs[x.qpar][:, cols], preferred_element_type=F32)
        if x.last:
            c = (cols.start % tq) + lax.broadcasted_iota(jnp.int32, s.shape, 1)
            r = lax.broadcasted_iota(jnp.int32, s.shape, 0)
            vis = (r <= c) if fox else ((r // CHUNK) <= (c // CHUNK))
            s = jnp.where(vis, s, NEG)
        s_refs[x.slot][cols.start // COL_TILE] = s
        bm_refs[x.slot][:, cols] = jnp.max(s, axis=0, keepdims=True)

    def softmax(x, cols):
        bm = bm_refs[x.slot][:, cols]
        if x.first:
            m_new = bm
            a_refs[x.slot][:, cols] = jnp.zeros_like(bm)
        else:
            m_prev = m_refs[x.qpar][:, cols]
            m_new = jnp.maximum(m_prev, bm)
            a_refs[x.slot][:, cols] = jnp.exp2(m_prev - m_new)
        m_refs[x.qpar][:, cols] = m_new
        ct = cols.start // COL_TILE
        p_refs[x.slot][ct] = jnp.exp2(s_refs[x.slot][ct] - m_new).astype(BF16)

    def values(x, cols):
        head = cols.start // tq
        v_t = vt_ref[x.kb, head * HEAD_DIM:(head + 1) * HEAD_DIM, :]
        lhs = jnp.concatenate([v_t, ones_rows], axis=0)
        pv = jnp.dot(lhs, p_refs[x.slot][cols.start // COL_TILE], preferred_element_type=F32)[:ACC_ROWS]
        acc_ref = acc_refs[x.qpar]
        if x.first:
            acc_ref[:, cols] = pv
        else:
            acc_ref[:, cols] = a_refs[x.slot][:, cols] * acc_ref[:, cols] + pv

    def finalize(x):
        acc = acc_refs[x.qpar][...]
        out_t = jnp.concatenate([acc[:HEAD_DIM, :tq] / acc[HEAD_DIM:HEAD_DIM + 1, :tq],
                                 acc[:HEAD_DIM, tq:] / acc[HEAD_DIM:HEAD_DIM + 1, tq:]], axis=0)
        o_ref[rows_of(x.q), :] = out_t.T.astype(o_ref.dtype)

    def tick(xs, xm, xv, next_q=None):
        for cols in col_tiles:
            if xs is not None:
                scores(xs, cols)
            if xv is not None:
                values(xv, cols)
            if xm is not None:
                softmax(xm, cols)
        if xv is not None and xv.last:
            finalize(xv)
        if next_q is not None:
            setup(*next_q)

    def generic_pairs(q, qpar, count):
        def body(i, carry):
            kb = 3 + 2 * i
            x_a, x_b = _Item(q, kb, qpar, 1), _Item(q, kb + 1, qpar, 0)
            tick(x_a, _Item(q, kb - 1, qpar, 0), _Item(q, kb - 2, qpar, 1))
            tick(x_b, x_a, _Item(q, kb - 1, qpar, 0))
            return carry
        lax.fori_loop(0, count, body, 0)

    def query_block(q, qpar, is_last_q=False):
        p2 = _Item(q - 1, q - 2, 1 - qpar, qpar)
        p1 = _Item(q - 1, q - 1, 1 - qpar, 1 - qpar, last=True)
        x0 = _Item(q, 0, qpar, 0, first=True)
        x1 = _Item(q, 1, qpar, 1)
        x2 = _Item(q, 2, qpar, 0)
        tick(x0, p1, p2)
        tick(x1, x0, p1)
        tick(x2, x1, x0)
        nxt = None if is_last_q else (q + 1, 1 - qpar)
        if qpar == 1:
            generic_pairs(q, qpar, (q - 3) // 2)
            a2, a1 = _Item(q, q - 2, qpar, 1), _Item(q, q - 1, qpar, 0)
        else:
            generic_pairs(q, qpar, (q - 4) // 2)
            a1 = _Item(q, q - 1, qpar, 1)
            a2 = _Item(q, q - 2, qpar, 0)
            tick(a1, a2, _Item(q, q - 3, qpar, 1))
        xl = _Item(q, q, qpar, qpar, last=True)
        tick(xl, a1, a2, next_q=nxt)
        return a1, xl

    setup(0, 0)
    x00 = _Item(0, 0, 0, 0, first=True, last=True)
    x10, x11 = _Item(1, 0, 1, 0, first=True), _Item(1, 1, 1, 1, last=True)
    x20, x21, x22 = _Item(2, 0, 0, 0, first=True), _Item(2, 1, 0, 1), _Item(2, 2, 0, 0, last=True)
    tick(x00, None, None, next_q=(1, 1))
    tick(x10, x00, None)
    tick(x11, x10, x00, next_q=(2, 0))
    tick(x20, x11, x10)
    tick(x21, x20, x11)
    tick(x22, x21, x20, next_q=(3, 1))

    def block_pair(j, carry):
        query_block(3 + 2 * j, 1)
        query_block(4 + 2 * j, 0)
        return carry

    lax.fori_loop(0, (nq - 4) // 2, block_pair, 0)
    a1, xl = query_block(nq - 1, 1, is_last_q=True)
    tick(None, xl, a1)
    tick(None, None, xl)


def _attn_flat(q1, q2, k1, k2, v_t, *, tq, fox):
    pairs, b, t, _ = q1.shape
    hw = pairs * LANES
    nq = t // tq
    assert t % tq == 0 and nq % 2 == 0 and nq >= 4, (t, tq)
    assert v_t.shape == (b, nq, hw, tq), v_t.shape
    m2 = 2 * tq
    spec = pl.BlockSpec((None, None, t, LANES), lambda bi, p: (p, bi, 0, 0))
    k2spec = pl.BlockSpec((None, t, LANES), lambda bi, p: (bi, 0, 0))
    vspec = pl.BlockSpec((None, nq, LANES, tq), lambda bi, p: (bi, 0, p, 0))
    if fox:
        in_specs = [spec, spec, k2spec, vspec]
        args = (q1, k1, k2, v_t)
    else:
        q2spec = pl.BlockSpec((None, None, t, LANES), lambda bi, p: (p // 2, bi, 0, 0))
        in_specs = [spec, q2spec, spec, k2spec, vspec]
        args = (q1, q2, k1, k2, v_t)
    vmem = lambda shape, dt, n: [pltpu.VMEM(shape, dt) for _ in range(n)]
    return pl.pallas_call(
        functools.partial(_attn_flat_kernel, tq=tq, nq=nq, fox=fox),
        out_shape=jax.ShapeDtypeStruct((pairs, b, t, LANES), BF16),
        grid=(b, pairs),
        in_specs=in_specs,
        out_specs=spec,
        scratch_shapes=(vmem((2 * LANES, m2), BF16, 2)
                        + vmem((1, m2), F32, 2)
                        + vmem((ACC_ROWS, m2), F32, 2)
                        + vmem((m2 // COL_TILE, tq, COL_TILE), F32, 4)
                        + vmem((m2 // COL_TILE, tq, COL_TILE), BF16, 4)
                        + vmem((1, m2), F32, 4)
                        + vmem((1, m2), F32, 4)),
        compiler_params=pltpu.CompilerParams(dimension_semantics=("arbitrary",) * 2,
                                             vmem_limit_bytes=VMEM_LIMIT),
        name="attn_flat_fox" if fox else "attn_flat_mla",
    )(*args)


def _rope_tables(pos):
    half = ROPE_DIM // 2
    freqs = ROPE_THETA ** (-jnp.arange(half, dtype=F32) / half)
    ang = pos.astype(F32)[:, None] * freqs[None, :]
    cos, sin = jnp.cos(ang), jnp.sin(ang)
    cos_t = jnp.tile(jnp.concatenate([cos, cos], axis=1), (1, HEADS))
    sin_t = jnp.tile(jnp.concatenate([-sin, sin], axis=1), (1, HEADS))
    return cos_t, sin_t


def _swap_halves(w):
    half = w.shape[-1] // 2
    return jnp.concatenate([w[..., half:], w[..., :half]], axis=-1)


def _prep_weights(w_in, b_forget, w_q_up, w_kv_up, w_out):
    d = w_in.shape[0]
    hw = HEADS * HEAD_DIM
    o = 0
    cq = w_in[:, o:o + Q_LORA]; o += Q_LORA
    ckv = w_in[:, o:o + KV_LORA]; o += KV_LORA
    kr = w_in[:, o:o + ROPE_DIM]; o += ROPE_DIM
    fq = w_in[:, o:o + hw]; o += hw
    fk = w_in[:, o:o + hw]; o += hw
    fv = w_in[:, o:o + hw]; o += hw
    gate = w_in[:, o:o + HEADS]
    reps = LANES // ROPE_DIM
    pad = LANES - 4 * HEADS
    gate128 = jnp.concatenate([gate, jnp.repeat(gate, 3, axis=1), jnp.zeros((d, pad), w_in.dtype)], axis=1)
    w_in_p = jnp.concatenate([cq, ckv, fq, fk, fv, jnp.tile(kr, (1, reps)), jnp.tile(_swap_halves(kr), (1, reps)),
                              gate128], axis=1).astype(BF16)
    bias128 = jnp.concatenate([b_forget, jnp.repeat(b_forget, 3), jnp.zeros((pad,), F32)])[None, :]
    wq = w_q_up.reshape(Q_LORA, HEADS, NOPE_DIM + ROPE_DIM)
    wq_rope = wq[:, :, NOPE_DIM:]
    w_q_p = jnp.concatenate([wq[:, :, :NOPE_DIM].reshape(Q_LORA, -1), wq_rope.reshape(Q_LORA, -1),
                             _swap_halves(wq_rope).reshape(Q_LORA, -1)], axis=1).astype(BF16)
    wkv = w_kv_up.reshape(KV_LORA, HEADS, NOPE_DIM + HEAD_DIM)
    w_kv_p = jnp.concatenate([wkv[:, :, :NOPE_DIM].reshape(KV_LORA, -1), wkv[:, :, NOPE_DIM:].reshape(KV_LORA, -1)],
                             axis=1).astype(BF16)
    w_o1 = w_out[:hw].astype(BF16)
    w_o2 = w_out[hw:].astype(BF16)
    return w_in_p, bias128, w_q_p, w_kv_p, w_o1, w_o2


def _expand_logf(lf):
    pad = LANES - 4 * HEADS
    return jnp.concatenate([lf, jnp.repeat(lf, 3, axis=-1), jnp.zeros(lf.shape[:-1] + (pad,), lf.dtype)], axis=-1)


def kernel(x_prompt, x_sample, cache_mla_ckv, cache_mla_krope, cache_fox_k, cache_fox_v, cache_fox_logf,
           g_ffn1_pre, g_ffn1_post, w_ffn1_gu, w_ffn1_down, g_mix_pre, g_mix_post, w_in, b_forget,
           g_q_latent, w_q_up, g_kv_latent, w_kv_up, w_out, g_ffn2_pre, g_ffn2_post, w_ffn2_gu, w_ffn2_down):
    depth = w_in.shape[0]
    bp, tp, d = x_prompt.shape
    bs, ts, _ = x_sample.shape
    past = cache_mla_ckv.shape[2]
    hw = HEADS * HEAD_DIM
    tq_p = tm_p = tc_s = ROW_TILE
    tm_s = bs * ts
    tk_pad = -(-(past + ts) // tc_s) * tc_s

    cos_p, sin_p = _rope_tables(jnp.arange(tp))
    cos_s, sin_s = _rope_tables(past + jnp.arange(ts))
    cos_s, sin_s = jnp.tile(cos_s, (bs, 1)), jnp.tile(sin_s, (bs, 1))

    xp = x_prompt.reshape(bp * tp, d)
    xs = x_sample.reshape(bs * ts, d)
    rows_p, rows_s = [], []

    def pad_keys(parts):
        n = sum(a.shape[1] for a in parts)
        parts = list(parts) + [jnp.zeros((bs, tk_pad - n, parts[0].shape[2]), parts[0].dtype)]
        return jnp.concatenate(parts, axis=1)

    for l in range(depth):
        d_ff = w_ffn1_down.shape[1]
        w1 = (w_ffn1_gu[l][:, :d_ff].astype(BF16), w_ffn1_gu[l][:, d_ff:].astype(BF16), w_ffn1_down[l].astype(BF16))
        w2 = (w_ffn2_gu[l][:, :d_ff].astype(BF16), w_ffn2_gu[l][:, d_ff:].astype(BF16), w_ffn2_down[l].astype(BF16))
        w_in_p, bias128, w_q_p, w_kv_p, w_o1, w_o2 = _prep_weights(w_in[l], b_forget[l], w_q_up[l], w_kv_up[l], w_out[l])
        g1 = (g_ffn1_pre[l][None, :], g_ffn1_post[l][None, :])
        g2 = (g_ffn2_pre[l][None, :], g_ffn2_post[l][None, :])
        gm_pre, gm_post = g_mix_pre[l][None, :], g_mix_post[l][None, :]
        gq, gkv = g_q_latent[l][None, :], g_kv_latent[l][None, :]

        hp = _ffn(xp, *g1, *w1, tm=tm_p)
        (ckv, krope, fk, fv, logf, qn, qr, kn, vm_t, kr4, fqb, fkb, fv_t, ls) = _proj(
            hp, cos_p, sin_p, gm_pre, w_in_p, bias128, gq, w_q_p, gkv, w_kv_p, w_kv_p[:, hw:].T, tm=tq_p)
        sh = lambda a: a.reshape(bp, tp, a.shape[-1])
        sh_g = lambda a: a.reshape(a.shape[0], bp, tp, LANES)
        sh_t = lambda a: a.reshape(bp, tp // tq_p, hw, tq_p)
        o_mla = _attn_flat(sh_g(qn), sh_g(qr), sh_g(kn), sh(kr4), sh_t(vm_t), tq=tq_p, fox=False)
        o_fox = _attn_flat(sh_g(fqb), None, sh_g(fkb), sh(ls), sh_t(fv_t), tq=tq_p, fox=True)
        un_g = lambda o: o.reshape(o.shape[0], bp * tp, LANES)
        xp = _ffn(hp, *g2, *w2, mix=(un_g(o_mla), un_g(o_fox), w_o1, w_o2, gm_post), tm=tm_p)
        rows_p.append((ckv.reshape(bp, tp, KV_LORA), krope.reshape(bp, tp, ROPE_DIM),
                       fk.reshape(bp, tp, HEADS, HEAD_DIM), fv.reshape(bp, tp, HEADS, HEAD_DIM),
                       logf.reshape(bp, tp, HEADS)))

        hs = _ffn(xs, *g1, *w1, tm=tm_s)
        (ckv, krope, fk, fv, logf, qn, qr, kn, vm, kr4, fqb, _, _, lf128) = _proj(
            hs, cos_s, sin_s, gm_pre, w_in_p, bias128, gq, w_q_p, gkv, w_kv_p, tm=tm_s)
        kn_past, vm_past, kr4_past = _pastkv(cache_mla_ckv[l].reshape(bs * past, KV_LORA),
                                             cache_mla_krope[l].reshape(bs * past, ROPE_DIM), w_kv_p, tm=ROW_TILE)
        sh = lambda a: a.reshape(bs, -1, a.shape[-1])
        lf_all = pad_keys([_expand_logf(cache_fox_logf[l].astype(F32)), sh(lf128)])
        ls = _lsplit(lf_all, tc=tc_s)
        o_mla = _attn_sample(sh(qn), sh(qr), sh(kn_past), sh(kr4_past), sh(vm_past),
                             sh(kn), sh(kr4), sh(vm), fox=False)
        o_fox = _attn_sample(sh(fqb), None, cache_fox_k[l].reshape(bs, past, hw), ls,
                             cache_fox_v[l].reshape(bs, past, hw), sh(fk), ls, sh(fv),
                             k2n_row0=past, fox=True)
        xs = _ffn(hs, *g2, *w2, mix=(o_mla.reshape(-1, hw), o_fox.reshape(-1, hw), w_o1, w_o2, gm_post), tm=tm_s)
        rows_s.append((ckv.reshape(bs, ts, KV_LORA), krope.reshape(bs, ts, ROPE_DIM),
                       fk.reshape(bs, ts, HEADS, HEAD_DIM), fv.reshape(bs, ts, HEADS, HEAD_DIM),
                       logf.reshape(bs, ts, HEADS)))

    outs_p = [jnp.stack([r[i] for r in rows_p]) for i in range(5)]
    outs_s = [jnp.stack([r[i] for r in rows_s]) for i in range(5)]
    return (xp.reshape(bp, tp, d), xs.reshape(bs, ts, d), *outs_p, *outs_s)
```

```python
import functools

import jax
import jax.numpy as jnp
from jax import lax
from jax.experimental import pallas as pl
from jax.experimental.pallas import tpu as pltpu

EPS = 1e-6
CHUNK = 64
ROPE_THETA = 10000.0
HEADS = 8
NOPE_DIM = 64
ROPE_DIM = 32
HEAD_DIM = 64
Q_LORA = 384
KV_LORA = 256
LANES = 128
COL_TILE = 256
ROW_TILE = 512
ONES_ROWS = 16
ACC_ROWS = HEAD_DIM + 8
LOG2E = 1.4426950408889634
NEG = -1e30
LOGF_LANE0 = 8
VMEM_LIMIT = 56 * 1024 * 1024

BF16 = jnp.bfloat16
F32 = jnp.float32


def _rms(x, g):
    ms = jnp.mean(x * x, axis=-1, keepdims=True)
    return x * lax.rsqrt(ms + EPS) * g


def _const_spec(shape):
    return pl.BlockSpec(shape, lambda *_: (0,) * len(shape), pipeline_mode=pl.Buffered(1))


def _ffn_kernel(*refs, ff_chunk, with_mix):
    if with_mix:
        (h_ref, o1_ref, o2_ref, wo1_ref, wo2_ref, gmix_ref,
         gpre_ref, gpost_ref, wg_ref, wu_ref, wd_ref, out_ref) = refs
        def rows(o_ref):
            if len(o_ref.shape) == 2:
                return o_ref[...]
            return jnp.concatenate([o_ref[g] for g in range(o_ref.shape[0])], axis=1)
        mix = jnp.dot(rows(o1_ref), wo1_ref[...], preferred_element_type=F32)
        mix = mix + jnp.dot(rows(o2_ref), wo2_ref[...], preferred_element_type=F32)
        x = h_ref[...] + _rms(mix, gmix_ref[...])
    else:
        x_ref, gpre_ref, gpost_ref, wg_ref, wu_ref, wd_ref, out_ref = refs
        x = x_ref[...]
    n = _rms(x, gpre_ref[...]).astype(BF16)
    d_ff = wg_ref.shape[1]
    acc = jnp.zeros(x.shape, F32)
    for c in range(d_ff // ff_chunk):
        cols = slice(c * ff_chunk, (c + 1) * ff_chunk)
        gate = jnp.dot(n, wg_ref[:, cols], preferred_element_type=F32)
        up = jnp.dot(n, wu_ref[:, cols], preferred_element_type=F32)
        act = (gate * jax.nn.sigmoid(gate) * up).astype(BF16)
        acc = acc + jnp.dot(act, wd_ref[cols, :], preferred_element_type=F32)
    out_ref[...] = x + 0.5 * _rms(acc, gpost_ref[...])


def _ffn(x, g_pre, g_post, w_g, w_u, w_d, mix=None, *, tm):
    n, d = x.shape
    d_ff = w_g.shape[1]
    row = lambda w: pl.BlockSpec((tm, w), lambda i: (i, 0))
    in_specs = [row(d)]
    args = [x]
    if mix is not None:
        o1, o2, wo1, wo2, g_mix = mix
        grouped = lambda o: pl.BlockSpec((o.shape[0], tm, LANES), lambda i: (0, i, 0))
        in_specs += [row(o.shape[1]) if o.ndim == 2 else grouped(o) for o in (o1, o2)]
        in_specs += [_const_spec(wo1.shape), _const_spec(wo2.shape), _const_spec((1, d))]
        args += [o1, o2, wo1, wo2, g_mix]
    in_specs += [_const_spec((1, d)), _const_spec((1, d)), _const_spec(w_g.shape), _const_spec(w_u.shape),
                 _const_spec(w_d.shape)]
    args += [g_pre, g_post, w_g, w_u, w_d]
    return pl.pallas_call(
        functools.partial(_ffn_kernel, ff_chunk=COL_TILE, with_mix=mix is not None),
        out_shape=jax.ShapeDtypeStruct((n, d), F32),
        grid=(n // tm,),
        in_specs=in_specs,
        out_specs=row(d),
        compiler_params=pltpu.CompilerParams(dimension_semantics=("arbitrary",), vmem_limit_bytes=VMEM_LIMIT),
        name="ffn_mix" if mix is not None else "ffn",
    )(*args)


_C_CQ = 0
_C_CKV = _C_CQ + Q_LORA
_C_FQ = _C_CKV + KV_LORA
_C_FK = _C_FQ + HEADS * HEAD_DIM
_C_FV = _C_FK + HEADS * HEAD_DIM
_C_KR = _C_FV + HEADS * HEAD_DIM
_C_KRS = _C_KR + LANES
_C_GATE = _C_KRS + LANES
_C_END = _C_GATE + LANES


def _log_sigmoid(x):
    return jnp.minimum(x, 0.0) - jnp.log1p(jnp.exp(-jnp.abs(x)))


def _proj_kernel(*refs, values_t, stream_blocks):
    (h_ref, cos_ref, sin_ref, gpre_ref, win_ref, bias_ref, gq_ref, wq_ref, gkv_ref, wkv_ref) = refs[:10]
    n_in = 11 if values_t else 10
    (ckv_ref, krope_ref, fk_ref, fv_ref, logf_ref,
     qn_ref, qr_ref, kn_ref, vm_ref, kr4_ref, fqb_ref, fkb_ref, fvb_ref, lf128_ref) = refs[n_in:n_in + 14]
    nt_dims = (((1,), (1,)), ((), ()))
    hw = HEADS * HEAD_DIM
    u = _rms(h_ref[...], gpre_ref[...]).astype(BF16)
    proj = jnp.dot(u, win_ref[...], preferred_element_type=F32)
    cos = cos_ref[...]
    sin = sin_ref[...]

    cq = _rms(proj[:, _C_CQ:_C_CKV], gq_ref[...]).astype(BF16)
    q = jnp.dot(cq, wq_ref[...], preferred_element_type=F32)
    q_scale = (NOPE_DIM + ROPE_DIM) ** -0.5 * LOG2E
    rw = HEADS * ROPE_DIM
    def put(ref, x):
        if values_t:
            for g in range(x.shape[1] // LANES):
                ref[g] = x[:, g * LANES:(g + 1) * LANES]
        else:
            ref[...] = x

    put(qn_ref, (q[:, :hw] * q_scale).astype(BF16))
    put(qr_ref, ((q[:, hw:hw + rw] * cos + q[:, hw + rw:] * sin) * q_scale).astype(BF16))

    ckv = _rms(proj[:, _C_CKV:_C_FQ], gkv_ref[...])
    ckv_ref[...] = ckv
    ckv_b = ckv.astype(BF16)
    if values_t:
        put(kn_ref, jnp.dot(ckv_b, wkv_ref[:, :hw], preferred_element_type=F32).astype(BF16))
        vm_ref[0] = lax.dot_general(refs[10][...], ckv_b, nt_dims, preferred_element_type=F32).astype(BF16)
    else:
        kv = jnp.dot(ckv_b, wkv_ref[...], preferred_element_type=F32)
        kn_ref[...] = kv[:, :hw].astype(BF16)
        vm_ref[...] = kv[:, hw:].astype(BF16)
    kr4 = proj[:, _C_KR:_C_KRS] * cos[:, :LANES] + proj[:, _C_KRS:_C_GATE] * sin[:, :LANES]
    krope_ref[...] = kr4[:, :ROPE_DIM]
    kr4_ref[...] = kr4.astype(BF16)

    fq = proj[:, _C_FQ:_C_FK]
    fk = proj[:, _C_FK:_C_FV]
    fv = proj[:, _C_FV:_C_KR]
    put(fqb_ref, (fq * (HEAD_DIM ** -0.5 * LOG2E)).astype(BF16))
    if values_t:
        def by_head(x):
            heads = jnp.stack([x[:, h * HEAD_DIM:(h + 1) * HEAD_DIM] for h in range(HEADS)], axis=0)
            return jnp.swapaxes(heads, 0, 1)
        fk_ref[...] = by_head(fk)
        fv_ref[...] = by_head(fv)
    else:
        fk_ref[...] = fk
        fv_ref[...] = fv
    put(fkb_ref, fk.astype(BF16))
    if values_t:
        fvb_ref[0] = fv.T.astype(BF16)
    else:
        fvb_ref[...] = fv.astype(BF16)
    logf = _log_sigmoid(proj[:, _C_GATE:_C_END] + bias_ref[...])
    logf_ref[...] = logf[:, :HEADS]
    if values_t:
        carry_ref = refs[-1]

        @pl.when(pl.program_id(0) % stream_blocks == 0)
        def _():
            carry_ref[...] = jnp.zeros_like(carry_ref)

        lf128_ref[...] = _cumsum_split(logf, carry_ref)
    else:
        lf128_ref[...] = logf


def _proj(h, cos, sin, g_pre, w_in, bias, g_q, w_q, g_kv, w_kv, w_v_t=None, *, tm):
    n, d = h.shape
    t_blocks = cos.shape[0] // tm
    hw = HEADS * HEAD_DIM
    values_t = w_v_t is not None
    row = lambda w: pl.BlockSpec((tm, w), lambda i: (i, 0))
    tab = pl.BlockSpec((tm, HEADS * ROPE_DIM), lambda i: (i % t_blocks, 0))
    widths = [(KV_LORA, F32), (ROPE_DIM, F32), (hw, F32), (hw, F32), (HEADS, F32),
              (hw, BF16), (HEADS * ROPE_DIM, BF16), (hw, BF16), (hw, BF16), (LANES, BF16),
              (hw, BF16), (hw, BF16), (hw, BF16), (LANES, F32)]
    out_shape = [jax.ShapeDtypeStruct((n, w), dt) for w, dt in widths]
    out_specs = [row(w) for w, _ in widths]
    in_specs = [row(d), tab, tab, _const_spec((1, d)), _const_spec(w_in.shape), _const_spec((1, LANES)),
                _const_spec((1, Q_LORA)), _const_spec(w_q.shape), _const_spec((1, KV_LORA)),
                _const_spec(w_kv.shape)]
    args = [h, cos, sin, g_pre, w_in, bias, g_q, w_q, g_kv, w_kv]
    if values_t:
        for i in (8, 12):
            out_shape[i] = jax.ShapeDtypeStruct((n // tm, hw, tm), BF16)
            out_specs[i] = pl.BlockSpec((1, hw, tm), lambda i: (i, 0, 0))
        out_shape[13] = jax.ShapeDtypeStruct((n, LANES), BF16)
        for i in (5, 6, 7, 10, 11):
            groups = widths[i][0] // LANES
            out_shape[i] = jax.ShapeDtypeStruct((groups, n, LANES), BF16)
            out_specs[i] = pl.BlockSpec((groups, tm, LANES), lambda i: (0, i, 0))
        for i in (2, 3):
            out_shape[i] = jax.ShapeDtypeStruct((n, HEADS, HEAD_DIM), F32)
            out_specs[i] = pl.BlockSpec((tm, HEADS, HEAD_DIM), lambda i: (i, 0, 0))
        in_specs.append(_const_spec(w_v_t.shape))
        args.append(w_v_t)
    return pl.pallas_call(
        functools.partial(_proj_kernel, values_t=values_t, stream_blocks=t_blocks),
        out_shape=out_shape,
        grid=(n // tm,),
        in_specs=in_specs,
        out_specs=out_specs,
        scratch_shapes=[pltpu.VMEM((1, LANES), F32)] if values_t else [],
        compiler_params=pltpu.CompilerParams(dimension_semantics=("arbitrary",), vmem_limit_bytes=VMEM_LIMIT),
        name="proj",
    )(*args)


def _pastkv_kernel(ckv_ref, kr_ref, wkv_ref, kn_ref, vm_ref, kr4_ref):
    hw = HEADS * HEAD_DIM
    kv = jnp.dot(ckv_ref[...].astype(BF16), wkv_ref[...], preferred_element_type=F32)
    kn_ref[...] = kv[:, :hw].astype(BF16)
    vm_ref[...] = kv[:, hw:].astype(BF16)
    src = lax.broadcasted_iota(jnp.int32, (ROPE_DIM, LANES), 0)
    dst = lax.broadcasted_iota(jnp.int32, (ROPE_DIM, LANES), 1)
    rep = (dst % ROPE_DIM == src).astype(BF16)
    kr4_ref[...] = jnp.dot(kr_ref[...].astype(BF16), rep, preferred_element_type=F32).astype(BF16)


def _pastkv(ckv, krope, w_kv, *, tm):
    n = ckv.shape[0]
    hw = HEADS * HEAD_DIM
    row = lambda w: pl.BlockSpec((tm, w), lambda i: (i, 0))
    return pl.pallas_call(
        _pastkv_kernel,
        out_shape=[jax.ShapeDtypeStruct((n, hw), BF16), jax.ShapeDtypeStruct((n, hw), BF16),
                   jax.ShapeDtypeStruct((n, LANES), BF16)],
        grid=(n // tm,),
        in_specs=[row(KV_LORA), row(ROPE_DIM), _const_spec(w_kv.shape)],
        out_specs=[row(hw), row(hw), row(LANES)],
        compiler_params=pltpu.CompilerParams(dimension_semantics=("arbitrary",), vmem_limit_bytes=VMEM_LIMIT),
        name="pastkv",
    )(ckv, krope, w_kv)


def _split3(y, lane):
    hi = y.astype(BF16).astype(F32)
    r1 = y - hi
    mid = r1.astype(BF16).astype(F32)
    lo = r1 - mid
    j = (lane - LOGF_LANE0) % 3
    sel = jnp.where(j == 0, hi, jnp.where(j == 1, mid, lo))
    used = (lane >= LOGF_LANE0) & (lane < LOGF_LANE0 + 3 * HEADS)
    return jnp.where(used, sel, 0.0).astype(BF16)


def _cumsum_split(x, carry_ref):
    tc = x.shape[0]
    hi = x.astype(BF16)
    r1 = x - hi.astype(F32)
    mid = r1.astype(BF16)
    lo = (r1 - mid.astype(F32)).astype(BF16)
    r = lax.broadcasted_iota(jnp.int32, (tc, tc), 0)
    c = lax.broadcasted_iota(jnp.int32, (tc, tc), 1)
    tri = (c <= r).astype(BF16)
    cum = (jnp.dot(tri, hi, preferred_element_type=F32) + jnp.dot(tri, mid, preferred_element_type=F32)
           + jnp.dot(tri, lo, preferred_element_type=F32)) + carry_ref[...]
    carry_ref[...] = cum[tc - 1:tc, :]
    lane = lax.broadcasted_iota(jnp.int32, (tc, LANES), 1)
    return _split3(cum * (-LOG2E), lane)


def _lsplit_kernel(x_ref, o_ref, carry_ref):
    @pl.when(pl.program_id(1) == 0)
    def _():
        carry_ref[...] = jnp.zeros_like(carry_ref)

    o_ref[0] = _cumsum_split(x_ref[0], carry_ref)


def _lsplit(lf128, *, tc):
    b, t, _ = lf128.shape
    spec = pl.BlockSpec((1, tc, LANES), lambda i, j: (i, j, 0))
    return pl.pallas_call(
        _lsplit_kernel,
        out_shape=jax.ShapeDtypeStruct((b, t, LANES), BF16),
        grid=(b, t // tc),
        in_specs=[spec],
        out_specs=spec,
        scratch_shapes=[pltpu.VMEM((1, LANES), F32)],
        compiler_params=pltpu.CompilerParams(dimension_semantics=("arbitrary", "arbitrary")),
        name="lsplit",
    )(lf128)


def _attn_sample_kernel(*refs, past, fox):
    n_q = 1 if fox else 2
    q1_ref = refs[0]
    q2_ref = None if fox else refs[1]
    k1p_ref, k2p_ref, vp_ref, k1n_ref, k2n_ref, vn_ref, o_ref = refs[n_q:]
    ts = q1_ref.shape[1]
    p = pl.program_id(1)
    lane = lax.broadcasted_iota(jnp.int32, (ts, LANES), 1)
    q1 = q1_ref[0].astype(F32)
    q2 = jnp.ones_like(q1) if fox else q2_ref[0].astype(F32)

    def head_rows(a):
        main = jnp.where((lane >= a * HEAD_DIM) & (lane < (a + 1) * HEAD_DIM), q1, 0.0)
        if fox:
            lo, width = LOGF_LANE0 + 3 * (2 * p + a), 3
        else:
            lo, width = ROPE_DIM * (2 * (p % 2) + a), ROPE_DIM
        aux = jnp.where((lane >= lo) & (lane < lo + width), q2, 0.0)
        return jnp.concatenate([main, aux], axis=1).astype(BF16)

    q = jnp.concatenate([head_rows(0), head_rows(1)], axis=0)
    nt_dims = (((1,), (1,)), ((), ()))

    def logits(k1_ref, k2_ref):
        k = jnp.concatenate([k1_ref[0].astype(BF16), k2_ref[0].astype(BF16)], axis=1)
        return lax.dot_general(q, k, nt_dims, preferred_element_type=F32)

    s_p = logits(k1p_ref, k2p_ref)
    s_n = logits(k1n_ref, k2n_ref)
    r = lax.broadcasted_iota(jnp.int32, s_n.shape, 0)
    t_pos = past + jnp.where(r >= ts, r - ts, r)
    s_pos = past + lax.broadcasted_iota(jnp.int32, s_n.shape, 1)
    vis = (s_pos <= t_pos) if fox else ((s_pos // CHUNK) <= (t_pos // CHUNK))
    s_n = jnp.where(vis, s_n, NEG)
    m = jnp.maximum(jnp.max(s_p, axis=1, keepdims=True), jnp.max(s_n, axis=1, keepdims=True))
    p_p = jnp.exp2(s_p - m)
    p_n = jnp.exp2(s_n - m)
    denom = jnp.sum(p_p, axis=1, keepdims=True) + jnp.sum(p_n, axis=1, keepdims=True)
    out = (jnp.dot(p_p.astype(BF16), vp_ref[0].astype(BF16), preferred_element_type=F32)
           + jnp.dot(p_n.astype(BF16), vn_ref[0].astype(BF16), preferred_element_type=F32)) / denom
    o_ref[0] = jnp.where(lane < HEAD_DIM, out[:ts], out[ts:]).astype(o_ref.dtype)


def _attn_sample(q1, q2, k1p, k2p, vp, k1n, k2n, vn, *, k2n_row0=0, fox):
    b, ts, hw = q1.shape
    past = k1p.shape[1]
    assert k2n_row0 % ts == 0
    pairs = hw // LANES
    qspec = pl.BlockSpec((1, ts, LANES), lambda bi, p: (bi, 0, p))
    pspec = pl.BlockSpec((1, past, LANES), lambda bi, p: (bi, 0, p))
    p2spec = pl.BlockSpec((1, past, LANES), lambda bi, p: (bi, 0, 0))
    n2spec = pl.BlockSpec((1, ts, LANES), lambda bi, p: (bi, k2n_row0 // ts, 0))
    in_specs = [pspec, p2spec, pspec, qspec, n2spec, qspec]
    args = (k1p, k2p, vp, k1n, k2n, vn)
    if fox:
        in_specs, args = [qspec] + in_specs, (q1,) + args
    else:
        q2spec = pl.BlockSpec((1, ts, LANES), lambda bi, p: (bi, 0, p // 2))
        in_specs, args = [qspec, q2spec] + in_specs, (q1, q2) + args
    return pl.pallas_call(
        functools.partial(_attn_sample_kernel, past=past, fox=fox),
        out_shape=jax.ShapeDtypeStruct((b, ts, hw), BF16),
        grid=(b, pairs),
        in_specs=in_specs,
        out_specs=qspec,
        compiler_params=pltpu.CompilerParams(dimension_semantics=("arbitrary",) * 2,
                                             vmem_limit_bytes=VMEM_LIMIT),
        name="attn_sample_fox" if fox else "attn_sample_mla",
    )(*args)


class _Item:
    def __init__(self, q, kb, qpar, kpar, first=False, last=False):
        self.q, self.kb, self.qpar, self.kpar, self.first, self.last = q, kb, qpar, kpar, first, last
        self.slot = 2 * qpar + kpar


def _attn_flat_kernel(*refs, tq, nq, fox):
    n_in = 4 if fox else 5
    q1_ref = refs[0]
    q2_ref = None if fox else refs[1]
    k1_ref, k2_ref, vt_ref, o_ref = refs[n_in - 3:n_in + 1]
    scratch = refs[n_in + 1:]
    qt_refs, m_refs, acc_refs = scratch[0:2], scratch[2:4], scratch[4:6]
    s_refs, p_refs, a_refs, bm_refs = scratch[6:10], scratch[10:14], scratch[14:18], scratch[18:22]
    tk = tq
    pair = pl.program_id(1)
    col_tiles = [slice(j, j + COL_TILE) for j in range(0, 2 * tq, COL_TILE)]
    ones_rows = jnp.ones((ONES_ROWS, tk), BF16)

    def rows_of(blk):
        start = blk * tq
        return pl.ds(start if isinstance(start, int) else pl.multiple_of(start, tq), tq)

    def setup(q, qpar):
        lane = lax.broadcasted_iota(jnp.int32, (tq, LANES), 1)
        q1 = q1_ref[rows_of(q), :].astype(F32)
        q2 = jnp.ones_like(q1) if fox else q2_ref[rows_of(q), :].astype(F32)

        def head_rows(a):
            main = jnp.where((lane >= a * HEAD_DIM) & (lane < (a + 1) * HEAD_DIM), q1, 0.0)
            if fox:
                lo, width = LOGF_LANE0 + 3 * (2 * pair + a), 3
            else:
                lo, width = ROPE_DIM * (2 * (pair % 2) + a), ROPE_DIM
            aux = jnp.where((lane >= lo) & (lane < lo + width), q2, 0.0)
            return jnp.concatenate([main, aux], axis=1)

        qt_refs[qpar][...] = jnp.concatenate([head_rows(0), head_rows(1)], axis=0).T.astype(BF16)

    def key_block(kb):
        return jnp.concatenate([k1_ref[rows_of(kb), :], k2_ref[rows_of(kb), :]], axis=1)

    def scores(x, cols):
        s = jnp.dot(key_block(x.kb), qt_refs[x.qpar][:, cols], preferred_element_type=F32)
        if x.last:
            c = (cols.start % tq) + lax.broadcasted_iota(jnp.int32, (1, COL_TILE), 1)
            r = lax.broadcasted_iota(jnp.int32, s.shape, 0)
            s = jnp.where(r <= (c if fox else (c | (CHUNK - 1))), s, NEG)
        s_refs[x.slot][cols.start // COL_TILE] = s
        bm_refs[x.slot][:, cols] = jnp.max(s, axis=0, keepdims=True)

    def softmax(x, cols):
        bm = bm_refs[x.slot][:, cols]
        if x.first:
            m_new = bm
            a_refs[x.slot][:, cols] = jnp.zeros_like(bm)
        else:
            m_prev = m_refs[x.qpar][:, cols]
            m_new = jnp.maximum(m_prev, bm)
            a_refs[x.slot][:, cols] = jnp.exp2(m_prev - m_new)
        m_refs[x.qpar][:, cols] = m_new
        ct = cols.start // COL_TILE
        p_refs[x.slot][ct] = jnp.exp2(s_refs[x.slot][ct] - m_new).astype(BF16)

    def values(x, cols):
        head = cols.start // tq
        v_t = vt_ref[x.kb, head * HEAD_DIM:(head + 1) * HEAD_DIM, :]
        lhs = jnp.concatenate([v_t, ones_rows], axis=0)
        pv = jnp.dot(lhs, p_refs[x.slot][cols.start // COL_TILE], preferred_element_type=F32)[:ACC_ROWS]
        acc_ref = acc_refs[x.qpar]
        if x.first:
            acc_ref[:, cols] = pv
        else:
            acc_ref[:, cols] = a_refs[x.slot][:, cols] * acc_ref[:, cols] + pv

    def finalize(x):
        acc = acc_refs[x.qpar][...]
        out_t = jnp.concatenate([acc[:HEAD_DIM, :tq] / acc[HEAD_DIM:HEAD_DIM + 1, :tq],
                                 acc[:HEAD_DIM, tq:] / acc[HEAD_DIM:HEAD_DIM + 1, tq:]], axis=0)
        o_ref[rows_of(x.q), :] = out_t.T.astype(o_ref.dtype)

    def tick(xs, xm, xv, next_q=None):
        for cols in col_tiles:
            if xs is not None:
                scores(xs, cols)
            if xv is not None:
                values(xv, cols)
            if xm is not None:
                softmax(xm, cols)
        if xv is not None and xv.last:
            finalize(xv)
        if next_q is not None:
            setup(*next_q)

    def generic_pairs(q, qpar, count):
        def body(i, carry):
            kb = 3 + 2 * i
            x_a, x_b = _Item(q, kb, qpar, 1), _Item(q, kb + 1, qpar, 0)
            tick(x_a, _Item(q, kb - 1, qpar, 0), _Item(q, kb - 2, qpar, 1))
            tick(x_b, x_a, _Item(q, kb - 1, qpar, 0))
            return carry
        lax.fori_loop(0, count, body, 0)

    def query_block(q, qpar, is_last_q=False):
        p2 = _Item(q - 1, q - 2, 1 - qpar, qpar)
        p1 = _Item(q - 1, q - 1, 1 - qpar, 1 - qpar, last=True)
        x0 = _Item(q, 0, qpar, 0, first=True)
        x1 = _Item(q, 1, qpar, 1)
        x2 = _Item(q, 2, qpar, 0)
        tick(x0, p1, p2)
        tick(x1, x0, p1)
        tick(x2, x1, x0)
        nxt = None if is_last_q else (q + 1, 1 - qpar)
        if qpar == 1:
            generic_pairs(q, qpar, (q - 3) // 2)
            a2, a1 = _Item(q, q - 2, qpar, 1), _Item(q, q - 1, qpar, 0)
        else:
            generic_pairs(q, qpar, (q - 4) // 2)
            a1 = _Item(q, q - 1, qpar, 1)
            a2 = _Item(q, q - 2, qpar, 0)
            tick(a1, a2, _Item(q, q - 3, qpar, 1))
        xl = _Item(q, q, qpar, qpar, last=True)
        tick(xl, a1, a2, next_q=nxt)
        return a1, xl

    setup(0, 0)
    x00 = _Item(0, 0, 0, 0, first=True, last=True)
    x10, x11 = _Item(1, 0, 1, 0, first=True), _Item(1, 1, 1, 1, last=True)
    x20, x21, x22 = _Item(2, 0, 0, 0, first=True), _Item(2, 1, 0, 1), _Item(2, 2, 0, 0, last=True)
    tick(x00, None, None, next_q=(1, 1))
    tick(x10, x00, None)
    tick(x11, x10, x00, next_q=(2, 0))
    tick(x20, x11, x10)
    tick(x21, x20, x11)
    tick(x22, x21, x20, next_q=(3, 1))

    def block_pair(j, carry):
        query_block(3 + 2 * j, 1)
        query_block(4 + 2 * j, 0)
        return carry

    lax.fori_loop(0, (nq - 4) // 2, block_pair, 0)
    a1, xl = query_block(nq - 1, 1, is_last_q=True)
    tick(None, xl, a1)
    tick(None, None, xl)


def _attn_flat(q1, q2, k1, k2, v_t, *, tq, fox):
    pairs, b, t, _ = q1.shape
    hw = pairs * LANES
    nq = t // tq
    assert t % tq == 0 and nq % 2 == 0 and nq >= 4, (t, tq)
    assert v_t.shape == (b, nq, hw, tq), v_t.shape
    m2 = 2 * tq
    spec = pl.BlockSpec((None, None, t, LANES), lambda bi, p: (p, bi, 0, 0))
    k2spec = pl.BlockSpec((None, t, LANES), lambda bi, p: (bi, 0, 0))
    vspec = pl.BlockSpec((None, nq, LANES, tq), lambda bi, p: (bi, 0, p, 0))
    if fox:
        in_specs = [spec, spec, k2spec, vspec]
        args = (q1, k1, k2, v_t)
    else:
        q2spec = pl.BlockSpec((None, None, t, LANES), lambda bi, p: (p // 2, bi, 0, 0))
        in_specs = [spec, q2spec, spec, k2spec, vspec]
        args = (q1, q2, k1, k2, v_t)
    vmem = lambda shape, dt, n: [pltpu.VMEM(shape, dt) for _ in range(n)]
    return pl.pallas_call(
        functools.partial(_attn_flat_kernel, tq=tq, nq=nq, fox=fox),
        out_shape=jax.ShapeDtypeStruct((pairs, b, t, LANES), BF16),
        grid=(b, pairs),
        in_specs=in_specs,
        out_specs=spec,
        scratch_shapes=(vmem((2 * LANES, m2), BF16, 2)
                        + vmem((1, m2), F32, 2)
                        + vmem((ACC_ROWS, m2), F32, 2)
                        + vmem((m2 // COL_TILE, tq, COL_TILE), F32, 4)
                        + vmem((m2 // COL_TILE, tq, COL_TILE), BF16, 4)
                        + vmem((1, m2), F32, 4)
                        + vmem((1, m2), F32, 4)),
        compiler_params=pltpu.CompilerParams(dimension_semantics=("arbitrary",) * 2,
                                             vmem_limit_bytes=VMEM_LIMIT),
        name="attn_flat_fox" if fox else "attn_flat_mla",
    )(*args)


def _rope_tables(pos):
    half = ROPE_DIM // 2
    freqs = ROPE_THETA ** (-jnp.arange(half, dtype=F32) / half)
    ang = pos.astype(F32)[:, None] * freqs[None, :]
    cos, sin = jnp.cos(ang), jnp.sin(ang)
    cos_t = jnp.tile(jnp.concatenate([cos, cos], axis=1), (1, HEADS))
    sin_t = jnp.tile(jnp.concatenate([-sin, sin], axis=1), (1, HEADS))
    return cos_t, sin_t


def _swap_halves(w):
    half = w.shape[-1] // 2
    return jnp.concatenate([w[..., half:], w[..., :half]], axis=-1)


def _prep_weights(w_in, b_forget, w_q_up, w_kv_up, w_out):
    d = w_in.shape[0]
    hw = HEADS * HEAD_DIM
    o = 0
    cq = w_in[:, o:o + Q_LORA]; o += Q_LORA
    ckv = w_in[:, o:o + KV_LORA]; o += KV_LORA
    kr = w_in[:, o:o + ROPE_DIM]; o += ROPE_DIM
    fq = w_in[:, o:o + hw]; o += hw
    fk = w_in[:, o:o + hw]; o += hw
    fv = w_in[:, o:o + hw]; o += hw
    gate = w_in[:, o:o + HEADS]
    reps = LANES // ROPE_DIM
    pad = LANES - 4 * HEADS
    gate128 = jnp.concatenate([gate, jnp.repeat(gate, 3, axis=1), jnp.zeros((d, pad), w_in.dtype)], axis=1)
    w_in_p = jnp.concatenate([cq, ckv, fq, fk, fv, jnp.tile(kr, (1, reps)), jnp.tile(_swap_halves(kr), (1, reps)),
                              gate128], axis=1).astype(BF16)
    bias128 = jnp.concatenate([b_forget, jnp.repeat(b_forget, 3), jnp.zeros((pad,), F32)])[None, :]
    wq = w_q_up.reshape(Q_LORA, HEADS, NOPE_DIM + ROPE_DIM)
    wq_rope = wq[:, :, NOPE_DIM:]
    w_q_p = jnp.concatenate([wq[:, :, :NOPE_DIM].reshape(Q_LORA, -1), wq_rope.reshape(Q_LORA, -1),
                             _swap_halves(wq_rope).reshape(Q_LORA, -1)], axis=1).astype(BF16)
    wkv = w_kv_up.reshape(KV_LORA, HEADS, NOPE_DIM + HEAD_DIM)
    w_kv_p = jnp.concatenate([wkv[:, :, :NOPE_DIM].reshape(KV_LORA, -1), wkv[:, :, NOPE_DIM:].reshape(KV_LORA, -1)],
                             axis=1).astype(BF16)
    w_o1 = w_out[:hw].astype(BF16)
    w_o2 = w_out[hw:].astype(BF16)
    return w_in_p, bias128, w_q_p, w_kv_p, w_o1, w_o2


def _expand_logf(lf):
    pad = LANES - 4 * HEADS
    return jnp.concatenate([lf, jnp.repeat(lf, 3, axis=-1), jnp.zeros(lf.shape[:-1] + (pad,), lf.dtype)], axis=-1)


def kernel(x_prompt, x_sample, cache_mla_ckv, cache_mla_krope, cache_fox_k, cache_fox_v, cache_fox_logf,
           g_ffn1_pre, g_ffn1_post, w_ffn1_gu, w_ffn1_down, g_mix_pre, g_mix_post, w_in, b_forget,
           g_q_latent, w_q_up, g_kv_latent, w_kv_up, w_out, g_ffn2_pre, g_ffn2_post, w_ffn2_gu, w_ffn2_down):
    depth = w_in.shape[0]
    bp, tp, d = x_prompt.shape
    bs, ts, _ = x_sample.shape
    past = cache_mla_ckv.shape[2]
    hw = HEADS * HEAD_DIM
    tq_p = tm_p = tc_s = ROW_TILE
    tm_s = bs * ts
    tk_pad = -(-(past + ts) // tc_s) * tc_s

    cos_p, sin_p = _rope_tables(jnp.arange(tp))
    cos_s, sin_s = _rope_tables(past + jnp.arange(ts))
    cos_s, sin_s = jnp.tile(cos_s, (bs, 1)), jnp.tile(sin_s, (bs, 1))

    xp = x_prompt.reshape(bp * tp, d)
    xs = x_sample.reshape(bs * ts, d)
    rows_p, rows_s = [], []

    def pad_keys(parts):
        n = sum(a.shape[1] for a in parts)
        parts = list(parts) + [jnp.zeros((bs, tk_pad - n, parts[0].shape[2]), parts[0].dtype)]
        return jnp.concatenate(parts, axis=1)

    for l in range(depth):
        d_ff = w_ffn1_down.shape[1]
        w1 = (w_ffn1_gu[l][:, :d_ff].astype(BF16), w_ffn1_gu[l][:, d_ff:].astype(BF16), w_ffn1_down[l].astype(BF16))
        w2 = (w_ffn2_gu[l][:, :d_ff].astype(BF16), w_ffn2_gu[l][:, d_ff:].astype(BF16), w_ffn2_down[l].astype(BF16))
        w_in_p, bias128, w_q_p, w_kv_p, w_o1, w_o2 = _prep_weights(w_in[l], b_forget[l], w_q_up[l], w_kv_up[l], w_out[l])
        g1 = (g_ffn1_pre[l][None, :], g_ffn1_post[l][None, :])
        g2 = (g_ffn2_pre[l][None, :], g_ffn2_post[l][None, :])
        gm_pre, gm_post = g_mix_pre[l][None, :], g_mix_post[l][None, :]
        gq, gkv = g_q_latent[l][None, :], g_kv_latent[l][None, :]

        hp = _ffn(xp, *g1, *w1, tm=tm_p)
        (ckv, krope, fk, fv, logf, qn, qr, kn, vm_t, kr4, fqb, fkb, fv_t, ls) = _proj(
            hp, cos_p, sin_p, gm_pre, w_in_p, bias128, gq, w_q_p, gkv, w_kv_p, w_kv_p[:, hw:].T, tm=tq_p)
        sh = lambda a: a.reshape(bp, tp, a.shape[-1])
        sh_g = lambda a: a.reshape(a.shape[0], bp, tp, LANES)
        sh_t = lambda a: a.reshape(bp, tp // tq_p, hw, tq_p)
        o_mla = _attn_flat(sh_g(qn), sh_g(qr), sh_g(kn), sh(kr4), sh_t(vm_t), tq=tq_p, fox=False)
        o_fox = _attn_flat(sh_g(fqb), None, sh_g(fkb), sh(ls), sh_t(fv_t), tq=tq_p, fox=True)
        un_g = lambda o: o.reshape(o.shape[0], bp * tp, LANES)
        xp = _ffn(hp, *g2, *w2, mix=(un_g(o_mla), un_g(o_fox), w_o1, w_o2, gm_post), tm=tm_p)
        rows_p.append((ckv.reshape(bp, tp, KV_LORA), krope.reshape(bp, tp, ROPE_DIM),
                       fk.reshape(bp, tp, HEADS, HEAD_DIM), fv.reshape(bp, tp, HEADS, HEAD_DIM),
                       logf.reshape(bp, tp, HEADS)))

        hs = _ffn(xs, *g1, *w1, tm=tm_s)
        (ckv, krope, fk, fv, logf, qn, qr, kn, vm, kr4, fqb, _, _, lf128) = _proj(
            hs, cos_s, sin_s, gm_pre, w_in_p, bias128, gq, w_q_p, gkv, w_kv_p, tm=tm_s)
        kn_past, vm_past, kr4_past = _pastkv(cache_mla_ckv[l].reshape(bs * past, KV_LORA),
                                             cache_mla_krope[l].reshape(bs * past, ROPE_DIM), w_kv_p, tm=ROW_TILE)
        sh = lambda a: a.reshape(bs, -1, a.shape[-1])
        lf_all = pad_keys([_expand_logf(cache_fox_logf[l].astype(F32)), sh(lf128)])
        ls = _lsplit(lf_all, tc=tc_s)
        o_mla = _attn_sample(sh(qn), sh(qr), sh(kn_past), sh(kr4_past), sh(vm_past),
                             sh(kn), sh(kr4), sh(vm), fox=False)
        o_fox = _attn_sample(sh(fqb), None, cache_fox_k[l].reshape(bs, past, hw), ls,
                             cache_fox_v[l].reshape(bs, past, hw), sh(fk), ls, sh(fv),
                             k2n_row0=past, fox=True)
        xs = _ffn(hs, *g2, *w2, mix=(o_mla.reshape(-1, hw), o_fox.reshape(-1, hw), w_o1, w_o2, gm_post), tm=tm_s)
        rows_s.append((ckv.reshape(bs, ts, KV_LORA), krope.reshape(bs, ts, ROPE_DIM),
                       fk.reshape(bs, ts, HEADS, HEAD_DIM), fv.reshape(bs, ts, HEADS, HEAD_DIM),
                       logf.reshape(bs, ts, HEADS)))

    outs_p = [jnp.stack([r[i] for r in rows_p]) for i in range(5)]
    outs_s = [jnp.stack([r[i] for r in rows_s]) for i in range(5)]
    return (xp.reshape(bp, tp, d), xs.reshape(bs, ts, d), *outs_p, *outs_s)
```

```python
import functools
import math

import jax
import jax.numpy as jnp
from jax import lax
from jax.experimental import pallas as pl
from jax.experimental.pallas import tpu as pltpu

EPS = 1e-6
CHUNK = 64
ROPE_THETA = 10000.0
HEADS = 8
NOPE_DIM = 64
ROPE_DIM = 32
HEAD_DIM = 64
Q_LORA = 384
KV_LORA = 256
LANES = 128
COL_TILE = 256
ROW_TILE = 512
ONES_ROWS = 16
ACC_ROWS = HEAD_DIM + 8
LOG2E = 1.4426950408889634
NEG = -1e30
LOGF_LANE0 = 8
VMEM_LIMIT = 56 * 1024 * 1024

BF16 = jnp.bfloat16
F32 = jnp.float32


def _rms(x, g):
    ms = jnp.mean(x * x, axis=-1, keepdims=True)
    return x * lax.rsqrt(ms + EPS) * g


def _const_spec(shape):
    return pl.BlockSpec(shape, lambda *_: (0,) * len(shape), pipeline_mode=pl.Buffered(1))


def _ffn_kernel(*refs, ff_chunk, with_mix):
    if with_mix:
        (h_ref, o1_ref, o2_ref, wo1_ref, wo2_ref, gmix_ref,
         gpre_ref, gpost_ref, wg_ref, wu_ref, wd_ref, out_ref) = refs
        def rows(o_ref):
            if len(o_ref.shape) == 2:
                return o_ref[...]
            return jnp.concatenate([o_ref[g] for g in range(o_ref.shape[0])], axis=1)
        mix = jnp.dot(rows(o1_ref), wo1_ref[...], preferred_element_type=F32)
        mix = mix + jnp.dot(rows(o2_ref), wo2_ref[...], preferred_element_type=F32)
        x = h_ref[...] + _rms(mix, gmix_ref[...])
    else:
        x_ref, gpre_ref, gpost_ref, wg_ref, wu_ref, wd_ref, out_ref = refs
        x = x_ref[...]
    n = _rms(x, gpre_ref[...]).astype(BF16)
    d_ff = wg_ref.shape[1]
    acc = jnp.zeros(x.shape, F32)
    for c in range(d_ff // ff_chunk):
        cols = slice(c * ff_chunk, (c + 1) * ff_chunk)
        gate = jnp.dot(n, wg_ref[:, cols], preferred_element_type=F32)
        up = jnp.dot(n, wu_ref[:, cols], preferred_element_type=F32)
        act = (gate * jax.nn.sigmoid(gate) * up).astype(BF16)
        acc = acc + jnp.dot(act, wd_ref[cols, :], preferred_element_type=F32)
    out_ref[...] = x + 0.5 * _rms(acc, gpost_ref[...])


def _ffn(x, g_pre, g_post, w_g, w_u, w_d, mix=None, *, tm):
    n, d = x.shape
    d_ff = w_g.shape[1]
    row = lambda w: pl.BlockSpec((tm, w), lambda i: (i, 0))
    in_specs = [row(d)]
    args = [x]
    if mix is not None:
        o1, o2, wo1, wo2, g_mix = mix
        grouped = lambda o: pl.BlockSpec((o.shape[0], tm, LANES), lambda i: (0, i, 0))
        in_specs += [row(o.shape[1]) if o.ndim == 2 else grouped(o) for o in (o1, o2)]
        in_specs += [_const_spec(wo1.shape), _const_spec(wo2.shape), _const_spec((1, d))]
        args += [o1, o2, wo1, wo2, g_mix]
    in_specs += [_const_spec((1, d)), _const_spec((1, d)), _const_spec(w_g.shape), _const_spec(w_u.shape),
                 _const_spec(w_d.shape)]
    args += [g_pre, g_post, w_g, w_u, w_d]
    return pl.pallas_call(
        functools.partial(_ffn_kernel, ff_chunk=COL_TILE, with_mix=mix is not None),
        out_shape=jax.ShapeDtypeStruct((n, d), F32),
        grid=(n // tm,),
        in_specs=in_specs,
        out_specs=row(d),
        compiler_params=pltpu.CompilerParams(dimension_semantics=("arbitrary",), vmem_limit_bytes=VMEM_LIMIT),
        name="ffn_mix" if mix is not None else "ffn",
    )(*args)


_C_CQ = 0
_C_CKV = _C_CQ + Q_LORA
_C_FQ = _C_CKV + KV_LORA
_C_FK = _C_FQ + HEADS * HEAD_DIM
_C_FV = _C_FK + HEADS * HEAD_DIM
_C_KR = _C_FV + HEADS * HEAD_DIM
_C_KRS = _C_KR + LANES
_C_GATE = _C_KRS + LANES
_C_END = _C_GATE + LANES


def _log_sigmoid(x):
    return jnp.minimum(x, 0.0) - jnp.log1p(jnp.exp(-jnp.abs(x)))


def _proj_kernel(*refs, values_t, stream_blocks):
    (h_ref, cos_ref, sin_ref, gpre_ref, win_ref, bias_ref, gq_ref, wq_ref, gkv_ref, wkv_ref) = refs[:10]
    n_in = 11 if values_t else 10
    (ckv_ref, krope_ref, fk_ref, fv_ref, logf_ref,
     qn_ref, qr_ref, kn_ref, vm_ref, kr4_ref, fqb_ref, fkb_ref, fvb_ref, lf128_ref) = refs[n_in:n_in + 14]
    nt_dims = (((1,), (1,)), ((), ()))
    hw = HEADS * HEAD_DIM
    u = _rms(h_ref[...], gpre_ref[...]).astype(BF16)
    proj = jnp.dot(u, win_ref[...], preferred_element_type=F32)
    cos = cos_ref[...]
    sin = sin_ref[...]

    cq = _rms(proj[:, _C_CQ:_C_CKV], gq_ref[...]).astype(BF16)
    q = jnp.dot(cq, wq_ref[...], preferred_element_type=F32)
    q_scale = (NOPE_DIM + ROPE_DIM) ** -0.5 * LOG2E
    rw = HEADS * ROPE_DIM
    def put(ref, x):
        if values_t:
            for g in range(x.shape[1] // LANES):
                ref[g] = x[:, g * LANES:(g + 1) * LANES]
        else:
            ref[...] = x

    put(qn_ref, (q[:, :hw] * q_scale).astype(BF16))
    put(qr_ref, ((q[:, hw:hw + rw] * cos + q[:, hw + rw:] * sin) * q_scale).astype(BF16))

    ckv = _rms(proj[:, _C_CKV:_C_FQ], gkv_ref[...])
    ckv_ref[...] = ckv
    ckv_b = ckv.astype(BF16)
    if values_t:
        put(kn_ref, jnp.dot(ckv_b, wkv_ref[:, :hw], preferred_element_type=F32).astype(BF16))
        vm_ref[0] = lax.dot_general(refs[10][...], ckv_b, nt_dims, preferred_element_type=F32).astype(BF16)
    else:
        kv = jnp.dot(ckv_b, wkv_ref[...], preferred_element_type=F32)
        kn_ref[...] = kv[:, :hw].astype(BF16)
        vm_ref[...] = kv[:, hw:].astype(BF16)
    kr4 = proj[:, _C_KR:_C_KRS] * cos[:, :LANES] + proj[:, _C_KRS:_C_GATE] * sin[:, :LANES]
    krope_ref[...] = kr4[:, :ROPE_DIM]
    kr4_ref[...] = kr4.astype(BF16)

    fq = proj[:, _C_FQ:_C_FK]
    fk = proj[:, _C_FK:_C_FV]
    fv = proj[:, _C_FV:_C_KR]
    put(fqb_ref, (fq * (HEAD_DIM ** -0.5 * LOG2E)).astype(BF16))
    if values_t:
        def by_head(x):
            heads = jnp.stack([x[:, h * HEAD_DIM:(h + 1) * HEAD_DIM] for h in range(HEADS)], axis=0)
            return jnp.swapaxes(heads, 0, 1)
        fk_ref[...] = by_head(fk)
        fv_ref[...] = by_head(fv)
    else:
        fk_ref[...] = fk
        fv_ref[...] = fv
    put(fkb_ref, fk.astype(BF16))
    if values_t:
        fvb_ref[0] = fv.T.astype(BF16)
    else:
        fvb_ref[...] = fv.astype(BF16)
    logf = _log_sigmoid(proj[:, _C_GATE:_C_END] + bias_ref[...])
    logf_ref[...] = logf[:, :HEADS]
    if values_t:
        carry_ref = refs[-1]

        @pl.when(pl.program_id(0) % stream_blocks == 0)
        def _():
            carry_ref[...] = jnp.zeros_like(carry_ref)

        lf128_ref[...] = _cumsum_split(logf, carry_ref)
    else:
        lf128_ref[...] = logf


def _proj(h, cos, sin, g_pre, w_in, bias, g_q, w_q, g_kv, w_kv, w_v_t=None, *, tm):
    n, d = h.shape
    t_blocks = cos.shape[0] // tm
    hw = HEADS * HEAD_DIM
    values_t = w_v_t is not None
    row = lambda w: pl.BlockSpec((tm, w), lambda i: (i, 0))
    tab = pl.BlockSpec((tm, HEADS * ROPE_DIM), lambda i: (i % t_blocks, 0))
    widths = [(KV_LORA, F32), (ROPE_DIM, F32), (hw, F32), (hw, F32), (HEADS, F32),
              (hw, BF16), (HEADS * ROPE_DIM, BF16), (hw, BF16), (hw, BF16), (LANES, BF16),
              (hw, BF16), (hw, BF16), (hw, BF16), (LANES, F32)]
    out_shape = [jax.ShapeDtypeStruct((n, w), dt) for w, dt in widths]
    out_specs = [row(w) for w, _ in widths]
    in_specs = [row(d), tab, tab, _const_spec((1, d)), _const_spec(w_in.shape), _const_spec((1, LANES)),
                _const_spec((1, Q_LORA)), _const_spec(w_q.shape), _const_spec((1, KV_LORA)),
                _const_spec(w_kv.shape)]
    args = [h, cos, sin, g_pre, w_in, bias, g_q, w_q, g_kv, w_kv]
    if values_t:
        for i in (8, 12):
            out_shape[i] = jax.ShapeDtypeStruct((n // tm, hw, tm), BF16)
            out_specs[i] = pl.BlockSpec((1, hw, tm), lambda i: (i, 0, 0))
        out_shape[13] = jax.ShapeDtypeStruct((n, LANES), BF16)
        for i in (5, 6, 7, 10, 11):
            groups = widths[i][0] // LANES
            out_shape[i] = jax.ShapeDtypeStruct((groups, n, LANES), BF16)
            out_specs[i] = pl.BlockSpec((groups, tm, LANES), lambda i: (0, i, 0))
        for i in (2, 3):
            out_shape[i] = jax.ShapeDtypeStruct((n, HEADS, HEAD_DIM), F32)
            out_specs[i] = pl.BlockSpec((tm, HEADS, HEAD_DIM), lambda i: (i, 0, 0))
        in_specs.append(_const_spec(w_v_t.shape))
        args.append(w_v_t)
    return pl.pallas_call(
        functools.partial(_proj_kernel, values_t=values_t, stream_blocks=t_blocks),
        out_shape=out_shape,
        grid=(n // tm,),
        in_specs=in_specs,
        out_specs=out_specs,
        scratch_shapes=[pltpu.VMEM((1, LANES), F32)] if values_t else [],
        compiler_params=pltpu.CompilerParams(dimension_semantics=("arbitrary",), vmem_limit_bytes=VMEM_LIMIT),
        name="proj",
    )(*args)


def _pastkv_kernel(ckv_ref, kr_ref, wkv_ref, kn_ref, vm_ref, kr4_ref):
    hw = HEADS * HEAD_DIM
    kv = jnp.dot(ckv_ref[...].astype(BF16), wkv_ref[...], preferred_element_type=F32)
    kn_ref[...] = kv[:, :hw].astype(BF16)
    vm_ref[...] = kv[:, hw:].astype(BF16)
    src = lax.broadcasted_iota(jnp.int32, (ROPE_DIM, LANES), 0)
    dst = lax.broadcasted_iota(jnp.int32, (ROPE_DIM, LANES), 1)
    rep = (dst % ROPE_DIM == src).astype(BF16)
    kr4_ref[...] = jnp.dot(kr_ref[...].astype(BF16), rep, preferred_element_type=F32).astype(BF16)


def _pastkv(ckv, krope, w_kv, *, tm):
    n = ckv.shape[0]
    hw = HEADS * HEAD_DIM
    row = lambda w: pl.BlockSpec((tm, w), lambda i: (i, 0))
    return pl.pallas_call(
        _pastkv_kernel,
        out_shape=[jax.ShapeDtypeStruct((n, hw), BF16), jax.ShapeDtypeStruct((n, hw), BF16),
                   jax.ShapeDtypeStruct((n, LANES), BF16)],
        grid=(n // tm,),
        in_specs=[row(KV_LORA), row(ROPE_DIM), _const_spec(w_kv.shape)],
        out_specs=[row(hw), row(hw), row(LANES)],
        compiler_params=pltpu.CompilerParams(dimension_semantics=("arbitrary",), vmem_limit_bytes=VMEM_LIMIT),
        name="pastkv",
    )(ckv, krope, w_kv)


def _split3(y, lane):
    hi = y.astype(BF16).astype(F32)
    r1 = y - hi
    mid = r1.astype(BF16).astype(F32)
    lo = r1 - mid
    j = (lane - LOGF_LANE0) % 3
    sel = jnp.where(j == 0, hi, jnp.where(j == 1, mid, lo))
    used = (lane >= LOGF_LANE0) & (lane < LOGF_LANE0 + 3 * HEADS)
    return jnp.where(used, sel, 0.0).astype(BF16)


def _cumsum_split(x, carry_ref):
    tc = x.shape[0]
    hi = x.astype(BF16)
    r1 = x - hi.astype(F32)
    mid = r1.astype(BF16)
    lo = (r1 - mid.astype(F32)).astype(BF16)
    r = lax.broadcasted_iota(jnp.int32, (tc, tc), 0)
    c = lax.broadcasted_iota(jnp.int32, (tc, tc), 1)
    tri = (c <= r).astype(BF16)
    cum = (jnp.dot(tri, hi, preferred_element_type=F32) + jnp.dot(tri, mid, preferred_element_type=F32)
           + jnp.dot(tri, lo, preferred_element_type=F32)) + carry_ref[...]
    carry_ref[...] = cum[tc - 1:tc, :]
    lane = lax.broadcasted_iota(jnp.int32, (tc, LANES), 1)
    return _split3(cum * (-LOG2E), lane)


def _lsplit_kernel(x_ref, o_ref, carry_ref):
    @pl.when(pl.program_id(1) == 0)
    def _():
        carry_ref[...] = jnp.zeros_like(carry_ref)

    o_ref[0] = _cumsum_split(x_ref[0], carry_ref)


def _lsplit(lf128, *, tc):
    b, t, _ = lf128.shape
    spec = pl.BlockSpec((1, tc, LANES), lambda i, j: (i, j, 0))
    return pl.pallas_call(
        _lsplit_kernel,
        out_shape=jax.ShapeDtypeStruct((b, t, LANES), BF16),
        grid=(b, t // tc),
        in_specs=[spec],
        out_specs=spec,
        scratch_shapes=[pltpu.VMEM((1, LANES), F32)],
        compiler_params=pltpu.CompilerParams(dimension_semantics=("arbitrary", "arbitrary")),
        name="lsplit",
    )(lf128)


def _attn_sample_kernel(*refs, past, fox):
    n_q = 1 if fox else 2
    q1_ref = refs[0]
    q2_ref = None if fox else refs[1]
    k1p_ref, k2p_ref, vp_ref, k1n_ref, k2n_ref, vn_ref, o_ref = refs[n_q:]
    ts = q1_ref.shape[1]
    p = pl.program_id(1)
    lane = lax.broadcasted_iota(jnp.int32, (ts, LANES), 1)
    q1 = q1_ref[0].astype(F32)
    q2 = jnp.ones_like(q1) if fox else q2_ref[0].astype(F32)

    def head_rows(a):
        main = jnp.where((lane >= a * HEAD_DIM) & (lane < (a + 1) * HEAD_DIM), q1, 0.0)
        if fox:
            lo, width = LOGF_LANE0 + 3 * (2 * p + a), 3
        else:
            lo, width = ROPE_DIM * (2 * (p % 2) + a), ROPE_DIM
        aux = jnp.where((lane >= lo) & (lane < lo + width), q2, 0.0)
        return jnp.concatenate([main, aux], axis=1).astype(BF16)

    q = jnp.concatenate([head_rows(0), head_rows(1)], axis=0)
    nt_dims = (((1,), (1,)), ((), ()))

    def logits(k1_ref, k2_ref):
        k = jnp.concatenate([k1_ref[0].astype(BF16), k2_ref[0].astype(BF16)], axis=1)
        return lax.dot_general(q, k, nt_dims, preferred_element_type=F32)

    s_p = logits(k1p_ref, k2p_ref)
    s_n = logits(k1n_ref, k2n_ref)
    r = lax.broadcasted_iota(jnp.int32, s_n.shape, 0)
    t_pos = past + jnp.where(r >= ts, r - ts, r)
    s_pos = past + lax.broadcasted_iota(jnp.int32, s_n.shape, 1)
    vis = (s_pos <= t_pos) if fox else ((s_pos // CHUNK) <= (t_pos // CHUNK))
    s_n = jnp.where(vis, s_n, NEG)
    m = jnp.maximum(jnp.max(s_p, axis=1, keepdims=True), jnp.max(s_n, axis=1, keepdims=True))
    p_p = jnp.exp2(s_p - m)
    p_n = jnp.exp2(s_n - m)
    denom = jnp.sum(p_p, axis=1, keepdims=True) + jnp.sum(p_n, axis=1, keepdims=True)
    out = (jnp.dot(p_p.astype(BF16), vp_ref[0].astype(BF16), preferred_element_type=F32)
           + jnp.dot(p_n.astype(BF16), vn_ref[0].astype(BF16), preferred_element_type=F32)) / denom
    o_ref[0] = jnp.where(lane < HEAD_DIM, out[:ts], out[ts:]).astype(o_ref.dtype)


def _attn_sample(q1, q2, k1p, k2p, vp, k1n, k2n, vn, *, k2n_row0=0, fox):
    b, ts, hw = q1.shape
    past = k1p.shape[1]
    assert k2n_row0 % ts == 0
    pairs = hw // LANES
    qspec = pl.BlockSpec((1, ts, LANES), lambda bi, p: (bi, 0, p))
    pspec = pl.BlockSpec((1, past, LANES), lambda bi, p: (bi, 0, p))
    p2spec = pl.BlockSpec((1, past, LANES), lambda bi, p: (bi, 0, 0))
    n2spec = pl.BlockSpec((1, ts, LANES), lambda bi, p: (bi, k2n_row0 // ts, 0))
    in_specs = [pspec, p2spec, pspec, qspec, n2spec, qspec]
    args = (k1p, k2p, vp, k1n, k2n, vn)
    if fox:
        in_specs, args = [qspec] + in_specs, (q1,) + args
    else:
        q2spec = pl.BlockSpec((1, ts, LANES), lambda bi, p: (bi, 0, p // 2))
        in_specs, args = [qspec, q2spec] + in_specs, (q1, q2) + args
    return pl.pallas_call(
        functools.partial(_attn_sample_kernel, past=past, fox=fox),
        out_shape=jax.ShapeDtypeStruct((b, ts, hw), BF16),
        grid=(b, pairs),
        in_specs=in_specs,
        out_specs=qspec,
        compiler_params=pltpu.CompilerParams(dimension_semantics=("arbitrary",) * 2,
                                             vmem_limit_bytes=VMEM_LIMIT),
        name="attn_sample_fox" if fox else "attn_sample_mla",
    )(*args)


class _Item:
    def __init__(self, q, kb, qpar, kpar, first=False, last=False):
        self.q, self.kb, self.qpar, self.kpar, self.first, self.last = q, kb, qpar, kpar, first, last
        self.slot = 2 * qpar + kpar


def _attn_flat_kernel(*refs, tq, nq, fox):
    n_in = 4 if fox else 5
    q1_ref = refs[0]
    q2_ref = None if fox else refs[1]
    k1_ref, k2_ref, vt_ref, o_ref = refs[n_in - 3:n_in + 1]
    scratch = refs[n_in + 1:]
    qt_refs, m_refs, acc_refs = scratch[0:2], scratch[2:4], scratch[4:6]
    s_refs, p_refs, a_refs, bm_refs = scratch[6:10], scratch[10:14], scratch[14:18], scratch[18:22]
    tk = tq
    pair = pl.program_id(1)
    col_tiles = [slice(j, j + COL_TILE) for j in range(0, 2 * tq, COL_TILE)]
    ones_rows = jnp.ones((ONES_ROWS, tk), BF16)

    def rows_of(blk):
        start = blk * tq
        return pl.ds(start if isinstance(start, int) else pl.multiple_of(start, tq), tq)

    def setup(q, qpar):
        lane = lax.broadcasted_iota(jnp.int32, (tq, LANES), 1)
        q1 = q1_ref[rows_of(q), :].astype(F32)
        q2 = jnp.ones_like(q1) if fox else q2_ref[rows_of(q), :].astype(F32)

        def head_rows(a):
            main = jnp.where((lane >= a * HEAD_DIM) & (lane < (a + 1) * HEAD_DIM), q1, 0.0)
            if fox:
                lo, width = LOGF_LANE0 + 3 * (2 * pair + a), 3
            else:
                lo, width = ROPE_DIM * (2 * (pair % 2) + a), ROPE_DIM
            aux = jnp.where((lane >= lo) & (lane < lo + width), q2, 0.0)
            return jnp.concatenate([main, aux], axis=1)

        qt_refs[qpar][...] = jnp.concatenate([head_rows(0), head_rows(1)], axis=0).T.astype(BF16)

    def key_block(kb):
        return jnp.concatenate([k1_ref[rows_of(kb), :], k2_ref[rows_of(kb), :]], axis=1)

    def scores(x, cols):
        s = jnp.dot(key_block(x.kb), qt_refs[x.qpar][:, cols], preferred_element_type=F32)
        if x.last:
            c = (cols.start % tq) + lax.broadcasted_iota(jnp.int32, (1, COL_TILE), 1)
            r = lax.broadcasted_iota(jnp.int32, s.shape, 0)
            s = jnp.where(r <= (c if fox else (c | (CHUNK - 1))), s, NEG)
        s_refs[x.slot][cols.start // COL_TILE] = s
        bm_refs[x.slot][:, cols] = jnp.max(s, axis=0, keepdims=True)

    def softmax(x, cols):
        bm = bm_refs[x.slot][:, cols]
        if x.first:
            m_new = bm
            a_refs[x.slot][:, cols] = jnp.zeros_like(bm)
        else:
            m_prev = m_refs[x.qpar][:, cols]
            m_new = jnp.maximum(m_prev, bm)
            a_refs[x.slot][:, cols] = jnp.exp2(m_prev - m_new)
        m_refs[x.qpar][:, cols] = m_new
        ct = cols.start // COL_TILE
        p_refs[x.slot][ct] = jnp.exp2(s_refs[x.slot][ct] - m_new).astype(BF16)

    def values(x, cols):
        head = cols.start // tq
        v_t = vt_ref[x.kb, head * HEAD_DIM:(head + 1) * HEAD_DIM, :]
        lhs = jnp.concatenate([v_t, ones_rows], axis=0)
        pv = jnp.dot(lhs, p_refs[x.slot][cols.start // COL_TILE], preferred_element_type=F32)[:ACC_ROWS]
        acc_ref = acc_refs[x.qpar]
        if x.first:
            acc_ref[:, cols] = pv
        else:
            acc_ref[:, cols] = a_refs[x.slot][:, cols] * acc_ref[:, cols] + pv

    def finalize(x):
        acc = acc_refs[x.qpar][...]
        out_t = jnp.concatenate([acc[:HEAD_DIM, :tq] / acc[HEAD_DIM:HEAD_DIM + 1, :tq],
                                 acc[:HEAD_DIM, tq:] / acc[HEAD_DIM:HEAD_DIM + 1, tq:]], axis=0)
        o_ref[rows_of(x.q), :] = out_t.T.astype(o_ref.dtype)

    def tick(xs, xm, xv, next_q=None):
        for cols in col_tiles:
            if xs is not None:
                scores(xs, cols)
            if xv is not None:
                values(xv, cols)
            if xm is not None:
                softmax(xm, cols)
        if xv is not None and xv.last:
            finalize(xv)
        if next_q is not None:
            setup(*next_q)

    def generic_pairs(q, qpar, count):
        def body(i, carry):
            kb = 3 + 2 * i
            x_a, x_b = _Item(q, kb, qpar, 1), _Item(q, kb + 1, qpar, 0)
            tick(x_a, _Item(q, kb - 1, qpar, 0), _Item(q, kb - 2, qpar, 1))
            tick(x_b, x_a, _Item(q, kb - 1, qpar, 0))
            return carry
        lax.fori_loop(0, count, body, 0)

    def query_block(q, qpar, is_last_q=False):
        p2 = _Item(q - 1, q - 2, 1 - qpar, qpar)
        p1 = _Item(q - 1, q - 1, 1 - qpar, 1 - qpar, last=True)
        x0 = _Item(q, 0, qpar, 0, first=True)
        x1 = _Item(q, 1, qpar, 1)
        x2 = _Item(q, 2, qpar, 0)
        tick(x0, p1, p2)
        tick(x1, x0, p1)
        tick(x2, x1, x0)
        nxt = None if is_last_q else (q + 1, 1 - qpar)
        if qpar == 1:
            generic_pairs(q, qpar, (q - 3) // 2)
            a2, a1 = _Item(q, q - 2, qpar, 1), _Item(q, q - 1, qpar, 0)
        else:
            generic_pairs(q, qpar, (q - 4) // 2)
            a1 = _Item(q, q - 1, qpar, 1)
            a2 = _Item(q, q - 2, qpar, 0)
            tick(a1, a2, _Item(q, q - 3, qpar, 1))
        xl = _Item(q, q, qpar, qpar, last=True)
        tick(xl, a1, a2, next_q=nxt)
        return a1, xl

    setup(0, 0)
    x00 = _Item(0, 0, 0, 0, first=True, last=True)
    x10, x11 = _Item(1, 0, 1, 0, first=True), _Item(1, 1, 1, 1, last=True)
    x20, x21, x22 = _Item(2, 0, 0, 0, first=True), _Item(2, 1, 0, 1), _Item(2, 2, 0, 0, last=True)
    tick(x00, None, None, next_q=(1, 1))
    tick(x10, x00, None)
    tick(x11, x10, x00, next_q=(2, 0))
    tick(x20, x11, x10)
    tick(x21, x20, x11)
    tick(x22, x21, x20, next_q=(3, 1))

    def block_pair(j, carry):
        query_block(3 + 2 * j, 1)
        query_block(4 + 2 * j, 0)
        return carry

    lax.fori_loop(0, (nq - 4) // 2, block_pair, 0)
    a1, xl = query_block(nq - 1, 1, is_last_q=True)
    tick(None, xl, a1)
    tick(None, None, xl)


def _attn_flat(q1, q2, k1, k2, v_t, *, tq, fox):
    pairs, b, t, _ = q1.shape
    hw = pairs * LANES
    nq = t // tq
    assert t % tq == 0 and nq % 2 == 0 and nq >= 4, (t, tq)
    assert v_t.shape == (b, nq, hw, tq), v_t.shape
    m2 = 2 * tq
    spec = pl.BlockSpec((None, None, t, LANES), lambda bi, p: (p, bi, 0, 0))
    k2spec = pl.BlockSpec((None, t, LANES), lambda bi, p: (bi, 0, 0))
    vspec = pl.BlockSpec((None, nq, LANES, tq), lambda bi, p: (bi, 0, p, 0))
    if fox:
        in_specs = [spec, spec, k2spec, vspec]
        args = (q1, k1, k2, v_t)
    else:
        q2spec = pl.BlockSpec((None, None, t, LANES), lambda bi, p: (p // 2, bi, 0, 0))
        in_specs = [spec, q2spec, spec, k2spec, vspec]
        args = (q1, q2, k1, k2, v_t)
    vmem = lambda shape, dt, n: [pltpu.VMEM(shape, dt) for _ in range(n)]
    return pl.pallas_call(
        functools.partial(_attn_flat_kernel, tq=tq, nq=nq, fox=fox),
        out_shape=jax.ShapeDtypeStruct((pairs, b, t, LANES), BF16),
        grid=(b, pairs),
        in_specs=in_specs,
        out_specs=spec,
        scratch_shapes=(vmem((2 * LANES, m2), BF16, 2)
                        + vmem((1, m2), F32, 2)
                        + vmem((ACC_ROWS, m2), F32, 2)
                        + vmem((m2 // COL_TILE, tq, COL_TILE), F32, 4)
                        + vmem((m2 // COL_TILE, tq, COL_TILE), BF16, 4)
                        + vmem((1, m2), F32, 4)
                        + vmem((1, m2), F32, 4)),
        compiler_params=pltpu.CompilerParams(dimension_semantics=("arbitrary",) * 2,
                                             vmem_limit_bytes=VMEM_LIMIT),
        name="attn_flat_fox" if fox else "attn_flat_mla",
    )(*args)


def _rope_tables(pos):
    half = ROPE_DIM // 2
    freqs = ROPE_THETA ** (-jnp.arange(half, dtype=F32) / half)
    ang = pos.astype(F32)[:, None] * freqs[None, :]
    cos, sin = jnp.cos(ang), jnp.sin(ang)
    cos_t = jnp.tile(jnp.concatenate([cos, cos], axis=1), (1, HEADS))
    sin_t = jnp.tile(jnp.concatenate([-sin, sin], axis=1), (1, HEADS))
    return cos_t, sin_t


def _swap_halves(w):
    half = w.shape[-1] // 2
    return jnp.concatenate([w[..., half:], w[..., :half]], axis=-1)


def _prep_weights(w_in, b_forget, w_q_up, w_kv_up, w_out):
    d = w_in.shape[0]
    hw = HEADS * HEAD_DIM
    o = 0
    cq = w_in[:, o:o + Q_LORA]; o += Q_LORA
    ckv = w_in[:, o:o + KV_LORA]; o += KV_LORA
    kr = w_in[:, o:o + ROPE_DIM]; o += ROPE_DIM
    fq = w_in[:, o:o + hw]; o += hw
    fk = w_in[:, o:o + hw]; o += hw
    fv = w_in[:, o:o + hw]; o += hw
    gate = w_in[:, o:o + HEADS]
    reps = LANES // ROPE_DIM
    pad = LANES - 4 * HEADS
    gate128 = jnp.concatenate([gate, jnp.repeat(gate, 3, axis=1), jnp.zeros((d, pad), w_in.dtype)], axis=1)
    w_in_p = jnp.concatenate([cq, ckv, fq, fk, fv, jnp.tile(kr, (1, reps)), jnp.tile(_swap_halves(kr), (1, reps)),
                              gate128], axis=1).astype(BF16)
    bias128 = jnp.concatenate([b_forget, jnp.repeat(b_forget, 3), jnp.zeros((pad,), F32)])[None, :]
    wq = w_q_up.reshape(Q_LORA, HEADS, NOPE_DIM + ROPE_DIM)
    wq_rope = wq[:, :, NOPE_DIM:]
    w_q_p = jnp.concatenate([wq[:, :, :NOPE_DIM].reshape(Q_LORA, -1), wq_rope.reshape(Q_LORA, -1),
                             _swap_halves(wq_rope).reshape(Q_LORA, -1)], axis=1).astype(BF16)
    wkv = w_kv_up.reshape(KV_LORA, HEADS, NOPE_DIM + HEAD_DIM)
    w_kv_p = jnp.concatenate([wkv[:, :, :NOPE_DIM].reshape(KV_LORA, -1), wkv[:, :, NOPE_DIM:].reshape(KV_LORA, -1)],
                             axis=1).astype(BF16)
    w_o1 = w_out[:hw].astype(BF16)
    w_o2 = w_out[hw:].astype(BF16)
    return w_in_p, bias128, w_q_p, w_kv_p, w_o1, w_o2


def _expand_logf(lf):
    pad = LANES - 4 * HEADS
    return jnp.concatenate([lf, jnp.repeat(lf, 3, axis=-1), jnp.zeros(lf.shape[:-1] + (pad,), lf.dtype)], axis=-1)


def kernel(x_prompt, x_sample, cache_mla_ckv, cache_mla_krope, cache_fox_k, cache_fox_v, cache_fox_logf,
           g_ffn1_pre, g_ffn1_post, w_ffn1_gu, w_ffn1_down, g_mix_pre, g_mix_post, w_in, b_forget,
           g_q_latent, w_q_up, g_kv_latent, w_kv_up, w_out, g_ffn2_pre, g_ffn2_post, w_ffn2_gu, w_ffn2_down):
    depth = w_in.shape[0]
    bp, tp, d = x_prompt.shape
    bs, ts, _ = x_sample.shape
    past = cache_mla_ckv.shape[2]
    hw = HEADS * HEAD_DIM
    tq_p = tm_p = tc_s = ROW_TILE
    tm_s = bs * ts
    tk_pad = -(-(past + ts) // tc_s) * tc_s

    cos_p, sin_p = _rope_tables(jnp.arange(tp))
    cos_s, sin_s = _rope_tables(past + jnp.arange(ts))
    cos_s, sin_s = jnp.tile(cos_s, (bs, 1)), jnp.tile(sin_s, (bs, 1))

    xp = x_prompt.reshape(bp * tp, d)
    xs = x_sample.reshape(bs * ts, d)
    rows_p, rows_s = [], []

    def pad_keys(parts):
        n = sum(a.shape[1] for a in parts)
        parts = list(parts) + [jnp.zeros((bs, tk_pad - n, parts[0].shape[2]), parts[0].dtype)]
        return jnp.concatenate(parts, axis=1)

    for l in range(depth):
        d_ff = w_ffn1_down.shape[1]
        w1 = (w_ffn1_gu[l][:, :d_ff].astype(BF16), w_ffn1_gu[l][:, d_ff:].astype(BF16), w_ffn1_down[l].astype(BF16))
        w2 = (w_ffn2_gu[l][:, :d_ff].astype(BF16), w_ffn2_gu[l][:, d_ff:].astype(BF16), w_ffn2_down[l].astype(BF16))
        w_in_p, bias128, w_q_p, w_kv_p, w_o1, w_o2 = _prep_weights(w_in[l], b_forget[l], w_q_up[l], w_kv_up[l], w_out[l])
        g1 = (g_ffn1_pre[l][None, :], g_ffn1_post[l][None, :])
        g2 = (g_ffn2_pre[l][None, :], g_ffn2_post[l][None, :])
        gm_pre, gm_post = g_mix_pre[l][None, :], g_mix_post[l][None, :]
        gq, gkv = g_q_latent[l][None, :], g_kv_latent[l][None, :]

        hp = _ffn(xp, *g1, *w1, tm=tm_p)
        (ckv, krope, fk, fv, logf, qn, qr, kn, vm_t, kr4, fqb, fkb, fv_t, ls) = _proj(
            hp, cos_p, sin_p, gm_pre, w_in_p, bias128, gq, w_q_p, gkv, w_kv_p, w_kv_p[:, hw:].T, tm=tq_p)
        sh = lambda a: a.reshape(bp, tp, a.shape[-1])
        sh_g = lambda a: a.reshape(a.shape[0], bp, tp, LANES)
        sh_t = lambda a: a.reshape(bp, tp // tq_p, hw, tq_p)
        o_mla = _attn_flat(sh_g(qn), sh_g(qr), sh_g(kn), sh(kr4), sh_t(vm_t), tq=tq_p, fox=False)
        o_fox = _attn_flat(sh_g(fqb), None, sh_g(fkb), sh(ls), sh_t(fv_t), tq=tq_p, fox=True)
        un_g = lambda o: o.reshape(o.shape[0], bp * tp, LANES)
        xp = _ffn(hp, *g2, *w2, mix=(un_g(o_mla), un_g(o_fox), w_o1, w_o2, gm_post), tm=tm_p)
        rows_p.append((ckv.reshape(bp, tp, KV_LORA), krope.reshape(bp, tp, ROPE_DIM),
                       fk.reshape(bp, tp, HEADS, HEAD_DIM), fv.reshape(bp, tp, HEADS, HEAD_DIM),
                       logf.reshape(bp, tp, HEADS)))

        hs = _ffn(xs, *g1, *w1, tm=tm_s)
        (ckv, krope, fk, fv, logf, qn, qr, kn, vm, kr4, fqb, _, _, lf128) = _proj(
            hs, cos_s, sin_s, gm_pre, w_in_p, bias128, gq, w_q_p, gkv, w_kv_p, tm=tm_s)
        kn_past, vm_past, kr4_past = _pastkv(cache_mla_ckv[l].reshape(bs * past, KV_LORA),
                                             cache_mla_krope[l].reshape(bs * past, ROPE_DIM), w_kv_p,
                                             tm=math.gcd(bs * past, 4 * ROW_TILE))
        sh = lambda a: a.reshape(bs, -1, a.shape[-1])
        lf_all = pad_keys([_expand_logf(cache_fox_logf[l].astype(F32)), sh(lf128)])
        ls = _lsplit(lf_all, tc=tc_s)
        o_mla = _attn_sample(sh(qn), sh(qr), sh(kn_past), sh(kr4_past), sh(vm_past),
                             sh(kn), sh(kr4), sh(vm), fox=False)
        o_fox = _attn_sample(sh(fqb), None, cache_fox_k[l].reshape(bs, past, hw), ls,
                             cache_fox_v[l].reshape(bs, past, hw), sh(fk), ls, sh(fv),
                             k2n_row0=past, fox=True)
        xs = _ffn(hs, *g2, *w2, mix=(o_mla.reshape(-1, hw), o_fox.reshape(-1, hw), w_o1, w_o2, gm_post), tm=tm_s)
        rows_s.append((ckv.reshape(bs, ts, KV_LORA), krope.reshape(bs, ts, ROPE_DIM),
                       fk.reshape(bs, ts, HEADS, HEAD_DIM), fv.reshape(bs, ts, HEADS, HEAD_DIM),
                       logf.reshape(bs, ts, HEADS)))

    outs_p = [jnp.stack([r[i] for r in rows_p]) for i in range(5)]
    outs_s = [jnp.stack([r[i] for r in rows_s]) for i in range(5)]
    return (xp.reshape(bp, tp, d), xs.reshape(bs, ts, d), *outs_p, *outs_s)
```

```python
import functools
import math

import jax
import jax.numpy as jnp
from jax import lax
from jax.experimental import pallas as pl
from jax.experimental.pallas import tpu as pltpu

EPS = 1e-6
CHUNK = 64
ROPE_THETA = 10000.0
HEADS = 8
NOPE_DIM = 64
ROPE_DIM = 32
HEAD_DIM = 64
Q_LORA = 384
KV_LORA = 256
LANES = 128
COL_TILE = 256
ROW_TILE = 512
ONES_ROWS = 16
ACC_ROWS = HEAD_DIM + 8
LOG2E = 1.4426950408889634
NEG = -1e30
LOGF_LANE0 = 8
VMEM_LIMIT = 56 * 1024 * 1024

BF16 = jnp.bfloat16
F32 = jnp.float32


def _rms(x, g):
    ms = jnp.mean(x * x, axis=-1, keepdims=True)
    return x * lax.rsqrt(ms + EPS) * g


def _const_spec(shape):
    return pl.BlockSpec(shape, lambda *_: (0,) * len(shape), pipeline_mode=pl.Buffered(1))


def _ffn_kernel(*refs, ff_chunk, with_mix):
    if with_mix:
        (h_ref, o1_ref, o2_ref, wo1_ref, wo2_ref, gmix_ref,
         gpre_ref, gpost_ref, wg_ref, wu_ref, wd_ref, out_ref) = refs
        def rows(o_ref):
            if len(o_ref.shape) == 2:
                return o_ref[...]
            return jnp.concatenate([o_ref[g] for g in range(o_ref.shape[0])], axis=1)
        mix = jnp.dot(rows(o1_ref), wo1_ref[...], preferred_element_type=F32)
        mix = mix + jnp.dot(rows(o2_ref), wo2_ref[...], preferred_element_type=F32)
        x = h_ref[...] + _rms(mix, gmix_ref[...])
    else:
        x_ref, gpre_ref, gpost_ref, wg_ref, wu_ref, wd_ref, out_ref = refs
        x = x_ref[...]
    n = _rms(x, gpre_ref[...]).astype(BF16)
    d_ff = wg_ref.shape[1]
    acc = jnp.zeros(x.shape, F32)
    for c in range(d_ff // ff_chunk):
        cols = slice(c * ff_chunk, (c + 1) * ff_chunk)
        gate = jnp.dot(n, wg_ref[:, cols], preferred_element_type=F32)
        up = jnp.dot(n, wu_ref[:, cols], preferred_element_type=F32)
        act = (gate * jax.nn.sigmoid(gate) * up).astype(BF16)
        acc = acc + jnp.dot(act, wd_ref[cols, :], preferred_element_type=F32)
    out_ref[...] = x + 0.5 * _rms(acc, gpost_ref[...])


def _ffn(x, g_pre, g_post, w_g, w_u, w_d, mix=None, *, tm):
    n, d = x.shape
    d_ff = w_g.shape[1]
    row = lambda w: pl.BlockSpec((tm, w), lambda i: (i, 0))
    in_specs = [row(d)]
    args = [x]
    if mix is not None:
        o1, o2, wo1, wo2, g_mix = mix
        grouped = lambda o: pl.BlockSpec((o.shape[0], tm, LANES), lambda i: (0, i, 0))
        in_specs += [row(o.shape[1]) if o.ndim == 2 else grouped(o) for o in (o1, o2)]
        in_specs += [_const_spec(wo1.shape), _const_spec(wo2.shape), _const_spec((1, d))]
        args += [o1, o2, wo1, wo2, g_mix]
    in_specs += [_const_spec((1, d)), _const_spec((1, d)), _const_spec(w_g.shape), _const_spec(w_u.shape),
                 _const_spec(w_d.shape)]
    args += [g_pre, g_post, w_g, w_u, w_d]
    return pl.pallas_call(
        functools.partial(_ffn_kernel, ff_chunk=COL_TILE, with_mix=mix is not None),
        out_shape=jax.ShapeDtypeStruct((n, d), F32),
        grid=(n // tm,),
        in_specs=in_specs,
        out_specs=row(d),
        compiler_params=pltpu.CompilerParams(dimension_semantics=("arbitrary",), vmem_limit_bytes=VMEM_LIMIT),
        name="ffn_mix" if mix is not None else "ffn",
    )(*args)


_C_CQ = 0
_C_CKV = _C_CQ + Q_LORA
_C_FQ = _C_CKV + KV_LORA
_C_FK = _C_FQ + HEADS * HEAD_DIM
_C_FV = _C_FK + HEADS * HEAD_DIM
_C_KR = _C_FV + HEADS * HEAD_DIM
_C_KRS = _C_KR + LANES
_C_GATE = _C_KRS + LANES
_C_END = _C_GATE + LANES


def _log_sigmoid(x):
    return jnp.minimum(x, 0.0) - jnp.log1p(jnp.exp(-jnp.abs(x)))


def _proj_kernel(*refs, values_t, stream_blocks):
    (h_ref, cos_ref, sin_ref, gpre_ref, win_ref, bias_ref, gq_ref, wq_ref, gkv_ref, wkv_ref) = refs[:10]
    n_in = 11 if values_t else 10
    (ckv_ref, krope_ref, fk_ref, fv_ref, logf_ref,
     qn_ref, qr_ref, kn_ref, vm_ref, kr4_ref, fqb_ref, fkb_ref, fvb_ref, lf128_ref) = refs[n_in:n_in + 14]
    nt_dims = (((1,), (1,)), ((), ()))
    hw = HEADS * HEAD_DIM
    u = _rms(h_ref[...], gpre_ref[...]).astype(BF16)
    proj = jnp.dot(u, win_ref[...], preferred_element_type=F32)
    cos = cos_ref[...]
    sin = sin_ref[...]

    cq = _rms(proj[:, _C_CQ:_C_CKV], gq_ref[...]).astype(BF16)
    q = jnp.dot(cq, wq_ref[...], preferred_element_type=F32)
    q_scale = (NOPE_DIM + ROPE_DIM) ** -0.5 * LOG2E
    rw = HEADS * ROPE_DIM
    def put(ref, x):
        if values_t:
            for g in range(x.shape[1] // LANES):
                ref[g] = x[:, g * LANES:(g + 1) * LANES]
        else:
            ref[...] = x

    put(qn_ref, (q[:, :hw] * q_scale).astype(BF16))
    put(qr_ref, ((q[:, hw:hw + rw] * cos + q[:, hw + rw:] * sin) * q_scale).astype(BF16))

    ckv = _rms(proj[:, _C_CKV:_C_FQ], gkv_ref[...])
    ckv_ref[...] = ckv
    ckv_b = ckv.astype(BF16)
    if values_t:
        put(kn_ref, jnp.dot(ckv_b, wkv_ref[:, :hw], preferred_element_type=F32).astype(BF16))
        vm_ref[0] = lax.dot_general(refs[10][...], ckv_b, nt_dims, preferred_element_type=F32).astype(BF16)
    else:
        kv = jnp.dot(ckv_b, wkv_ref[...], preferred_element_type=F32)
        kn_ref[...] = kv[:, :hw].astype(BF16)
        vm_ref[...] = kv[:, hw:].astype(BF16)
    kr4 = proj[:, _C_KR:_C_KRS] * cos[:, :LANES] + proj[:, _C_KRS:_C_GATE] * sin[:, :LANES]
    krope_ref[...] = kr4[:, :ROPE_DIM]
    kr4_ref[...] = kr4.astype(BF16)

    fq = proj[:, _C_FQ:_C_FK]
    fk = proj[:, _C_FK:_C_FV]
    fv = proj[:, _C_FV:_C_KR]
    put(fqb_ref, (fq * (HEAD_DIM ** -0.5 * LOG2E)).astype(BF16))
    if values_t:
        def by_head(x):
            heads = jnp.stack([x[:, h * HEAD_DIM:(h + 1) * HEAD_DIM] for h in range(HEADS)], axis=0)
            return jnp.swapaxes(heads, 0, 1)
        fk_ref[...] = by_head(fk)
        fv_ref[...] = by_head(fv)
    else:
        fk_ref[...] = fk
        fv_ref[...] = fv
    put(fkb_ref, fk.astype(BF16))
    if values_t:
        fvb_ref[0] = fv.T.astype(BF16)
    else:
        fvb_ref[...] = fv.astype(BF16)
    logf = _log_sigmoid(proj[:, _C_GATE:_C_END] + bias_ref[...])
    logf_ref[...] = logf[:, :HEADS]
    if values_t:
        carry_ref = refs[-1]

        @pl.when(pl.program_id(0) % stream_blocks == 0)
        def _():
            carry_ref[...] = jnp.zeros_like(carry_ref)

        lf128_ref[...] = _cumsum_split(logf, carry_ref)
    else:
        lf128_ref[...] = logf


def _proj(h, cos, sin, g_pre, w_in, bias, g_q, w_q, g_kv, w_kv, w_v_t=None, *, tm):
    n, d = h.shape
    t_blocks = cos.shape[0] // tm
    hw = HEADS * HEAD_DIM
    values_t = w_v_t is not None
    row = lambda w: pl.BlockSpec((tm, w), lambda i: (i, 0))
    tab = pl.BlockSpec((tm, HEADS * ROPE_DIM), lambda i: (i % t_blocks, 0))
    widths = [(KV_LORA, F32), (ROPE_DIM, F32), (hw, F32), (hw, F32), (HEADS, F32),
              (hw, BF16), (HEADS * ROPE_DIM, BF16), (hw, BF16), (hw, BF16), (LANES, BF16),
              (hw, BF16), (hw, BF16), (hw, BF16), (LANES, F32)]
    out_shape = [jax.ShapeDtypeStruct((n, w), dt) for w, dt in widths]
    out_specs = [row(w) for w, _ in widths]
    in_specs = [row(d), tab, tab, _const_spec((1, d)), _const_spec(w_in.shape), _const_spec((1, LANES)),
                _const_spec((1, Q_LORA)), _const_spec(w_q.shape), _const_spec((1, KV_LORA)),
                _const_spec(w_kv.shape)]
    args = [h, cos, sin, g_pre, w_in, bias, g_q, w_q, g_kv, w_kv]
    if values_t:
        for i in (8, 12):
            out_shape[i] = jax.ShapeDtypeStruct((n // tm, hw, tm), BF16)
            out_specs[i] = pl.BlockSpec((1, hw, tm), lambda i: (i, 0, 0))
        out_shape[13] = jax.ShapeDtypeStruct((n, LANES), BF16)
        for i in (5, 6, 7, 10, 11):
            groups = widths[i][0] // LANES
            out_shape[i] = jax.ShapeDtypeStruct((groups, n, LANES), BF16)
            out_specs[i] = pl.BlockSpec((groups, tm, LANES), lambda i: (0, i, 0))
        for i in (2, 3):
            out_shape[i] = jax.ShapeDtypeStruct((n, HEADS, HEAD_DIM), F32)
            out_specs[i] = pl.BlockSpec((tm, HEADS, HEAD_DIM), lambda i: (i, 0, 0))
        in_specs.append(_const_spec(w_v_t.shape))
        args.append(w_v_t)
    return pl.pallas_call(
        functools.partial(_proj_kernel, values_t=values_t, stream_blocks=t_blocks),
        out_shape=out_shape,
        grid=(n // tm,),
        in_specs=in_specs,
        out_specs=out_specs,
        scratch_shapes=[pltpu.VMEM((1, LANES), F32)] if values_t else [],
        compiler_params=pltpu.CompilerParams(dimension_semantics=("arbitrary",), vmem_limit_bytes=VMEM_LIMIT),
        name="proj",
    )(*args)


def _pastkv_kernel(ckv_ref, kr_ref, wkv_ref, kn_ref, vm_ref, kr4_ref):
    hw = HEADS * HEAD_DIM
    kv = jnp.dot(ckv_ref[...].astype(BF16), wkv_ref[...], preferred_element_type=F32)
    kn_ref[...] = kv[:, :hw].astype(BF16)
    vm_ref[...] = kv[:, hw:].astype(BF16)
    src = lax.broadcasted_iota(jnp.int32, (ROPE_DIM, LANES), 0)
    dst = lax.broadcasted_iota(jnp.int32, (ROPE_DIM, LANES), 1)
    rep = (dst % ROPE_DIM == src).astype(BF16)
    kr4_ref[...] = jnp.dot(kr_ref[...].astype(BF16), rep, preferred_element_type=F32).astype(BF16)


def _pastkv(ckv, krope, w_kv, *, tm):
    n = ckv.shape[0]
    hw = HEADS * HEAD_DIM
    row = lambda w: pl.BlockSpec((tm, w), lambda i: (i, 0))
    return pl.pallas_call(
        _pastkv_kernel,
        out_shape=[jax.ShapeDtypeStruct((n, hw), BF16), jax.ShapeDtypeStruct((n, hw), BF16),
                   jax.ShapeDtypeStruct((n, LANES), BF16)],
        grid=(n // tm,),
        in_specs=[row(KV_LORA), row(ROPE_DIM), _const_spec(w_kv.shape)],
        out_specs=[row(hw), row(hw), row(LANES)],
        compiler_params=pltpu.CompilerParams(dimension_semantics=("arbitrary",), vmem_limit_bytes=VMEM_LIMIT),
        name="pastkv",
    )(ckv, krope, w_kv)


def _split3(y, lane):
    hi = y.astype(BF16).astype(F32)
    r1 = y - hi
    mid = r1.astype(BF16).astype(F32)
    lo = r1 - mid
    j = (lane - LOGF_LANE0) % 3
    sel = jnp.where(j == 0, hi, jnp.where(j == 1, mid, lo))
    used = (lane >= LOGF_LANE0) & (lane < LOGF_LANE0 + 3 * HEADS)
    return jnp.where(used, sel, 0.0).astype(BF16)


def _cumsum_split(x, carry_ref):
    tc = x.shape[0]
    hi = x.astype(BF16)
    r1 = x - hi.astype(F32)
    mid = r1.astype(BF16)
    lo = (r1 - mid.astype(F32)).astype(BF16)
    r = lax.broadcasted_iota(jnp.int32, (tc, tc), 0)
    c = lax.broadcasted_iota(jnp.int32, (tc, tc), 1)
    tri = (c <= r).astype(BF16)
    cum = (jnp.dot(tri, hi, preferred_element_type=F32) + jnp.dot(tri, mid, preferred_element_type=F32)
           + jnp.dot(tri, lo, preferred_element_type=F32)) + carry_ref[...]
    carry_ref[...] = cum[tc - 1:tc, :]
    lane = lax.broadcasted_iota(jnp.int32, (tc, LANES), 1)
    return _split3(cum * (-LOG2E), lane)


def _lsplit_kernel(x_ref, o_ref, carry_ref):
    @pl.when(pl.program_id(1) == 0)
    def _():
        carry_ref[...] = jnp.zeros_like(carry_ref)

    o_ref[0] = _cumsum_split(x_ref[0], carry_ref)


def _lsplit(lf128, *, tc):
    b, t, _ = lf128.shape
    spec = pl.BlockSpec((1, tc, LANES), lambda i, j: (i, j, 0))
    return pl.pallas_call(
        _lsplit_kernel,
        out_shape=jax.ShapeDtypeStruct((b, t, LANES), BF16),
        grid=(b, t // tc),
        in_specs=[spec],
        out_specs=spec,
        scratch_shapes=[pltpu.VMEM((1, LANES), F32)],
        compiler_params=pltpu.CompilerParams(dimension_semantics=("arbitrary", "arbitrary")),
        name="lsplit",
    )(lf128)


def _attn_sample_kernel(*refs, past, fox):
    n_q = 1 if fox else 2
    q1_ref = refs[0]
    q2_ref = None if fox else refs[1]
    k1p_ref, k2p_ref, vp_ref, k1n_ref, k2n_ref, vn_ref, o_ref = refs[n_q:]
    ts = q1_ref.shape[1]
    p = pl.program_id(1)
    lane = lax.broadcasted_iota(jnp.int32, (ts, LANES), 1)
    q1 = q1_ref[0].astype(F32)
    q2 = jnp.ones_like(q1) if fox else q2_ref[0].astype(F32)

    def head_rows(a):
        main = jnp.where((lane >= a * HEAD_DIM) & (lane < (a + 1) * HEAD_DIM), q1, 0.0)
        if fox:
            lo, width = LOGF_LANE0 + 3 * (2 * p + a), 3
        else:
            lo, width = ROPE_DIM * (2 * (p % 2) + a), ROPE_DIM
        aux = jnp.where((lane >= lo) & (lane < lo + width), q2, 0.0)
        return jnp.concatenate([main, aux], axis=1).astype(BF16)

    q = jnp.concatenate([head_rows(0), head_rows(1)], axis=0)
    nt_dims = (((1,), (1,)), ((), ()))

    def logits(k1_ref, k2_ref):
        k = jnp.concatenate([k1_ref[0].astype(BF16), k2_ref[0].astype(BF16)], axis=1)
        return lax.dot_general(q, k, nt_dims, preferred_element_type=F32)

    s_p = logits(k1p_ref, k2p_ref)
    s_n = logits(k1n_ref, k2n_ref)
    r = lax.broadcasted_iota(jnp.int32, s_n.shape, 0)
    t_pos = past + jnp.where(r >= ts, r - ts, r)
    s_pos = past + lax.broadcasted_iota(jnp.int32, s_n.shape, 1)
    vis = (s_pos <= t_pos) if fox else ((s_pos // CHUNK) <= (t_pos // CHUNK))
    s_n = jnp.where(vis, s_n, NEG)
    m = jnp.maximum(jnp.max(s_p, axis=1, keepdims=True), jnp.max(s_n, axis=1, keepdims=True))
    p_p = jnp.exp2(s_p - m)
    p_n = jnp.exp2(s_n - m)
    denom = jnp.sum(p_p, axis=1, keepdims=True) + jnp.sum(p_n, axis=1, keepdims=True)
    out = (jnp.dot(p_p.astype(BF16), vp_ref[0].astype(BF16), preferred_element_type=F32)
           + jnp.dot(p_n.astype(BF16), vn_ref[0].astype(BF16), preferred_element_type=F32)) / denom
    o_ref[0] = jnp.where(lane < HEAD_DIM, out[:ts], out[ts:]).astype(o_ref.dtype)


def _attn_sample(q1, q2, k1p, k2p, vp, k1n, k2n, vn, *, k2n_row0=0, fox):
    b, ts, hw = q1.shape
    past = k1p.shape[1]
    assert k2n_row0 % ts == 0
    pairs = hw // LANES
    qspec = pl.BlockSpec((1, ts, LANES), lambda bi, p: (bi, 0, p))
    pspec = pl.BlockSpec((1, past, LANES), lambda bi, p: (bi, 0, p))
    p2spec = pl.BlockSpec((1, past, LANES), lambda bi, p: (bi, 0, 0))
    n2spec = pl.BlockSpec((1, ts, LANES), lambda bi, p: (bi, k2n_row0 // ts, 0))
    in_specs = [pspec, p2spec, pspec, qspec, n2spec, qspec]
    args = (k1p, k2p, vp, k1n, k2n, vn)
    if fox:
        in_specs, args = [qspec] + in_specs, (q1,) + args
    else:
        q2spec = pl.BlockSpec((1, ts, LANES), lambda bi, p: (bi, 0, p // 2))
        in_specs, args = [qspec, q2spec] + in_specs, (q1, q2) + args
    return pl.pallas_call(
        functools.partial(_attn_sample_kernel, past=past, fox=fox),
        out_shape=jax.ShapeDtypeStruct((b, ts, hw), BF16),
        grid=(b, pairs),
        in_specs=in_specs,
        out_specs=qspec,
        compiler_params=pltpu.CompilerParams(dimension_semantics=("arbitrary",) * 2,
                                             vmem_limit_bytes=VMEM_LIMIT),
        name="attn_sample_fox" if fox else "attn_sample_mla",
    )(*args)


class _Item:
    def __init__(self, q, kb, qpar, kpar, first=False, last=False):
        self.q, self.kb, self.qpar, self.kpar, self.first, self.last = q, kb, qpar, kpar, first, last
        self.slot = 2 * qpar + kpar


def _attn_flat_kernel(*refs, tq, nq, fox):
    n_in = 4 if fox else 5
    q1_ref = refs[0]
    q2_ref = None if fox else refs[1]
    k1_ref, k2_ref, vt_ref, o_ref = refs[n_in - 3:n_in + 1]
    scratch = refs[n_in + 1:]
    qt_refs, m_refs, acc_refs = scratch[0:2], scratch[2:4], scratch[4:6]
    s_refs, p_refs, a_refs, bm_refs = scratch[6:10], scratch[10:14], scratch[14:18], scratch[18:22]
    tk = tq
    pair = pl.program_id(1)
    col_tiles = [slice(j, j + COL_TILE) for j in range(0, 2 * tq, COL_TILE)]
    ones_rows = jnp.ones((ONES_ROWS, tk), BF16)

    def rows_of(blk):
        start = blk * tq
        return pl.ds(start if isinstance(start, int) else pl.multiple_of(start, tq), tq)

    def setup(q, qpar):
        lane = lax.broadcasted_iota(jnp.int32, (tq, LANES), 1)
        q1 = q1_ref[rows_of(q), :].astype(F32)
        q2 = jnp.ones_like(q1) if fox else q2_ref[rows_of(q), :].astype(F32)

        def head_rows(a):
            main = jnp.where((lane >= a * HEAD_DIM) & (lane < (a + 1) * HEAD_DIM), q1, 0.0)
            if fox:
                lo, width = LOGF_LANE0 + 3 * (2 * pair + a), 3
            else:
                lo, width = ROPE_DIM * (2 * (pair % 2) + a), ROPE_DIM
            aux = jnp.where((lane >= lo) & (lane < lo + width), q2, 0.0)
            return jnp.concatenate([main, aux], axis=1)

        qt_refs[qpar][...] = jnp.concatenate([head_rows(0), head_rows(1)], axis=0).T.astype(BF16)

    def key_block(kb):
        return jnp.concatenate([k1_ref[rows_of(kb), :], k2_ref[rows_of(kb), :]], axis=1)

    def scores(x, cols):
        s = jnp.dot(key_block(x.kb), qt_refs[x.qpar][:, cols], preferred_element_type=F32)
        if x.last:
            c = (cols.start % tq) + lax.broadcasted_iota(jnp.int32, (1, COL_TILE), 1)
            r = lax.broadcasted_iota(jnp.int32, s.shape, 0)
            s = jnp.where(r <= (c if fox else (c | (CHUNK - 1))), s, NEG)
        s_refs[x.slot][cols.start // COL_TILE] = s
        bm_refs[x.slot][:, cols] = jnp.max(s, axis=0, keepdims=True)

    def softmax(x, cols):
        bm = bm_refs[x.slot][:, cols]
        if x.first:
            m_new = bm
            a_refs[x.slot][:, cols] = jnp.zeros_like(bm)
        else:
            m_prev = m_refs[x.qpar][:, cols]
            m_new = jnp.maximum(m_prev, bm)
            a_refs[x.slot][:, cols] = jnp.exp2(m_prev - m_new)
        m_refs[x.qpar][:, cols] = m_new
        ct = cols.start // COL_TILE
        p_refs[x.slot][ct] = jnp.exp2(s_refs[x.slot][ct] - m_new).astype(BF16)

    def values(x, cols):
        head = cols.start // tq
        v_t = vt_ref[x.kb, head * HEAD_DIM:(head + 1) * HEAD_DIM, :]
        lhs = jnp.concatenate([v_t, ones_rows], axis=0)
        pv = jnp.dot(lhs, p_refs[x.slot][cols.start // COL_TILE], preferred_element_type=F32)[:ACC_ROWS]
        acc_ref = acc_refs[x.qpar]
        if x.first:
            acc_ref[:, cols] = pv
        else:
            acc_ref[:, cols] = a_refs[x.slot][:, cols] * acc_ref[:, cols] + pv

    def finalize(x):
        acc = acc_refs[x.qpar][...]
        out_t = jnp.concatenate([acc[:HEAD_DIM, :tq] / acc[HEAD_DIM:HEAD_DIM + 1, :tq],
                                 acc[:HEAD_DIM, tq:] / acc[HEAD_DIM:HEAD_DIM + 1, tq:]], axis=0)
        o_ref[rows_of(x.q), :] = out_t.T.astype(o_ref.dtype)

    def tick(xs, xm, xv, next_q=None):
        for cols in col_tiles:
            if xs is not None:
                scores(xs, cols)
            if xv is not None:
                values(xv, cols)
            if xm is not None:
                softmax(xm, cols)
        if xv is not None and xv.last:
            finalize(xv)
        if next_q is not None:
            setup(*next_q)

    def generic_pairs(q, qpar, count):
        def body(i, carry):
            kb = 3 + 2 * i
            x_a, x_b = _Item(q, kb, qpar, 1), _Item(q, kb + 1, qpar, 0)
            tick(x_a, _Item(q, kb - 1, qpar, 0), _Item(q, kb - 2, qpar, 1))
            tick(x_b, x_a, _Item(q, kb - 1, qpar, 0))
            return carry
        lax.fori_loop(0, count, body, 0)

    def query_block(q, qpar, is_last_q=False):
        p2 = _Item(q - 1, q - 2, 1 - qpar, qpar)
        p1 = _Item(q - 1, q - 1, 1 - qpar, 1 - qpar, last=True)
        x0 = _Item(q, 0, qpar, 0, first=True)
        x1 = _Item(q, 1, qpar, 1)
        x2 = _Item(q, 2, qpar, 0)
        tick(x0, p1, p2)
        tick(x1, x0, p1)
        tick(x2, x1, x0)
        nxt = None if is_last_q else (q + 1, 1 - qpar)
        if qpar == 1:
            generic_pairs(q, qpar, (q - 3) // 2)
            a2, a1 = _Item(q, q - 2, qpar, 1), _Item(q, q - 1, qpar, 0)
        else:
            generic_pairs(q, qpar, (q - 4) // 2)
            a1 = _Item(q, q - 1, qpar, 1)
            a2 = _Item(q, q - 2, qpar, 0)
            tick(a1, a2, _Item(q, q - 3, qpar, 1))
        xl = _Item(q, q, qpar, qpar, last=True)
        tick(xl, a1, a2, next_q=nxt)
        return a1, xl

    setup(0, 0)
    x00 = _Item(0, 0, 0, 0, first=True, last=True)
    x10, x11 = _Item(1, 0, 1, 0, first=True), _Item(1, 1, 1, 1, last=True)
    x20, x21, x22 = _Item(2, 0, 0, 0, first=True), _Item(2, 1, 0, 1), _Item(2, 2, 0, 0, last=True)
    tick(x00, None, None, next_q=(1, 1))
    tick(x10, x00, None)
    tick(x11, x10, x00, next_q=(2, 0))
    tick(x20, x11, x10)
    tick(x21, x20, x11)
    tick(x22, x21, x20, next_q=(3, 1))

    def block_pair(j, carry):
        query_block(3 + 2 * j, 1)
        query_block(4 + 2 * j, 0)
        return carry

    lax.fori_loop(0, (nq - 4) // 2, block_pair, 0)
    a1, xl = query_block(nq - 1, 1, is_last_q=True)
    tick(None, xl, a1)
    tick(None, None, xl)


def _attn_flat(q1, q2, k1, k2, v_t, *, tq, fox):
    pairs, b, t, _ = q1.shape
    hw = pairs * LANES
    nq = t // tq
    assert t % tq == 0 and nq % 2 == 0 and nq >= 4, (t, tq)
    assert v_t.shape == (b, nq, hw, tq), v_t.shape
    m2 = 2 * tq
    spec = pl.BlockSpec((None, None, t, LANES), lambda bi, p: (p, bi, 0, 0))
    k2spec = pl.BlockSpec((None, t, LANES), lambda bi, p: (bi, 0, 0))
    vspec = pl.BlockSpec((None, nq, LANES, tq), lambda bi, p: (bi, 0, p, 0))
    if fox:
        in_specs = [spec, spec, k2spec, vspec]
        args = (q1, k1, k2, v_t)
    else:
        q2spec = pl.BlockSpec((None, None, t, LANES), lambda bi, p: (p // 2, bi, 0, 0))
        in_specs = [spec, q2spec, spec, k2spec, vspec]
        args = (q1, q2, k1, k2, v_t)
    vmem = lambda shape, dt, n: [pltpu.VMEM(shape, dt) for _ in range(n)]
    return pl.pallas_call(
        functools.partial(_attn_flat_kernel, tq=tq, nq=nq, fox=fox),
        out_shape=jax.ShapeDtypeStruct((pairs, b, t, LANES), BF16),
        grid=(b, pairs),
        in_specs=in_specs,
        out_specs=spec,
        scratch_shapes=(vmem((2 * LANES, m2), BF16, 2)
                        + vmem((1, m2), F32, 2)
                        + vmem((ACC_ROWS, m2), F32, 2)
                        + vmem((m2 // COL_TILE, tq, COL_TILE), F32, 4)
                        + vmem((m2 // COL_TILE, tq, COL_TILE), BF16, 4)
                        + vmem((1, m2), F32, 4)
                        + vmem((1, m2), F32, 4)),
        compiler_params=pltpu.CompilerParams(dimension_semantics=("arbitrary",) * 2,
                                             vmem_limit_bytes=VMEM_LIMIT),
        name="attn_flat_fox" if fox else "attn_flat_mla",
    )(*args)


def _rope_tables(pos):
    half = ROPE_DIM // 2
    freqs = ROPE_THETA ** (-jnp.arange(half, dtype=F32) / half)
    ang = pos.astype(F32)[:, None] * freqs[None, :]
    cos, sin = jnp.cos(ang), jnp.sin(ang)
    cos_t = jnp.tile(jnp.concatenate([cos, cos], axis=1), (1, HEADS))
    sin_t = jnp.tile(jnp.concatenate([-sin, sin], axis=1), (1, HEADS))
    return cos_t, sin_t


def _swap_halves(w):
    half = w.shape[-1] // 2
    return jnp.concatenate([w[..., half:], w[..., :half]], axis=-1)


def _prep_weights(w_in, b_forget, w_q_up, w_kv_up, w_out):
    d = w_in.shape[0]
    hw = HEADS * HEAD_DIM
    o = 0
    cq = w_in[:, o:o + Q_LORA]; o += Q_LORA
    ckv = w_in[:, o:o + KV_LORA]; o += KV_LORA
    kr = w_in[:, o:o + ROPE_DIM]; o += ROPE_DIM
    fq = w_in[:, o:o + hw]; o += hw
    fk = w_in[:, o:o + hw]; o += hw
    fv = w_in[:, o:o + hw]; o += hw
    gate = w_in[:, o:o + HEADS]
    reps = LANES // ROPE_DIM
    pad = LANES - 4 * HEADS
    gate128 = jnp.concatenate([gate, jnp.repeat(gate, 3, axis=1), jnp.zeros((d, pad), w_in.dtype)], axis=1)
    w_in_p = jnp.concatenate([cq, ckv, fq, fk, fv, jnp.tile(kr, (1, reps)), jnp.tile(_swap_halves(kr), (1, reps)),
                              gate128], axis=1).astype(BF16)
    bias128 = jnp.concatenate([b_forget, jnp.repeat(b_forget, 3), jnp.zeros((pad,), F32)])[None, :]
    wq = w_q_up.reshape(Q_LORA, HEADS, NOPE_DIM + ROPE_DIM)
    wq_rope = wq[:, :, NOPE_DIM:]
    w_q_p = jnp.concatenate([wq[:, :, :NOPE_DIM].reshape(Q_LORA, -1), wq_rope.reshape(Q_LORA, -1),
                             _swap_halves(wq_rope).reshape(Q_LORA, -1)], axis=1).astype(BF16)
    wkv = w_kv_up.reshape(KV_LORA, HEADS, NOPE_DIM + HEAD_DIM)
    w_kv_p = jnp.concatenate([wkv[:, :, :NOPE_DIM].reshape(KV_LORA, -1), wkv[:, :, NOPE_DIM:].reshape(KV_LORA, -1)],
                             axis=1).astype(BF16)
    w_o1 = w_out[:hw].astype(BF16)
    w_o2 = w_out[hw:].astype(BF16)
    return w_in_p, bias128, w_q_p, w_kv_p, w_o1, w_o2


def _expand_logf(lf):
    pad = LANES - 4 * HEADS
    return jnp.concatenate([lf, jnp.repeat(lf, 3, axis=-1), jnp.zeros(lf.shape[:-1] + (pad,), lf.dtype)], axis=-1)


def kernel(x_prompt, x_sample, cache_mla_ckv, cache_mla_krope, cache_fox_k, cache_fox_v, cache_fox_logf,
           g_ffn1_pre, g_ffn1_post, w_ffn1_gu, w_ffn1_down, g_mix_pre, g_mix_post, w_in, b_forget,
           g_q_latent, w_q_up, g_kv_latent, w_kv_up, w_out, g_ffn2_pre, g_ffn2_post, w_ffn2_gu, w_ffn2_down):
    depth = w_in.shape[0]
    bp, tp, d = x_prompt.shape
    bs, ts, _ = x_sample.shape
    past = cache_mla_ckv.shape[2]
    hw = HEADS * HEAD_DIM
    tq_p = tc_s = ROW_TILE
    tm_p = math.gcd(bp * tp, 2 * ROW_TILE)
    tm_s = bs * ts
    tk_pad = -(-(past + ts) // tc_s) * tc_s

    cos_p, sin_p = _rope_tables(jnp.arange(tp))
    cos_s, sin_s = _rope_tables(past + jnp.arange(ts))
    cos_s, sin_s = jnp.tile(cos_s, (bs, 1)), jnp.tile(sin_s, (bs, 1))

    xp = x_prompt.reshape(bp * tp, d)
    xs = x_sample.reshape(bs * ts, d)
    rows_p, rows_s = [], []

    def pad_keys(parts):
        n = sum(a.shape[1] for a in parts)
        parts = list(parts) + [jnp.zeros((bs, tk_pad - n, parts[0].shape[2]), parts[0].dtype)]
        return jnp.concatenate(parts, axis=1)

    for l in range(depth):
        d_ff = w_ffn1_down.shape[1]
        w1 = (w_ffn1_gu[l][:, :d_ff].astype(BF16), w_ffn1_gu[l][:, d_ff:].astype(BF16), w_ffn1_down[l].astype(BF16))
        w2 = (w_ffn2_gu[l][:, :d_ff].astype(BF16), w_ffn2_gu[l][:, d_ff:].astype(BF16), w_ffn2_down[l].astype(BF16))
        w_in_p, bias128, w_q_p, w_kv_p, w_o1, w_o2 = _prep_weights(w_in[l], b_forget[l], w_q_up[l], w_kv_up[l], w_out[l])
        g1 = (g_ffn1_pre[l][None, :], g_ffn1_post[l][None, :])
        g2 = (g_ffn2_pre[l][None, :], g_ffn2_post[l][None, :])
        gm_pre, gm_post = g_mix_pre[l][None, :], g_mix_post[l][None, :]
        gq, gkv = g_q_latent[l][None, :], g_kv_latent[l][None, :]

        hp = _ffn(xp, *g1, *w1, tm=tm_p)
        (ckv, krope, fk, fv, logf, qn, qr, kn, vm_t, kr4, fqb, fkb, fv_t, ls) = _proj(
            hp, cos_p, sin_p, gm_pre, w_in_p, bias128, gq, w_q_p, gkv, w_kv_p, w_kv_p[:, hw:].T, tm=tq_p)
        sh = lambda a: a.reshape(bp, tp, a.shape[-1])
        sh_g = lambda a: a.reshape(a.shape[0], bp, tp, LANES)
        sh_t = lambda a: a.reshape(bp, tp // tq_p, hw, tq_p)
        o_mla = _attn_flat(sh_g(qn), sh_g(qr), sh_g(kn), sh(kr4), sh_t(vm_t), tq=tq_p, fox=False)
        o_fox = _attn_flat(sh_g(fqb), None, sh_g(fkb), sh(ls), sh_t(fv_t), tq=tq_p, fox=True)
        un_g = lambda o: o.reshape(o.shape[0], bp * tp, LANES)
        xp = _ffn(hp, *g2, *w2, mix=(un_g(o_mla), un_g(o_fox), w_o1, w_o2, gm_post), tm=tm_p)
        rows_p.append((ckv.reshape(bp, tp, KV_LORA), krope.reshape(bp, tp, ROPE_DIM),
                       fk.reshape(bp, tp, HEADS, HEAD_DIM), fv.reshape(bp, tp, HEADS, HEAD_DIM),
                       logf.reshape(bp, tp, HEADS)))

        hs = _ffn(xs, *g1, *w1, tm=tm_s)
        (ckv, krope, fk, fv, logf, qn, qr, kn, vm, kr4, fqb, _, _, lf128) = _proj(
            hs, cos_s, sin_s, gm_pre, w_in_p, bias128, gq, w_q_p, gkv, w_kv_p, tm=tm_s)
        kn_past, vm_past, kr4_past = _pastkv(cache_mla_ckv[l].reshape(bs * past, KV_LORA),
                                             cache_mla_krope[l].reshape(bs * past, ROPE_DIM), w_kv_p,
                                             tm=math.gcd(bs * past, 4 * ROW_TILE))
        sh = lambda a: a.reshape(bs, -1, a.shape[-1])
        lf_all = pad_keys([_expand_logf(cache_fox_logf[l].astype(F32)), sh(lf128)])
        ls = _lsplit(lf_all, tc=tc_s)
        o_mla = _attn_sample(sh(qn), sh(qr), sh(kn_past), sh(kr4_past), sh(vm_past),
                             sh(kn), sh(kr4), sh(vm), fox=False)
        o_fox = _attn_sample(sh(fqb), None, cache_fox_k[l].reshape(bs, past, hw), ls,
                             cache_fox_v[l].reshape(bs, past, hw), sh(fk), ls, sh(fv),
                             k2n_row0=past, fox=True)
        xs = _ffn(hs, *g2, *w2, mix=(o_mla.reshape(-1, hw), o_fox.reshape(-1, hw), w_o1, w_o2, gm_post), tm=tm_s)
        rows_s.append((ckv.reshape(bs, ts, KV_LORA), krope.reshape(bs, ts, ROPE_DIM),
                       fk.reshape(bs, ts, HEADS, HEAD_DIM), fv.reshape(bs, ts, HEADS, HEAD_DIM),
                       logf.reshape(bs, ts, HEADS)))

    outs_p = [jnp.stack([r[i] for r in rows_p]) for i in range(5)]
    outs_s = [jnp.stack([r[i] for r in rows_s]) for i in range(5)]
    return (xp.reshape(bp, tp, d), xs.reshape(bs, ts, d), *outs_p, *outs_s)
```

```python
import functools
import math

import jax
import jax.numpy as jnp
from jax import lax
from jax.experimental import pallas as pl
from jax.experimental.pallas import tpu as pltpu

EPS = 1e-6
CHUNK = 64
ROPE_THETA = 10000.0
HEADS = 8
NOPE_DIM = 64
ROPE_DIM = 32
HEAD_DIM = 64
Q_LORA = 384
KV_LORA = 256
LANES = 128
COL_TILE = 256
ROW_TILE = 512
ONES_ROWS = 16
ACC_ROWS = HEAD_DIM + 8
LOG2E = 1.4426950408889634
NEG = -1e30
LOGF_LANE0 = 8
VMEM_LIMIT = 56 * 1024 * 1024

BF16 = jnp.bfloat16
F32 = jnp.float32


def _rms(x, g):
    ms = jnp.mean(x * x, axis=-1, keepdims=True)
    return x * lax.rsqrt(ms + EPS) * g


def _const_spec(shape):
    return pl.BlockSpec(shape, lambda *_: (0,) * len(shape), pipeline_mode=pl.Buffered(1))


def _ffn_kernel(*refs, ff_chunk, with_mix):
    if with_mix:
        (h_ref, o1_ref, o2_ref, wo1_ref, wo2_ref, gmix_ref,
         gpre_ref, gpost_ref, wg_ref, wu_ref, wd_ref, out_ref) = refs
        def rows(o_ref):
            if len(o_ref.shape) == 2:
                return o_ref[...]
            return jnp.concatenate([o_ref[g] for g in range(o_ref.shape[0])], axis=1)
        mix = jnp.dot(rows(o1_ref), wo1_ref[...], preferred_element_type=F32)
        mix = mix + jnp.dot(rows(o2_ref), wo2_ref[...], preferred_element_type=F32)
        x = h_ref[...] + _rms(mix, gmix_ref[...])
    else:
        x_ref, gpre_ref, gpost_ref, wg_ref, wu_ref, wd_ref, out_ref = refs
        x = x_ref[...]
    n = _rms(x, gpre_ref[...]).astype(BF16)
    d_ff = wg_ref.shape[1]
    acc = jnp.zeros(x.shape, F32)
    for c in range(d_ff // ff_chunk):
        cols = slice(c * ff_chunk, (c + 1) * ff_chunk)
        gate = jnp.dot(n, wg_ref[:, cols], preferred_element_type=F32)
        up = jnp.dot(n, wu_ref[:, cols], preferred_element_type=F32)
        act = (gate * jax.nn.sigmoid(gate) * up).astype(BF16)
        acc = acc + jnp.dot(act, wd_ref[cols, :], preferred_element_type=F32)
    out_ref[...] = x + 0.5 * _rms(acc, gpost_ref[...])


def _ffn(x, g_pre, g_post, w_g, w_u, w_d, mix=None, *, tm):
    n, d = x.shape
    d_ff = w_g.shape[1]
    row = lambda w: pl.BlockSpec((tm, w), lambda i: (i, 0))
    in_specs = [row(d)]
    args = [x]
    if mix is not None:
        o1, o2, wo1, wo2, g_mix = mix
        grouped = lambda o: pl.BlockSpec((o.shape[0], tm, LANES), lambda i: (0, i, 0))
        in_specs += [row(o.shape[1]) if o.ndim == 2 else grouped(o) for o in (o1, o2)]
        in_specs += [_const_spec(wo1.shape), _const_spec(wo2.shape), _const_spec((1, d))]
        args += [o1, o2, wo1, wo2, g_mix]
    in_specs += [_const_spec((1, d)), _const_spec((1, d)), _const_spec(w_g.shape), _const_spec(w_u.shape),
                 _const_spec(w_d.shape)]
    args += [g_pre, g_post, w_g, w_u, w_d]
    return pl.pallas_call(
        functools.partial(_ffn_kernel, ff_chunk=COL_TILE, with_mix=mix is not None),
        out_shape=jax.ShapeDtypeStruct((n, d), F32),
        grid=(n // tm,),
        in_specs=in_specs,
        out_specs=row(d),
        compiler_params=pltpu.CompilerParams(dimension_semantics=("arbitrary",), vmem_limit_bytes=VMEM_LIMIT),
        name="ffn_mix" if mix is not None else "ffn",
    )(*args)


_C_CQ = 0
_C_CKV = _C_CQ + Q_LORA
_C_FQ = _C_CKV + KV_LORA
_C_FK = _C_FQ + HEADS * HEAD_DIM
_C_FV = _C_FK + HEADS * HEAD_DIM
_C_KR = _C_FV + HEADS * HEAD_DIM
_C_KRS = _C_KR + LANES
_C_GATE = _C_KRS + LANES
_C_END = _C_GATE + LANES


def _log_sigmoid(x):
    return jnp.minimum(x, 0.0) - jnp.log1p(jnp.exp(-jnp.abs(x)))


def _proj_kernel(*refs, values_t, stream_blocks):
    (h_ref, cos_ref, sin_ref, gpre_ref, win_ref, bias_ref, gq_ref, wq_ref, gkv_ref, wkv_ref) = refs[:10]
    n_in = 11 if values_t else 10
    (ckv_ref, krope_ref, fk_ref, fv_ref, logf_ref,
     qn_ref, qr_ref, kn_ref, vm_ref, kr4_ref, fqb_ref, fkb_ref, fvb_ref, lf128_ref) = refs[n_in:n_in + 14]
    nt_dims = (((1,), (1,)), ((), ()))
    hw = HEADS * HEAD_DIM
    u = _rms(h_ref[...], gpre_ref[...]).astype(BF16)
    proj = jnp.dot(u, win_ref[...], preferred_element_type=F32)
    cos = cos_ref[...]
    sin = sin_ref[...]

    cq = _rms(proj[:, _C_CQ:_C_CKV], gq_ref[...]).astype(BF16)
    q = jnp.dot(cq, wq_ref[...], preferred_element_type=F32)
    q_scale = (NOPE_DIM + ROPE_DIM) ** -0.5 * LOG2E
    rw = HEADS * ROPE_DIM
    def put(ref, x):
        if values_t:
            for g in range(x.shape[1] // LANES):
                ref[g] = x[:, g * LANES:(g + 1) * LANES]
        else:
            ref[...] = x

    put(qn_ref, (q[:, :hw] * q_scale).astype(BF16))
    put(qr_ref, ((q[:, hw:hw + rw] * cos + q[:, hw + rw:] * sin) * q_scale).astype(BF16))

    ckv = _rms(proj[:, _C_CKV:_C_FQ], gkv_ref[...])
    ckv_ref[...] = ckv
    ckv_b = ckv.astype(BF16)
    if values_t:
        put(kn_ref, jnp.dot(ckv_b, wkv_ref[:, :hw], preferred_element_type=F32).astype(BF16))
        vm_ref[0] = lax.dot_general(refs[10][...], ckv_b, nt_dims, preferred_element_type=F32).astype(BF16)
    else:
        kv = jnp.dot(ckv_b, wkv_ref[...], preferred_element_type=F32)
        kn_ref[...] = kv[:, :hw].astype(BF16)
        vm_ref[...] = kv[:, hw:].astype(BF16)
    kr4 = proj[:, _C_KR:_C_KRS] * cos[:, :LANES] + proj[:, _C_KRS:_C_GATE] * sin[:, :LANES]
    krope_ref[...] = kr4[:, :ROPE_DIM]
    kr4_ref[...] = kr4.astype(BF16)

    fq = proj[:, _C_FQ:_C_FK]
    fk = proj[:, _C_FK:_C_FV]
    fv = proj[:, _C_FV:_C_KR]
    put(fqb_ref, (fq * (HEAD_DIM ** -0.5 * LOG2E)).astype(BF16))
    if values_t:
        def by_head(x):
            heads = jnp.stack([x[:, h * HEAD_DIM:(h + 1) * HEAD_DIM] for h in range(HEADS)], axis=0)
            return jnp.swapaxes(heads, 0, 1)
        fk_ref[...] = by_head(fk)
        fv_ref[...] = by_head(fv)
    else:
        fk_ref[...] = fk
        fv_ref[...] = fv
    put(fkb_ref, fk.astype(BF16))
    if values_t:
        fvb_ref[0] = fv.T.astype(BF16)
    else:
        fvb_ref[...] = fv.astype(BF16)
    logf = _log_sigmoid(proj[:, _C_GATE:_C_END] + bias_ref[...])
    logf_ref[...] = logf[:, :HEADS]
    if values_t:
        carry_ref = refs[-1]

        @pl.when(pl.program_id(0) % stream_blocks == 0)
        def _():
            carry_ref[...] = jnp.zeros_like(carry_ref)

        lf128_ref[...] = _cumsum_split(logf, carry_ref)
    else:
        lf128_ref[...] = logf


def _proj(h, cos, sin, g_pre, w_in, bias, g_q, w_q, g_kv, w_kv, w_v_t=None, *, tm):
    n, d = h.shape
    t_blocks = cos.shape[0] // tm
    hw = HEADS * HEAD_DIM
    values_t = w_v_t is not None
    row = lambda w: pl.BlockSpec((tm, w), lambda i: (i, 0))
    tab = pl.BlockSpec((tm, HEADS * ROPE_DIM), lambda i: (i % t_blocks, 0))
    widths = [(KV_LORA, F32), (ROPE_DIM, F32), (hw, F32), (hw, F32), (HEADS, F32),
              (hw, BF16), (HEADS * ROPE_DIM, BF16), (hw, BF16), (hw, BF16), (LANES, BF16),
              (hw, BF16), (hw, BF16), (hw, BF16), (LANES, F32)]
    out_shape = [jax.ShapeDtypeStruct((n, w), dt) for w, dt in widths]
    out_specs = [row(w) for w, _ in widths]
    in_specs = [row(d), tab, tab, _const_spec((1, d)), _const_spec(w_in.shape), _const_spec((1, LANES)),
                _const_spec((1, Q_LORA)), _const_spec(w_q.shape), _const_spec((1, KV_LORA)),
                _const_spec(w_kv.shape)]
    args = [h, cos, sin, g_pre, w_in, bias, g_q, w_q, g_kv, w_kv]
    if values_t:
        for i in (8, 12):
            out_shape[i] = jax.ShapeDtypeStruct((n // tm, hw, tm), BF16)
            out_specs[i] = pl.BlockSpec((1, hw, tm), lambda i: (i, 0, 0))
        out_shape[13] = jax.ShapeDtypeStruct((n, LANES), BF16)
        for i in (5, 6, 7, 10, 11):
            groups = widths[i][0] // LANES
            out_shape[i] = jax.ShapeDtypeStruct((groups, n, LANES), BF16)
            out_specs[i] = pl.BlockSpec((groups, tm, LANES), lambda i: (0, i, 0))
        for i in (2, 3):
            out_shape[i] = jax.ShapeDtypeStruct((n, HEADS, HEAD_DIM), F32)
            out_specs[i] = pl.BlockSpec((tm, HEADS, HEAD_DIM), lambda i: (i, 0, 0))
        in_specs.append(_const_spec(w_v_t.shape))
        args.append(w_v_t)
    return pl.pallas_call(
        functools.partial(_proj_kernel, values_t=values_t, stream_blocks=t_blocks),
        out_shape=out_shape,
        grid=(n // tm,),
        in_specs=in_specs,
        out_specs=out_specs,
        scratch_shapes=[pltpu.VMEM((1, LANES), F32)] if values_t else [],
        compiler_params=pltpu.CompilerParams(dimension_semantics=("arbitrary",), vmem_limit_bytes=VMEM_LIMIT),
        name="proj",
    )(*args)


def _pastkv_kernel(ckv_ref, kr_ref, wkv_ref, kn_ref, vm_ref, kr4_ref):
    hw = HEADS * HEAD_DIM
    kv = jnp.dot(ckv_ref[...].astype(BF16), wkv_ref[...], preferred_element_type=F32)
    kn_ref[...] = kv[:, :hw].astype(BF16)
    vm_ref[...] = kv[:, hw:].astype(BF16)
    src = lax.broadcasted_iota(jnp.int32, (ROPE_DIM, LANES), 0)
    dst = lax.broadcasted_iota(jnp.int32, (ROPE_DIM, LANES), 1)
    rep = (dst % ROPE_DIM == src).astype(BF16)
    kr4_ref[...] = jnp.dot(kr_ref[...].astype(BF16), rep, preferred_element_type=F32).astype(BF16)


def _pastkv(ckv, krope, w_kv, *, tm):
    n = ckv.shape[0]
    hw = HEADS * HEAD_DIM
    row = lambda w: pl.BlockSpec((tm, w), lambda i: (i, 0))
    return pl.pallas_call(
        _pastkv_kernel,
        out_shape=[jax.ShapeDtypeStruct((n, hw), BF16), jax.ShapeDtypeStruct((n, hw), BF16),
                   jax.ShapeDtypeStruct((n, LANES), BF16)],
        grid=(n // tm,),
        in_specs=[row(KV_LORA), row(ROPE_DIM), _const_spec(w_kv.shape)],
        out_specs=[row(hw), row(hw), row(LANES)],
        compiler_params=pltpu.CompilerParams(dimension_semantics=("arbitrary",), vmem_limit_bytes=VMEM_LIMIT),
        name="pastkv",
    )(ckv, krope, w_kv)


def _split3(y, lane):
    hi = y.astype(BF16).astype(F32)
    r1 = y - hi
    mid = r1.astype(BF16).astype(F32)
    lo = r1 - mid
    j = (lane - LOGF_LANE0) % 3
    sel = jnp.where(j == 0, hi, jnp.where(j == 1, mid, lo))
    used = (lane >= LOGF_LANE0) & (lane < LOGF_LANE0 + 3 * HEADS)
    return jnp.where(used, sel, 0.0).astype(BF16)


def _cumsum_split(x, carry_ref):
    tc = x.shape[0]
    hi = x.astype(BF16)
    r1 = x - hi.astype(F32)
    mid = r1.astype(BF16)
    lo = (r1 - mid.astype(F32)).astype(BF16)
    r = lax.broadcasted_iota(jnp.int32, (tc, tc), 0)
    c = lax.broadcasted_iota(jnp.int32, (tc, tc), 1)
    tri = (c <= r).astype(BF16)
    cum = (jnp.dot(tri, hi, preferred_element_type=F32) + jnp.dot(tri, mid, preferred_element_type=F32)
           + jnp.dot(tri, lo, preferred_element_type=F32)) + carry_ref[...]
    carry_ref[...] = cum[tc - 1:tc, :]
    lane = lax.broadcasted_iota(jnp.int32, (tc, LANES), 1)
    return _split3(cum * (-LOG2E), lane)


def _lsplit_kernel(x_ref, o_ref, carry_ref):
    @pl.when(pl.program_id(1) == 0)
    def _():
        carry_ref[...] = jnp.zeros_like(carry_ref)

    o_ref[0] = _cumsum_split(x_ref[0], carry_ref)


def _lsplit(lf128, *, tc):
    b, t, _ = lf128.shape
    spec = pl.BlockSpec((1, tc, LANES), lambda i, j: (i, j, 0))
    return pl.pallas_call(
        _lsplit_kernel,
        out_shape=jax.ShapeDtypeStruct((b, t, LANES), BF16),
        grid=(b, t // tc),
        in_specs=[spec],
        out_specs=spec,
        scratch_shapes=[pltpu.VMEM((1, LANES), F32)],
        compiler_params=pltpu.CompilerParams(dimension_semantics=("arbitrary", "arbitrary")),
        name="lsplit",
    )(lf128)


def _attn_sample_kernel(*refs, past, fox, latent=False):
    n_q = 1 if fox else 2
    q1_ref = refs[0]
    q2_ref = None if fox else refs[1]
    if latent:
        ckv_ref, kr_ref, wk_ref, wv_ref, k1n_ref, k2n_ref, vn_ref, o_ref = refs[n_q:]
        ckv = ckv_ref[0].astype(BF16)
        k1p = jnp.dot(ckv, wk_ref[...], preferred_element_type=F32).astype(BF16)
        vp = jnp.dot(ckv, wv_ref[...], preferred_element_type=F32).astype(BF16)
        src = lax.broadcasted_iota(jnp.int32, (ROPE_DIM, LANES), 0)
        dst = lax.broadcasted_iota(jnp.int32, (ROPE_DIM, LANES), 1)
        rep = (dst % ROPE_DIM == src).astype(BF16)
        k2p = jnp.dot(kr_ref[0].astype(BF16), rep, preferred_element_type=F32).astype(BF16)
    else:
        k1p_ref, k2p_ref, vp_ref, k1n_ref, k2n_ref, vn_ref, o_ref = refs[n_q:]
        k1p, k2p, vp = k1p_ref[0].astype(BF16), k2p_ref[0].astype(BF16), vp_ref[0].astype(BF16)
    ts = q1_ref.shape[1]
    p = pl.program_id(1)
    lane = lax.broadcasted_iota(jnp.int32, (ts, LANES), 1)
    q1 = q1_ref[0].astype(F32)
    q2 = jnp.ones_like(q1) if fox else q2_ref[0].astype(F32)

    def head_rows(a):
        main = jnp.where((lane >= a * HEAD_DIM) & (lane < (a + 1) * HEAD_DIM), q1, 0.0)
        if fox:
            lo, width = LOGF_LANE0 + 3 * (2 * p + a), 3
        else:
            lo, width = ROPE_DIM * (2 * (p % 2) + a), ROPE_DIM
        aux = jnp.where((lane >= lo) & (lane < lo + width), q2, 0.0)
        return jnp.concatenate([main, aux], axis=1).astype(BF16)

    q = jnp.concatenate([head_rows(0), head_rows(1)], axis=0)
    nt_dims = (((1,), (1,)), ((), ()))

    def logits(k1, k2):
        return lax.dot_general(q, jnp.concatenate([k1, k2], axis=1), nt_dims, preferred_element_type=F32)

    s_p = logits(k1p, k2p)
    s_n = logits(k1n_ref[0].astype(BF16), k2n_ref[0].astype(BF16))
    r = lax.broadcasted_iota(jnp.int32, s_n.shape, 0)
    t_pos = past + jnp.where(r >= ts, r - ts, r)
    s_pos = past + lax.broadcasted_iota(jnp.int32, s_n.shape, 1)
    vis = (s_pos <= t_pos) if fox else ((s_pos // CHUNK) <= (t_pos // CHUNK))
    s_n = jnp.where(vis, s_n, NEG)
    m = jnp.maximum(jnp.max(s_p, axis=1, keepdims=True), jnp.max(s_n, axis=1, keepdims=True))
    p_p = jnp.exp2(s_p - m)
    p_n = jnp.exp2(s_n - m)
    denom = jnp.sum(p_p, axis=1, keepdims=True) + jnp.sum(p_n, axis=1, keepdims=True)
    out = (jnp.dot(p_p.astype(BF16), vp, preferred_element_type=F32)
           + jnp.dot(p_n.astype(BF16), vn_ref[0].astype(BF16), preferred_element_type=F32)) / denom
    o_ref[0] = jnp.where(lane < HEAD_DIM, out[:ts], out[ts:]).astype(o_ref.dtype)


def _attn_sample(q1, q2, k1p, k2p, vp, k1n, k2n, vn, *, k2n_row0=0, fox):
    b, ts, hw = q1.shape
    past = k1p.shape[1]
    assert k2n_row0 % ts == 0
    pairs = hw // LANES
    qspec = pl.BlockSpec((1, ts, LANES), lambda bi, p: (bi, 0, p))
    pspec = pl.BlockSpec((1, past, LANES), lambda bi, p: (bi, 0, p))
    p2spec = pl.BlockSpec((1, past, LANES), lambda bi, p: (bi, 0, 0))
    n2spec = pl.BlockSpec((1, ts, LANES), lambda bi, p: (bi, k2n_row0 // ts, 0))
    in_specs = [pspec, p2spec, pspec, qspec, n2spec, qspec]
    args = (k1p, k2p, vp, k1n, k2n, vn)
    if fox:
        in_specs, args = [qspec] + in_specs, (q1,) + args
    else:
        q2spec = pl.BlockSpec((1, ts, LANES), lambda bi, p: (bi, 0, p // 2))
        in_specs, args = [qspec, q2spec] + in_specs, (q1, q2) + args
    return pl.pallas_call(
        functools.partial(_attn_sample_kernel, past=past, fox=fox),
        out_shape=jax.ShapeDtypeStruct((b, ts, hw), BF16),
        grid=(b, pairs),
        in_specs=in_specs,
        out_specs=qspec,
        compiler_params=pltpu.CompilerParams(dimension_semantics=("arbitrary",) * 2,
                                             vmem_limit_bytes=VMEM_LIMIT),
        name="attn_sample_fox" if fox else "attn_sample_mla",
    )(*args)


def _attn_sample_latent(q1, q2, ckv, krope, w_kv, k1n, k2n, vn):
    b, ts, hw = q1.shape
    past = ckv.shape[1]
    pairs = hw // LANES
    qspec = pl.BlockSpec((1, ts, LANES), lambda bi, p: (bi, 0, p))
    q2spec = pl.BlockSpec((1, ts, LANES), lambda bi, p: (bi, 0, p // 2))
    n2spec = pl.BlockSpec((1, ts, LANES), lambda bi, p: (bi, 0, 0))
    in_specs = [qspec, q2spec,
                pl.BlockSpec((1, past, KV_LORA), lambda bi, p: (bi, 0, 0)),
                pl.BlockSpec((1, past, ROPE_DIM), lambda bi, p: (bi, 0, 0)),
                pl.BlockSpec((KV_LORA, LANES), lambda bi, p: (0, p)),
                pl.BlockSpec((KV_LORA, LANES), lambda bi, p: (0, pairs + p)),
                qspec, n2spec, qspec]
    return pl.pallas_call(
        functools.partial(_attn_sample_kernel, past=past, fox=False, latent=True),
        out_shape=jax.ShapeDtypeStruct((b, ts, hw), BF16),
        grid=(b, pairs),
        in_specs=in_specs,
        out_specs=qspec,
        compiler_params=pltpu.CompilerParams(dimension_semantics=("arbitrary",) * 2,
                                             vmem_limit_bytes=VMEM_LIMIT),
        name="attn_sample_latent",
    )(q1, q2, ckv, krope, w_kv, w_kv, k1n, k2n, vn)


class _Item:
    def __init__(self, q, kb, qpar, kpar, first=False, last=False):
        self.q, self.kb, self.qpar, self.kpar, self.first, self.last = q, kb, qpar, kpar, first, last
        self.slot = 2 * qpar + kpar


def _attn_flat_kernel(*refs, tq, nq, fox):
    n_in = 4 if fox else 5
    q1_ref = refs[0]
    q2_ref = None if fox else refs[1]
    k1_ref, k2_ref, vt_ref, o_ref = refs[n_in - 3:n_in + 1]
    scratch = refs[n_in + 1:]
    qt_refs, m_refs, acc_refs = scratch[0:2], scratch[2:4], scratch[4:6]
    s_refs, p_refs, a_refs, bm_refs = scratch[6:10], scratch[10:14], scratch[14:18], scratch[18:22]
    tk = tq
    pair = pl.program_id(1)
    col_tiles = [slice(j, j + COL_TILE) for j in range(0, 2 * tq, COL_TILE)]
    ones_rows = jnp.ones((ONES_ROWS, tk), BF16)

    def rows_of(blk):
        start = blk * tq
        return pl.ds(start if isinstance(start, int) else pl.multiple_of(start, tq), tq)

    def setup(q, qpar):
        lane = lax.broadcasted_iota(jnp.int32, (tq, LANES), 1)
        q1 = q1_ref[rows_of(q), :].astype(F32)
        q2 = jnp.ones_like(q1) if fox else q2_ref[rows_of(q), :].astype(F32)

        def head_rows(a):
            main = jnp.where((lane >= a * HEAD_DIM) & (lane < (a + 1) * HEAD_DIM), q1, 0.0)
            if fox:
                lo, width = LOGF_LANE0 + 3 * (2 * pair + a), 3
            else:
                lo, width = ROPE_DIM * (2 * (pair % 2) + a), ROPE_DIM
            aux = jnp.where((lane >= lo) & (lane < lo + width), q2, 0.0)
            return jnp.concatenate([main, aux], axis=1)

        qt_refs[qpar][...] = jnp.concatenate([head_rows(0), head_rows(1)], axis=0).T.astype(BF16)

    def key_block(kb):
        return jnp.concatenate([k1_ref[rows_of(kb), :], k2_ref[rows_of(kb), :]], axis=1)

    def scores(x, cols):
        s = jnp.dot(key_block(x.kb), qt_refs[x.qpar][:, cols], preferred_element_type=F32)
        if x.last:
            c = (cols.start % tq) + lax.broadcasted_iota(jnp.int32, (1, COL_TILE), 1)
            r = lax.broadcasted_iota(jnp.int32, s.shape, 0)
            s = jnp.where(r <= (c if fox else (c | (CHUNK - 1))), s, NEG)
        s_refs[x.slot][cols.start // COL_TILE] = s
        bm_refs[x.slot][:, cols] = jnp.max(s, axis=0, keepdims=True)

    def softmax(x, cols):
        bm = bm_refs[x.slot][:, cols]
        if x.first:
            m_new = bm
            a_refs[x.slot][:, cols] = jnp.zeros_like(bm)
        else:
            m_prev = m_refs[x.qpar][:, cols]
            m_new = jnp.maximum(m_prev, bm)
            a_refs[x.slot][:, cols] = jnp.exp2(m_prev - m_new)
        m_refs[x.qpar][:, cols] = m_new
        ct = cols.start // COL_TILE
        p_refs[x.slot][ct] = jnp.exp2(s_refs[x.slot][ct] - m_new).astype(BF16)

    def values(x, cols):
        head = cols.start // tq
        v_t = vt_ref[x.kb, head * HEAD_DIM:(head + 1) * HEAD_DIM, :]
        lhs = jnp.concatenate([v_t, ones_rows], axis=0)
        pv = jnp.dot(lhs, p_refs[x.slot][cols.start // COL_TILE], preferred_element_type=F32)[:ACC_ROWS]
        acc_ref = acc_refs[x.qpar]
        if x.first:
            acc_ref[:, cols] = pv
        else:
            acc_ref[:, cols] = a_refs[x.slot][:, cols] * acc_ref[:, cols] + pv

    def finalize(x):
        acc = acc_refs[x.qpar][...]
        out_t = jnp.concatenate([acc[:HEAD_DIM, :tq] / acc[HEAD_DIM:HEAD_DIM + 1, :tq],
                                 acc[:HEAD_DIM, tq:] / acc[HEAD_DIM:HEAD_DIM + 1, tq:]], axis=0)
        o_ref[rows_of(x.q), :] = out_t.T.astype(o_ref.dtype)

    def tick(xs, xm, xv, next_q=None):
        for cols in col_tiles:
            if xs is not None:
                scores(xs, cols)
            if xv is not None:
                values(xv, cols)
            if xm is not None:
                softmax(xm, cols)
        if xv is not None and xv.last:
            finalize(xv)
        if next_q is not None:
            setup(*next_q)

    def generic_pairs(q, qpar, count):
        def body(i, carry):
            kb = 3 + 2 * i
            x_a, x_b = _Item(q, kb, qpar, 1), _Item(q, kb + 1, qpar, 0)
            tick(x_a, _Item(q, kb - 1, qpar, 0), _Item(q, kb - 2, qpar, 1))
            tick(x_b, x_a, _Item(q, kb - 1, qpar, 0))
            return carry
        lax.fori_loop(0, count, body, 0)

    def query_block(q, qpar, is_last_q=False):
        p2 = _Item(q - 1, q - 2, 1 - qpar, qpar)
        p1 = _Item(q - 1, q - 1, 1 - qpar, 1 - qpar, last=True)
        x0 = _Item(q, 0, qpar, 0, first=True)
        x1 = _Item(q, 1, qpar, 1)
        x2 = _Item(q, 2, qpar, 0)
        tick(x0, p1, p2)
        tick(x1, x0, p1)
        tick(x2, x1, x0)
        nxt = None if is_last_q else (q + 1, 1 - qpar)
        if qpar == 1:
            generic_pairs(q, qpar, (q - 3) // 2)
            a2, a1 = _Item(q, q - 2, qpar, 1), _Item(q, q - 1, qpar, 0)
        else:
            generic_pairs(q, qpar, (q - 4) // 2)
            a1 = _Item(q, q - 1, qpar, 1)
            a2 = _Item(q, q - 2, qpar, 0)
            tick(a1, a2, _Item(q, q - 3, qpar, 1))
        xl = _Item(q, q, qpar, qpar, last=True)
        tick(xl, a1, a2, next_q=nxt)
        return a1, xl

    setup(0, 0)
    x00 = _Item(0, 0, 0, 0, first=True, last=True)
    x10, x11 = _Item(1, 0, 1, 0, first=True), _Item(1, 1, 1, 1, last=True)
    x20, x21, x22 = _Item(2, 0, 0, 0, first=True), _Item(2, 1, 0, 1), _Item(2, 2, 0, 0, last=True)
    tick(x00, None, None, next_q=(1, 1))
    tick(x10, x00, None)
    tick(x11, x10, x00, next_q=(2, 0))
    tick(x20, x11, x10)
    tick(x21, x20, x11)
    tick(x22, x21, x20, next_q=(3, 1))

    def block_pair(j, carry):
        query_block(3 + 2 * j, 1)
        query_block(4 + 2 * j, 0)
        return carry

    lax.fori_loop(0, (nq - 4) // 2, block_pair, 0)
    a1, xl = query_block(nq - 1, 1, is_last_q=True)
    tick(None, xl, a1)
    tick(None, None, xl)


def _attn_flat(q1, q2, k1, k2, v_t, *, tq, fox):
    pairs, b, t, _ = q1.shape
    hw = pairs * LANES
    nq = t // tq
    assert t % tq == 0 and nq % 2 == 0 and nq >= 4, (t, tq)
    assert v_t.shape == (b, nq, hw, tq), v_t.shape
    m2 = 2 * tq
    spec = pl.BlockSpec((None, None, t, LANES), lambda bi, p: (p, bi, 0, 0))
    k2spec = pl.BlockSpec((None, t, LANES), lambda bi, p: (bi, 0, 0))
    vspec = pl.BlockSpec((None, nq, LANES, tq), lambda bi, p: (bi, 0, p, 0))
    if fox:
        in_specs = [spec, spec, k2spec, vspec]
        args = (q1, k1, k2, v_t)
    else:
        q2spec = pl.BlockSpec((None, None, t, LANES), lambda bi, p: (p // 2, bi, 0, 0))
        in_specs = [spec, q2spec, spec, k2spec, vspec]
        args = (q1, q2, k1, k2, v_t)
    vmem = lambda shape, dt, n: [pltpu.VMEM(shape, dt) for _ in range(n)]
    return pl.pallas_call(
        functools.partial(_attn_flat_kernel, tq=tq, nq=nq, fox=fox),
        out_shape=jax.ShapeDtypeStruct((pairs, b, t, LANES), BF16),
        grid=(b, pairs),
        in_specs=in_specs,
        out_specs=spec,
        scratch_shapes=(vmem((2 * LANES, m2), BF16, 2)
                        + vmem((1, m2), F32, 2)
                        + vmem((ACC_ROWS, m2), F32, 2)
                        + vmem((m2 // COL_TILE, tq, COL_TILE), F32, 4)
                        + vmem((m2 // COL_TILE, tq, COL_TILE), BF16, 4)
                        + vmem((1, m2), F32, 4)
                        + vmem((1, m2), F32, 4)),
        compiler_params=pltpu.CompilerParams(dimension_semantics=("arbitrary",) * 2,
                                             vmem_limit_bytes=VMEM_LIMIT),
        name="attn_flat_fox" if fox else "attn_flat_mla",
    )(*args)


def _rope_tables(pos):
    half = ROPE_DIM // 2
    freqs = ROPE_THETA ** (-jnp.arange(half, dtype=F32) / half)
    ang = pos.astype(F32)[:, None] * freqs[None, :]
    cos, sin = jnp.cos(ang), jnp.sin(ang)
    cos_t = jnp.tile(jnp.concatenate([cos, cos], axis=1), (1, HEADS))
    sin_t = jnp.tile(jnp.concatenate([-sin, sin], axis=1), (1, HEADS))
    return cos_t, sin_t


def _swap_halves(w):
    half = w.shape[-1] // 2
    return jnp.concatenate([w[..., half:], w[..., :half]], axis=-1)


def _prep_weights(w_in, b_forget, w_q_up, w_kv_up, w_out):
    d = w_in.shape[0]
    hw = HEADS * HEAD_DIM
    o = 0
    cq = w_in[:, o:o + Q_LORA]; o += Q_LORA
    ckv = w_in[:, o:o + KV_LORA]; o += KV_LORA
    kr = w_in[:, o:o + ROPE_DIM]; o += ROPE_DIM
    fq = w_in[:, o:o + hw]; o += hw
    fk = w_in[:, o:o + hw]; o += hw
    fv = w_in[:, o:o + hw]; o += hw
    gate = w_in[:, o:o + HEADS]
    reps = LANES // ROPE_DIM
    pad = LANES - 4 * HEADS
    gate128 = jnp.concatenate([gate, jnp.repeat(gate, 3, axis=1), jnp.zeros((d, pad), w_in.dtype)], axis=1)
    w_in_p = jnp.concatenate([cq, ckv, fq, fk, fv, jnp.tile(kr, (1, reps)), jnp.tile(_swap_halves(kr), (1, reps)),
                              gate128], axis=1).astype(BF16)
    bias128 = jnp.concatenate([b_forget, jnp.repeat(b_forget, 3), jnp.zeros((pad,), F32)])[None, :]
    wq = w_q_up.reshape(Q_LORA, HEADS, NOPE_DIM + ROPE_DIM)
    wq_rope = wq[:, :, NOPE_DIM:]
    w_q_p = jnp.concatenate([wq[:, :, :NOPE_DIM].reshape(Q_LORA, -1), wq_rope.reshape(Q_LORA, -1),
                             _swap_halves(wq_rope).reshape(Q_LORA, -1)], axis=1).astype(BF16)
    wkv = w_kv_up.reshape(KV_LORA, HEADS, NOPE_DIM + HEAD_DIM)
    w_kv_p = jnp.concatenate([wkv[:, :, :NOPE_DIM].reshape(KV_LORA, -1), wkv[:, :, NOPE_DIM:].reshape(KV_LORA, -1)],
                             axis=1).astype(BF16)
    w_o1 = w_out[:hw].astype(BF16)
    w_o2 = w_out[hw:].astype(BF16)
    return w_in_p, bias128, w_q_p, w_kv_p, w_o1, w_o2


def _expand_logf(lf):
    pad = LANES - 4 * HEADS
    return jnp.concatenate([lf, jnp.repeat(lf, 3, axis=-1), jnp.zeros(lf.shape[:-1] + (pad,), lf.dtype)], axis=-1)


def kernel(x_prompt, x_sample, cache_mla_ckv, cache_mla_krope, cache_fox_k, cache_fox_v, cache_fox_logf,
           g_ffn1_pre, g_ffn1_post, w_ffn1_gu, w_ffn1_down, g_mix_pre, g_mix_post, w_in, b_forget,
           g_q_latent, w_q_up, g_kv_latent, w_kv_up, w_out, g_ffn2_pre, g_ffn2_post, w_ffn2_gu, w_ffn2_down):
    depth = w_in.shape[0]
    bp, tp, d = x_prompt.shape
    bs, ts, _ = x_sample.shape
    past = cache_mla_ckv.shape[2]
    hw = HEADS * HEAD_DIM
    tq_p = tc_s = ROW_TILE
    tm_p = math.gcd(bp * tp, 2 * ROW_TILE)
    tm_s = bs * ts
    tk_pad = -(-(past + ts) // tc_s) * tc_s

    cos_p, sin_p = _rope_tables(jnp.arange(tp))
    cos_s, sin_s = _rope_tables(past + jnp.arange(ts))
    cos_s, sin_s = jnp.tile(cos_s, (bs, 1)), jnp.tile(sin_s, (bs, 1))

    xp = x_prompt.reshape(bp * tp, d)
    xs = x_sample.reshape(bs * ts, d)
    rows_p, rows_s = [], []

    def pad_keys(parts):
        n = sum(a.shape[1] for a in parts)
        parts = list(parts) + [jnp.zeros((bs, tk_pad - n, parts[0].shape[2]), parts[0].dtype)]
        return jnp.concatenate(parts, axis=1)

    for l in range(depth):
        d_ff = w_ffn1_down.shape[1]
        w1 = (w_ffn1_gu[l][:, :d_ff].astype(BF16), w_ffn1_gu[l][:, d_ff:].astype(BF16), w_ffn1_down[l].astype(BF16))
        w2 = (w_ffn2_gu[l][:, :d_ff].astype(BF16), w_ffn2_gu[l][:, d_ff:].astype(BF16), w_ffn2_down[l].astype(BF16))
        w_in_p, bias128, w_q_p, w_kv_p, w_o1, w_o2 = _prep_weights(w_in[l], b_forget[l], w_q_up[l], w_kv_up[l], w_out[l])
        g1 = (g_ffn1_pre[l][None, :], g_ffn1_post[l][None, :])
        g2 = (g_ffn2_pre[l][None, :], g_ffn2_post[l][None, :])
        gm_pre, gm_post = g_mix_pre[l][None, :], g_mix_post[l][None, :]
        gq, gkv = g_q_latent[l][None, :], g_kv_latent[l][None, :]

        hp = _ffn(xp, *g1, *w1, tm=tm_p)
        (ckv, krope, fk, fv, logf, qn, qr, kn, vm_t, kr4, fqb, fkb, fv_t, ls) = _proj(
            hp, cos_p, sin_p, gm_pre, w_in_p, bias128, gq, w_q_p, gkv, w_kv_p, w_kv_p[:, hw:].T, tm=tq_p)
        sh = lambda a: a.reshape(bp, tp, a.shape[-1])
        sh_g = lambda a: a.reshape(a.shape[0], bp, tp, LANES)
        sh_t = lambda a: a.reshape(bp, tp // tq_p, hw, tq_p)
        o_mla = _attn_flat(sh_g(qn), sh_g(qr), sh_g(kn), sh(kr4), sh_t(vm_t), tq=tq_p, fox=False)
        o_fox = _attn_flat(sh_g(fqb), None, sh_g(fkb), sh(ls), sh_t(fv_t), tq=tq_p, fox=True)
        un_g = lambda o: o.reshape(o.shape[0], bp * tp, LANES)
        xp = _ffn(hp, *g2, *w2, mix=(un_g(o_mla), un_g(o_fox), w_o1, w_o2, gm_post), tm=tm_p)
        rows_p.append((ckv.reshape(bp, tp, KV_LORA), krope.reshape(bp, tp, ROPE_DIM),
                       fk.reshape(bp, tp, HEADS, HEAD_DIM), fv.reshape(bp, tp, HEADS, HEAD_DIM),
                       logf.reshape(bp, tp, HEADS)))

        hs = _ffn(xs, *g1, *w1, tm=tm_s)
        (ckv, krope, fk, fv, logf, qn, qr, kn, vm, kr4, fqb, _, _, lf128) = _proj(
            hs, cos_s, sin_s, gm_pre, w_in_p, bias128, gq, w_q_p, gkv, w_kv_p, tm=tm_s)
        sh = lambda a: a.reshape(bs, -1, a.shape[-1])
        lf_all = pad_keys([_expand_logf(cache_fox_logf[l].astype(F32)), sh(lf128)])
        ls = _lsplit(lf_all, tc=tc_s)
        o_mla = _attn_sample_latent(sh(qn), sh(qr), cache_mla_ckv[l], cache_mla_krope[l], w_kv_p,
                                    sh(kn), sh(kr4), sh(vm))
        o_fox = _attn_sample(sh(fqb), None, cache_fox_k[l].reshape(bs, past, hw), ls,
                             cache_fox_v[l].reshape(bs, past, hw), sh(fk), ls, sh(fv),
                             k2n_row0=past, fox=True)
        xs = _ffn(hs, *g2, *w2, mix=(o_mla.reshape(-1, hw), o_fox.reshape(-1, hw), w_o1, w_o2, gm_post), tm=tm_s)
        rows_s.append((ckv.reshape(bs, ts, KV_LORA), krope.reshape(bs, ts, ROPE_DIM),
                       fk.reshape(bs, ts, HEADS, HEAD_DIM), fv.reshape(bs, ts, HEADS, HEAD_DIM),
                       logf.reshape(bs, ts, HEADS)))

    outs_p = [jnp.stack([r[i] for r in rows_p]) for i in range(5)]
    outs_s = [jnp.stack([r[i] for r in rows_s]) for i in range(5)]
    return (xp.reshape(bp, tp, d), xs.reshape(bs, ts, d), *outs_p, *outs_s)
```
